```python
import numpy as np
import jax
import jax.numpy as jnp
from jax import lax

D_MODEL = 2048
BATCH = 8
SEQ = 2048
DEPTH = 4

CTX_LEN = 256
GRID_W = 64

RW_HEADS = 16
RW_HEAD_DIM = 64
RW_WIDTH = RW_HEADS * RW_HEAD_DIM
DECAY_LORA = 64
ICLR_LORA = 64
GATE_LORA = 160
GN_EPS = 64e-5
NA_HEADS = 16
NA_HEAD_DIM = 64
NA_WIDTH = NA_HEADS * NA_HEAD_DIM
NA_WIN_ROWS = 8
NA_WIN_COLS = 16
NA_QCB = 16
NA_KCB = 32
SC_WIDTH = 1024
SC_TAPS = 3
N_BRANCH = 3
RW_COLS = 3 * RW_WIDTH + 2 * DECAY_LORA + 2 * ICLR_LORA + GATE_LORA
NA_IN_COLS = 3 * NA_WIDTH
SC_IN_COLS = 3 * SC_WIDTH
GATE_COLS = N_BRANCH * D_MODEL
IN_COLS = RW_COLS + NA_IN_COLS + SC_IN_COLS + GATE_COLS
N_EXPERTS = 32
TOP_K = 4
D_EXPERT = 512
SWIGLU_ALPHA = 1.702
SWIGLU_LIMIT = 7.0
MOE_BLOCK = 256
ROPE_BASE = 10000.0
LN_EPS = 1e-6
NEG_INF = -1e30
DEEPNORM_ALPHA = (2 * DEPTH) ** 0.25
DEEPNORM_BETA = (8 * DEPTH) ** -0.25

kernel_name = 'hybrid_rwkv7_natten_shortconv_moe_dit'


def _layernorm(x):
    x32 = x.astype(jnp.float32)
    mu = jnp.mean(x32, -1, keepdims=True)
    var = jnp.mean(jnp.square(x32 - mu), -1, keepdims=True)
    return ((x32 - mu) * lax.rsqrt(var + LN_EPS)).astype(x.dtype)


def _shift(u, direction):
    pad = jnp.zeros_like(u[:, :1])
    if direction > 0:
        return jnp.concatenate([pad, u[:, :-1]], axis=1)
    return jnp.concatenate([u[:, 1:], pad], axis=1)


def _modulation(cvec, w_mod, b_mod):
    m = jax.nn.silu(cvec) @ w_mod + b_mod
    return jnp.split(m, 6, axis=-1)


def _axial_rope(x):
    T, hd = x.shape[1], x.shape[-1]
    n_freq = hd // 4
    t = jnp.arange(T)
    inv = jnp.power(ROPE_BASE, -jnp.arange(n_freq, dtype=jnp.float32) / n_freq)
    row = (t // GRID_W).astype(jnp.float32)
    col = (t % GRID_W).astype(jnp.float32)
    ang = jnp.concatenate([row[:, None] * inv, col[:, None] * inv], -1)
    shape = (1, T) + (1,) * (x.ndim - 3) + (hd // 2,)
    cos = jnp.cos(ang).reshape(shape).astype(x.dtype)
    sin = jnp.sin(ang).reshape(shape).astype(x.dtype)
    x1, x2 = x[..., :hd // 2], x[..., hd // 2:]
    return jnp.concatenate([x1 * cos - x2 * sin, x2 * cos + x1 * sin], -1)


def _rwkv_prepare(p, mu, w0, w_lora, a0, a_lora, g_lora, k_k, k_a, rotate):
    p = p.astype(jnp.float32)
    B, T, _ = p.shape
    C, H, N = RW_WIDTH, RW_HEADS, RW_HEAD_DIM
    p = p + mu[0] * (_shift(p, 1) - p) + mu[1] * (_shift(p, -1) - p)
    r, k, v = p[..., :C], p[..., C:2 * C], p[..., 2 * C:3 * C]
    o = 3 * C
    wl = p[..., o:o + 2 * DECAY_LORA].reshape(B, T, 2, DECAY_LORA)
    o += 2 * DECAY_LORA
    al = p[..., o:o + 2 * ICLR_LORA].reshape(B, T, 2, ICLR_LORA)
    o += 2 * ICLR_LORA
    gl = p[..., o:o + GATE_LORA]
    w_raw = w0 + jnp.einsum('btdr,drc->btdc', jnp.tanh(wl), w_lora)
    decay = jnp.exp(-jnp.exp(-jax.nn.softplus(-w_raw) - 0.5))
    a = jax.nn.sigmoid(a0 + jnp.einsum('btdr,drc->btdc', al, a_lora))
    g = jax.nn.sigmoid(gl) @ g_lora
    kk = (k * k_k).reshape(B, T, H, N)
    kk = kk / jnp.maximum(jnp.sqrt(jnp.sum(kk * kk, -1, keepdims=True)), 1e-12)
    kd = (k[:, :, None, :] * (1.0 + (a - 1.0) * k_a)).reshape(B, T, 2, H, N)
    r = r.reshape(B, T, H, N)
    v = v.reshape(B, T, H, N)
    decay = decay.reshape(B, T, 2, H, N)
    a = a.reshape(B, T, 2, H, N)
    if rotate:
        r_s, kd_s, kk_s = _axial_rope(r), _axial_rope(kd), _axial_rope(kk)
    else:
        r_s, kd_s, kk_s = r, kd, kk
    scan_f = (r_s, decay[:, :, 0], kd_s[:, :, 0], v, kk_s, a[:, :, 0])
    scan_b = (r_s, decay[:, :, 1], kd_s[:, :, 1], v, kk_s, a[:, :, 1])
    return scan_f, scan_b, r, jnp.sum(kd, axis=2), v, g


def _wkv_scan(state0, r, w, k, v, kk, a, reverse):
    xs = tuple(jnp.moveaxis(t, 1, 0) for t in (r, w, k, v, kk, a))

    def step(S, inp):
        r_t, w_t, k_t, v_t, kk_t, a_t = inp
        s_kk = jnp.einsum('bhvk,bhk->bhv', S, kk_t)
        S = (S * w_t[:, :, None, :] - s_kk[..., None] * (kk_t * a_t)[:, :, None, :]
             + v_t[..., None] * k_t[:, :, None, :])
        return S, jnp.einsum('bhvk,bhk->bhv', S, r_t)

    S, out = lax.scan(step, state0, xs, reverse=reverse)
    return S, jnp.moveaxis(out, 0, 1)


def _rwkv_out(wkv, r, k_sum, v, g, r_k, gn_w, gn_b, out_dtype):
    B, T, H, N = wkv.shape
    mu = jnp.mean(wkv, -1, keepdims=True)
    var = jnp.mean(jnp.square(wkv - mu), -1, keepdims=True)
    y = ((wkv - mu) * lax.rsqrt(var + GN_EPS)).reshape(B, T, H * N) * gn_w + gn_b
    bonus = jnp.sum(r * k_sum * r_k, -1, keepdims=True) * v
    return ((y + bonus.reshape(B, T, H * N)) * g).astype(out_dtype)


def _na_heads(p):
    B, T, _ = p.shape
    return [p[..., i * NA_WIDTH:(i + 1) * NA_WIDTH].reshape(B, T, NA_HEADS, NA_HEAD_DIM) for i in range(3)]


def _na_static():
    n_cb = GRID_W // NA_QCB
    qcol = np.arange(GRID_W).reshape(n_cb, NA_QCB)
    kstart = np.clip(np.arange(n_cb) * NA_QCB - NA_WIN_COLS // 2, 0, GRID_W - NA_KCB)
    kcol = kstart[:, None] + np.arange(NA_KCB)[None, :]
    wstart = np.clip(qcol - NA_WIN_COLS // 2, 0, GRID_W - NA_WIN_COLS)
    inside = ((kcol[:, None, :] >= wstart[:, :, None])
              & (kcol[:, None, :] < wstart[:, :, None] + NA_WIN_COLS))
    dc = np.clip(kcol[:, None, :] - qcol[:, :, None] + NA_WIN_COLS - 1, 0, 2 * NA_WIN_COLS - 2)
    return kcol, inside, dc


def _na_latent(q, k, v, kc, vc, rpb):
    B, S, H, hd = q.shape
    rows = S // GRID_W
    kr = min(NA_WIN_ROWS, rows)
    n_cb = GRID_W // NA_QCB
    kcol, inside, dc = _na_static()
    scale = hd ** -0.5
    qr = jnp.moveaxis(q.reshape(B, rows, n_cb, NA_QCB, H, hd), 1, 0)
    kg = k.reshape(B, rows, GRID_W, H, hd)
    vg = v.reshape(B, rows, GRID_W, H, hd)
    r_idx = jnp.arange(rows)
    r_start = jnp.clip(r_idx - kr // 2, 0, rows - kr)

    def one_row(args):
        qb, r, rs = args
        kb = lax.dynamic_slice_in_dim(kg, rs, kr, axis=1)[:, :, kcol]
        vb = lax.dynamic_slice_in_dim(vg, rs, kr, axis=1)[:, :, kcol]
        s_win = jnp.einsum('bnqhd,bjnkhd->bhnqjk', qb, kb).astype(jnp.float32) * scale
        dr = rs + jnp.arange(kr) - r + NA_WIN_ROWS - 1
        bias = rpb[:, dr[:, None, None, None], dc[None]]
        bias = jnp.transpose(bias, (0, 2, 3, 1, 4)).astype(jnp.float32)
        s_win = jnp.where(inside[:, :, None, :], s_win + bias, NEG_INF)
        s_ctx = jnp.einsum('bnqhd,blhd->bhnql', qb, kc).astype(jnp.float32) * scale
        s = jnp.concatenate([s_win.reshape(B, H, n_cb, NA_QCB, kr * NA_KCB), s_ctx], -1)
        p = jax.nn.softmax(s, axis=-1).astype(v.dtype)
        p_win = p[..., :kr * NA_KCB].reshape(B, H, n_cb, NA_QCB, kr, NA_KCB)
        return (jnp.einsum('bhnqjk,bjnkhd->bnqhd', p_win, vb)
                + jnp.einsum('bhnql,blhd->bnqhd', p[..., kr * NA_KCB:], vc))

    out = lax.map(one_row, (qr, r_idx, r_start))
    return jnp.moveaxis(out, 0, 1).reshape(B, S, H * hd)


def _dense_attn(q, k, v):
    B, L, H, hd = q.shape
    s = jnp.einsum('bqhd,bkhd->bhqk', q, k).astype(jnp.float32) * hd ** -0.5
    p = jax.nn.softmax(s, axis=-1).astype(v.dtype)
    return jnp.einsum('bhqk,bkhd->bqhd', p, v).reshape(B, L, H * hd)


def _short_conv(p, conv_w):
    C = SC_WIDTH
    bg, cg, xin = p[..., :C], p[..., C:2 * C], p[..., 2 * C:]
    u = cg * xin
    y = conv_w[0] * _shift(u, 1) + conv_w[1] * u + conv_w[2] * _shift(u, -1)
    return bg * y


def _merge(y_rw, y_na, y_sc, gate_pre, b_gate, w_out_rw, w_out_na, w_out_sc, w_merge):
    D = w_merge.shape[0]
    gates = jax.nn.sigmoid(gate_pre + b_gate)
    m = (gates[..., :D] * (y_rw @ w_out_rw)
         + gates[..., D:2 * D] * (y_na @ w_out_na)
         + gates[..., 2 * D:] * (y_sc @ w_out_sc))
    return m @ w_merge


def _moe(h, router_w, router_b, w1, b1, w2, b2):
    N, D = h.shape
    logits = (h @ router_w + router_b).astype(jnp.float32)
    top_v, top_i = lax.top_k(logits, TOP_K)
    gate = jax.nn.softmax(top_v, axis=-1)
    A = N * TOP_K
    flat_e = top_i.reshape(A)
    flat_tok = jnp.arange(A, dtype=jnp.int32) // TOP_K
    order = jnp.argsort(flat_e)
    e_sorted = flat_e[order]
    counts = jnp.zeros(N_EXPERTS, jnp.int32).at[flat_e].add(1)
    padded = (counts + MOE_BLOCK - 1) // MOE_BLOCK * MOE_BLOCK
    pad_end = jnp.cumsum(padded)
    pad_start = pad_end - padded
    start = jnp.cumsum(counts) - counts
    dest = pad_start[e_sorted] + jnp.arange(A, dtype=jnp.int32) - start[e_sorted]
    n_blocks = -(-A // MOE_BLOCK) + N_EXPERTS
    n_slots = n_blocks * MOE_BLOCK
    slot_tok = jnp.zeros(n_slots, jnp.int32).at[dest].set(flat_tok[order])
    slot_w = jnp.zeros(n_slots, jnp.float32).at[dest].set(gate.reshape(A)[order])
    block_e = jnp.minimum(jnp.searchsorted(pad_end, jnp.arange(n_blocks, dtype=jnp.int32) * MOE_BLOCK, side='right'),
                          N_EXPERTS - 1)

    def expert_block(args):
        tok, e = args
        z = h[tok] @ w1[e] + b1[e]
        z_glu = jnp.minimum(z[:, ::2], SWIGLU_LIMIT)
        z_lin = jnp.clip(z[:, 1::2], -SWIGLU_LIMIT, SWIGLU_LIMIT)
        act = z_glu * jax.nn.sigmoid(SWIGLU_ALPHA * z_glu) * (z_lin + 1.0)
        return act @ w2[e] + b2[e]

    y = lax.map(expert_block, (slot_tok.reshape(n_blocks, MOE_BLOCK), block_e))
    y = y.reshape(n_slots, D) * slot_w[:, None].astype(y.dtype)
    return jax.ops.segment_sum(y, slot_tok, num_segments=N)


def _layer(x, xc, mods_lat, mods_ctx, lp, need_ctx_out):
    B, S, D = x.shape
    L = xc.shape[1]
    sh1, sc1, g1, sh2, sc2, g2 = mods_lat
    csh1, csc1, cg1, csh2, csc2, cg2 = mods_ctx
    h = _layernorm(x) * (1.0 + sc1) + sh1
    hc = _layernorm(xc) * (1.0 + csc1) + csh1
    proj = jnp.concatenate([hc, h], axis=1) @ lp['w_in']
    pc, pl = proj[:, :L], proj[:, L:]
    o_na = RW_COLS
    o_sc = RW_COLS + NA_IN_COLS
    o_gate = RW_COLS + NA_IN_COLS + SC_IN_COLS
    rw = (lp['rw_mu'], lp['rw_w0'], lp['rw_w_lora'], lp['rw_a0'], lp['rw_a_lora'],
          lp['rw_g_lora'], lp['rw_k_k'], lp['rw_k_a'])
    rw_post = (lp['rw_r_k'], lp['rw_gn_w'], lp['rw_gn_b'], x.dtype)
    fc = _rwkv_prepare(pc[..., :o_na], *rw, rotate=False)
    fl = _rwkv_prepare(pl[..., :o_na], *rw, rotate=True)
    s0 = jnp.zeros((B, RW_HEADS, RW_HEAD_DIM, RW_HEAD_DIM), jnp.float32)
    s_cf, wkv_cf = _wkv_scan(s0, *fc[0], reverse=False)
    s_cb, wkv_cb = _wkv_scan(s0, *fc[1], reverse=True)
    _, wkv_lf = _wkv_scan(s_cf, *fl[0], reverse=False)
    _, wkv_lb = _wkv_scan(s_cb, *fl[1], reverse=True)
    y_rw = _rwkv_out(wkv_lf + wkv_lb, *fl[2:], *rw_post)
    qc, kc, vc = _na_heads(pc[..., o_na:o_sc])
    q, k, v = _na_heads(pl[..., o_na:o_sc])
    y_na = _na_latent(q, k, v, kc, vc, lp['na_rpb'])
    y_sc = _short_conv(pl[..., o_sc:o_gate], lp['sc_conv'])
    merge_w = (lp['b_gate'], lp['w_out_rw'], lp['w_out_na'], lp['w_out_sc'], lp['w_merge'])
    mix = _merge(y_rw, y_na, y_sc, pl[..., o_gate:], *merge_w)
    x = _layernorm(DEEPNORM_ALPHA * x + g1 * mix) * lp['ln1_g'] + lp['ln1_b']
    moe_w = (lp['router_w'], lp['router_b'], lp['moe_w1'], lp['moe_b1'], lp['moe_w2'], lp['moe_b2'])
    h2 = _layernorm(x) * (1.0 + sc2) + sh2
    if need_ctx_out:
        yc_rw = _rwkv_out(wkv_cf + wkv_cb, *fc[2:], *rw_post)
        yc_na = _dense_attn(qc, kc, vc)
        yc_sc = _short_conv(pc[..., o_sc:o_gate], lp['sc_conv'])
        mix_c = _merge(yc_rw, yc_na, yc_sc, pc[..., o_gate:], *merge_w)
        xc = _layernorm(DEEPNORM_ALPHA * xc + cg1 * mix_c) * lp['ln1_g'] + lp['ln1_b']
        h2c = _layernorm(xc) * (1.0 + csc2) + csh2
        y = _moe(jnp.concatenate([h2c, h2], axis=1).reshape(B * (L + S), D), *moe_w).reshape(B, L + S, D)
        xc = _layernorm(DEEPNORM_ALPHA * xc + cg2 * y[:, :L]) * lp['ln2_g'] + lp['ln2_b']
        y_lat = y[:, L:]
    else:
        xc = None
        y_lat = _moe(h2.reshape(B * S, D), *moe_w).reshape(B, S, D)
    x = _layernorm(DEEPNORM_ALPHA * x + g2 * y_lat) * lp['ln2_g'] + lp['ln2_b']
    return x, xc


def setup_inputs(seed: int = 0) -> dict:
    key = jax.random.key(seed)
    ks = iter(jax.random.split(key, 48))

    def nrm(shape, s):
        return jax.random.normal(next(ks), shape, jnp.float32) * s

    def unif(shape, lo, hi):
        return jax.random.uniform(next(ks), shape, jnp.float32, lo, hi)

    Ld, D, C = DEPTH, D_MODEL, RW_WIDTH
    return {
        'x': nrm((BATCH, SEQ, D), 1.0),
        'c': nrm((BATCH, D), 1.0),
        'ctx': nrm((BATCH, CTX_LEN, D), 1.0),
        'c_ctx': nrm((D,), 1.0),
        'w_mod': nrm((Ld, D, 6 * D), 0.3 * D ** -0.5),
        'b_mod': nrm((Ld, 6 * D), 0.01),
        'w_in': nrm((Ld, D, IN_COLS), D ** -0.5),
        'rw_mu': unif((Ld, 2, RW_COLS), 0.0, 0.5),
        'rw_w0': unif((Ld, 2, C), -4.0, 1.0),
        'rw_w_lora': nrm((Ld, 2, DECAY_LORA, C), 0.5 * DECAY_LORA ** -0.5),
        'rw_a0': nrm((Ld, 2, C), 0.1),
        'rw_a_lora': nrm((Ld, 2, ICLR_LORA, C), 0.5 * ICLR_LORA ** -0.5),
        'rw_g_lora': nrm((Ld, GATE_LORA, C), GATE_LORA ** -0.5),
        'rw_k_k': 0.85 + nrm((Ld, C), 0.05),
        'rw_k_a': 1.0 + nrm((Ld, C), 0.05),
        'rw_r_k': nrm((Ld, RW_HEADS, RW_HEAD_DIM), 0.1),
        'rw_gn_w': 1.0 + nrm((Ld, C), 0.05),
        'rw_gn_b': nrm((Ld, C), 0.01),
        'w_out_rw': nrm((Ld, C, D), C ** -0.5),
        'na_rpb': nrm((Ld, NA_HEADS, 2 * NA_WIN_ROWS - 1, 2 * NA_WIN_COLS - 1), 0.1),
        'w_out_na': nrm((Ld, NA_WIDTH, D), NA_WIDTH ** -0.5),
        'sc_conv': nrm((Ld, SC_TAPS, SC_WIDTH), SC_TAPS ** -0.5),
        'w_out_sc': nrm((Ld, SC_WIDTH, D), SC_WIDTH ** -0.5),
        'b_gate': nrm((Ld, GATE_COLS), 0.01),
        'w_merge': nrm((Ld, D, D), DEEPNORM_BETA * D ** -0.5),
        'ln1_g': 1.0 + nrm((Ld, D), 0.05),
        'ln1_b': nrm((Ld, D), 0.01),
        'router_w': nrm((Ld, D, N_EXPERTS), D ** -0.5),
        'router_b': nrm((Ld, N_EXPERTS), 0.01),
        'moe_w1': nrm((Ld, N_EXPERTS, D, 2 * D_EXPERT), D ** -0.5),
        'moe_b1': nrm((Ld, N_EXPERTS, 2 * D_EXPERT), 0.01),
        'moe_w2': nrm((Ld, N_EXPERTS, D_EXPERT, D), DEEPNORM_BETA * D_EXPERT ** -0.5),
        'moe_b2': nrm((Ld, N_EXPERTS, D), 0.01),
        'ln2_g': 1.0 + nrm((Ld, D), 0.05),
        'ln2_b': nrm((Ld, D), 0.01),
    }


def reference(x, c, ctx, c_ctx, w_mod, b_mod, w_in, rw_mu, rw_w0, rw_w_lora, rw_a0, rw_a_lora,
              rw_g_lora, rw_k_k, rw_k_a, rw_r_k, rw_gn_w, rw_gn_b, w_out_rw, na_rpb, w_out_na,
              sc_conv, w_out_sc, b_gate, w_merge, ln1_g, ln1_b, router_w, router_b, moe_w1,
              moe_b1, moe_w2, moe_b2, ln2_g, ln2_b):
    xc = ctx
    for i in range(DEPTH):
        lp = dict(w_in=w_in[i], rw_mu=rw_mu[i], rw_w0=rw_w0[i], rw_w_lora=rw_w_lora[i],
                  rw_a0=rw_a0[i], rw_a_lora=rw_a_lora[i], rw_g_lora=rw_g_lora[i],
                  rw_k_k=rw_k_k[i], rw_k_a=rw_k_a[i], rw_r_k=rw_r_k[i], rw_gn_w=rw_gn_w[i],
                  rw_gn_b=rw_gn_b[i], w_out_rw=w_out_rw[i], na_rpb=na_rpb[i],
                  w_out_na=w_out_na[i], sc_conv=sc_conv[i], w_out_sc=w_out_sc[i],
                  b_gate=b_gate[i], w_merge=w_merge[i], ln1_g=ln1_g[i], ln1_b=ln1_b[i],
                  router_w=router_w[i], router_b=router_b[i], moe_w1=moe_w1[i],
                  moe_b1=moe_b1[i], moe_w2=moe_w2[i], moe_b2=moe_b2[i],
                  ln2_g=ln2_g[i], ln2_b=ln2_b[i])
        mods_lat = [m[:, None, :] for m in _modulation(c, w_mod[i], b_mod[i])]
        mods_ctx = _modulation(c_ctx, w_mod[i], b_mod[i])
        x, xc = _layer(x, xc, mods_lat, mods_ctx, lp, i < DEPTH - 1)
    return x
```

```python
import functools
import math

import numpy as np
import jax
import jax.numpy as jnp
from jax import lax
from jax.experimental import pallas as pl
from jax.experimental.pallas import tpu as pltpu

GRID_W = 64
RW_HEADS = 16
HEAD_DIM = 64
RW_WIDTH = RW_HEADS * HEAD_DIM
DECAY_LORA = 64
ICLR_LORA = 64
GATE_LORA = 160
GATE_LORA_PAD = 256
GN_EPS = 64e-5
NA_HEADS = 16
NA_WIDTH = NA_HEADS * HEAD_DIM
NA_WIN_ROWS = 8
NA_WIN_COLS = 16
SC_WIDTH = 1024
RW_COLS = 3 * RW_WIDTH + 2 * DECAY_LORA + 2 * ICLR_LORA + GATE_LORA
RW_COLS_PAD = 3 * RW_WIDTH + 2 * DECAY_LORA + 2 * ICLR_LORA + GATE_LORA_PAD
N_EXPERTS = 32
TOP_K = 4
D_EXPERT = 512
SWIGLU_ALPHA = 1.702
SWIGLU_LIMIT = 7.0
MOE_BLOCK = 256
ROPE_BASE = 10000.0
LN_EPS = 1e-6
NEG_INF = -1e30

LANE = 128
ROW_BLK = 256
MM_TM = 1024
MM_TN = 512
SCAN_TT = 32
NA_GROUP = 4
NA_KROWS = 12
VMEM_LIMIT = 56 * 1024 * 1024

HI = lax.Precision.HIGHEST
F32 = jnp.float32
BF16 = jnp.bfloat16


def _cparams(sem):
    return pltpu.CompilerParams(dimension_semantics=sem, vmem_limit_bytes=VMEM_LIMIT)


def _ln(x):
    mu = jnp.mean(x, axis=-1, keepdims=True)
    xc = x - mu
    var = jnp.mean(xc * xc, axis=-1, keepdims=True)
    return xc * lax.rsqrt(var + LN_EPS)


def _sigmoid(x):
    return 1.0 / (1.0 + jnp.exp(-x))


def _mod_kernel(c_ref, w_ref, b_ref, o_ref):
    c = c_ref[...]
    s = (c * _sigmoid(c)).astype(BF16)
    o_ref[0] = jnp.dot(s, w_ref[0].astype(BF16), preferred_element_type=F32) + b_ref[0]


def _modulation(cc, w_mod, b_mod):
    depth, d, n = w_mod.shape
    tn = 1024
    return pl.pallas_call(
        _mod_kernel,
        grid=(depth, n // tn),
        in_specs=[pl.BlockSpec((16, d), lambda l, j: (0, 0)),
                  pl.BlockSpec((1, d, tn), lambda l, j: (l, 0, j)),
                  pl.BlockSpec((1, 1, tn), lambda l, j: (l, 0, j))],
        out_specs=pl.BlockSpec((1, 16, tn), lambda l, j: (l, 0, j)),
        out_shape=jax.ShapeDtypeStruct((depth, 16, n), F32),
        compiler_params=_cparams(("arbitrary", "arbitrary")),
        name="modulation",
    )(cc, w_mod, b_mod.reshape(depth, 1, n))


def _mod_spec(chunk, d, nblk_per_batch):
    def imap(i):
        row = jnp.where(i % nblk_per_batch == 0, 8, i // nblk_per_batch)
        return (row, 0, chunk)
    return pl.BlockSpec((1, 1, d), imap)


def _lnmod_kernel(x_ref, sh_ref, sc_ref, o_ref):
    o_ref[...] = (_ln(x_ref[...]) * (1.0 + sc_ref[0]) + sh_ref[0]).astype(o_ref.dtype)


def _lnmod(x, mods3, nbb):
    n, d = x.shape
    return pl.pallas_call(
        _lnmod_kernel,
        grid=(n // ROW_BLK,),
        in_specs=[pl.BlockSpec((ROW_BLK, d), lambda i: (i, 0)),
                  _mod_spec(0, d, nbb), _mod_spec(1, d, nbb)],
        out_specs=pl.BlockSpec((ROW_BLK, d), lambda i: (i, 0)),
        out_shape=jax.ShapeDtypeStruct((n, d), BF16),
        compiler_params=_cparams(("parallel",)),
        name="ln_mod",
    )(x, mods3, mods3)


def _mm_kernel(a_ref, b_ref, o_ref):
    o_ref[...] = jnp.dot(a_ref[...], b_ref[...], preferred_element_type=F32).astype(o_ref.dtype)


def _matmul(a, b, out_dtype=F32):
    m, k = a.shape
    _, n = b.shape
    tm = MM_TM if m % MM_TM == 0 else ROW_BLK
    return pl.pallas_call(
        _mm_kernel,
        grid=(m // tm, n // MM_TN),
        in_specs=[pl.BlockSpec((tm, k), lambda i, j: (i, 0)),
                  pl.BlockSpec((k, MM_TN), lambda i, j: (0, j))],
        out_specs=pl.BlockSpec((tm, MM_TN), lambda i, j: (i, j)),
        out_shape=jax.ShapeDtypeStruct((m, n), out_dtype),
        compiler_params=_cparams(("parallel", "arbitrary")),
        name="in_proj",
    )(a, b)


def _halo_specs(width, col_blk, nblk):
    per = ROW_BLK // 8

    def prev_map(i):
        return (jnp.maximum(i * per - 1, 0), col_blk)

    def next_map(i):
        return (jnp.minimum((i + 1) * per, nblk * per - 1), col_blk)

    return (pl.BlockSpec((8, width), prev_map), pl.BlockSpec((8, width), next_map))


def _shifted(p, prev8, next8, i, nbb):
    pos = i % nbb
    has_prev = pos >= 2
    has_next = jnp.logical_and(pos >= 1, pos <= nbb - 2)
    rows = lax.broadcasted_iota(jnp.int32, p.shape, 0)
    prow = jnp.where(has_prev, prev8[7:8, :], 0.0)
    nrow = jnp.where(has_next, next8[0:1, :], 0.0)
    p_prev = jnp.where(rows == 0, prow, pltpu.roll(p, 1, 0))
    p_next = jnp.where(rows == p.shape[0] - 1, nrow, pltpu.roll(p, p.shape[0] - 1, 0))
    return p_prev, p_next


def _seg_sum(x, e_ref, et_ref):
    s = jnp.dot(x, e_ref[...], precision=HI, preferred_element_type=F32)
    return jnp.dot(s, et_ref[...], precision=HI, preferred_element_type=F32)


def _swap_halves(x):
    lanes = lax.broadcasted_iota(jnp.int32, x.shape, 1)
    first = (lanes % HEAD_DIM) < (HEAD_DIM // 2)
    n = x.shape[1]
    return jnp.where(first, pltpu.roll(x, n - HEAD_DIM // 2, 1), pltpu.roll(x, HEAD_DIM // 2, 1))


def _prep_kernel(nbb, p_ref, pp_ref, pn_ref, mu_ref, w0_ref, wl_ref, a0_ref, al_ref, gl_ref,
                 kk_ref, ka_ref, rk_ref, cos_ref, sin_ref, e_ref, et_ref,
                 r_o, kap_o, v_o, w_o, kd_o, b_o, bonus_o, g_o):
    i = pl.program_id(0)
    c = RW_WIDTH
    p = p_ref[...]
    p_prev, p_next = _shifted(p, pp_ref[...], pn_ref[...], i, nbb)
    pm = p + mu_ref[0:1, :] * (p_prev - p) + mu_ref[1:2, :] * (p_next - p)
    r, k, v = pm[:, :c], pm[:, c:2 * c], pm[:, 2 * c:3 * c]
    o = 3 * c
    wl = jnp.tanh(pm[:, o:o + 2 * DECAY_LORA])
    o += 2 * DECAY_LORA
    al = pm[:, o:o + 2 * ICLR_LORA]
    o += 2 * ICLR_LORA
    gl = _sigmoid(pm[:, o:o + GATE_LORA_PAD])
    g_o[...] = jnp.dot(gl, gl_ref[...], precision=HI, preferred_element_type=F32)
    kk = k * kk_ref[...]
    ss = _seg_sum(kk * kk, e_ref, et_ref)
    kk = kk / jnp.maximum(jnp.sqrt(ss), 1e-12)
    cos = jnp.concatenate([cos_ref[...]] * (c // LANE), axis=1)
    sin = jnp.concatenate([sin_ref[...]] * (c // LANE), axis=1)

    def rope(x):
        return x * cos + _swap_halves(x) * sin

    kap = rope(kk)
    r_o[...] = rope(r)
    kap_o[...] = kap
    v_o[...] = v
    k_sum = jnp.zeros_like(k)
    for d in range(2):
        w_raw = w0_ref[d:d + 1, :] + jnp.dot(wl, wl_ref[d], precision=HI, preferred_element_type=F32)
        w_o[d] = jnp.exp(-math.exp(-0.5) * _sigmoid(w_raw))
        a = _sigmoid(a0_ref[d:d + 1, :] + jnp.dot(al, al_ref[d], precision=HI, preferred_element_type=F32))
        kd = k * (1.0 + (a - 1.0) * ka_ref[...])
        k_sum = k_sum + kd
        kd_o[d] = rope(kd)
        b_o[d] = kap * a
    bonus_o[...] = _seg_sum(r * k_sum * rk_ref[...], e_ref, et_ref) * v


def _rwkv_prepare(proj, lw, consts, nbb):
    n = proj.shape[0]
    c = RW_WIDTH
    nblk = n // ROW_BLK
    prev_spec, next_spec = _halo_specs(RW_COLS_PAD, 0, nblk)
    full = lambda shape: pl.BlockSpec(shape, lambda i: (0,) * len(shape))
    row_c = pl.BlockSpec((ROW_BLK, c), lambda i: (i, 0))
    row_2c = pl.BlockSpec((2, ROW_BLK, c), lambda i: (0, i, 0))
    tab = pl.BlockSpec((ROW_BLK, LANE), lambda i: (i % nbb, 0))
    outs = pl.pallas_call(
        functools.partial(_prep_kernel, nbb),
        grid=(nblk,),
        in_specs=[pl.BlockSpec((ROW_BLK, RW_COLS_PAD), lambda i: (i, 0)), prev_spec, next_spec,
                  full((2, RW_COLS_PAD)), full((2, c)), full((2, 2 * DECAY_LORA, c)), full((2, c)),
                  full((2, 2 * ICLR_LORA, c)), full((GATE_LORA_PAD, c)), full((1, c)), full((1, c)),
                  full((1, c)), tab, tab, full((c, LANE)), full((LANE, c))],
        out_specs=[row_c, row_c, row_c, row_2c, row_2c, row_2c, row_c, row_c],
        out_shape=[jax.ShapeDtypeStruct((n, c), F32)] * 3 + [jax.ShapeDtypeStruct((2, n, c), F32)] * 3
        + [jax.ShapeDtypeStruct((n, c), F32)] * 2,
        compiler_params=_cparams(("parallel",)),
        name="rwkv_prepare",
    )(proj, proj, proj, lw["mu"], lw["w0"], lw["w_lora"], lw["a0"], lw["a_lora"], lw["g_lora"],
      lw["k_k"], lw["k_a"], lw["r_k"], consts["cos"], consts["sin"], consts["seg"], consts["seg_t"])
    return outs


def _scan_kernel(r_ref, kap_ref, v_ref, w_ref, kd_ref, b_ref, o_ref, s_ref):
    d = pl.program_id(0)
    tt = r_ref.shape[0]

    @pl.when(pl.program_id(1) == 0)
    def _():
        s_ref[...] = jnp.zeros_like(s_ref)

    def step(i, carry):
        t = jnp.where(d == 0, i, tt - 1 - i)
        skk = s_ref[0] * kap_ref[t, 0:1, :]
        for k in range(1, HEAD_DIM):
            skk = skk + s_ref[k] * kap_ref[t, k:k + 1, :]
        v = v_ref[t]
        out = None
        for k in range(HEAD_DIM):
            s_new = (s_ref[k] * w_ref[0, t, k:k + 1, :] - skk * b_ref[0, t, k:k + 1, :]
                     + v * kd_ref[0, t, k:k + 1, :])
            s_ref[k] = s_new
            term = s_new * r_ref[t, k:k + 1, :]
            out = term if out is None else out + term
        o_ref[0, t] = out
        return carry

    lax.fori_loop(0, tt, step, 0)


def _scan_block(n_ctx_blk, nb):
    def blk(d, j):
        rev = jnp.where(j < n_ctx_blk, n_ctx_blk - 1 - j, nb - 1 - (j - n_ctx_blk))
        return jnp.where(d == 0, j, rev)
    return blk


def _wkv_scan(r, kap, v, w, kd, b, ctx_len):
    t, _, lanes = r.shape
    nb = t // SCAN_TT
    blk = _scan_block(ctx_len // SCAN_TT, nb)
    shared = pl.BlockSpec((SCAN_TT, HEAD_DIM, lanes), lambda d, j: (blk(d, j), 0, 0))
    per_dir = pl.BlockSpec((1, SCAN_TT, HEAD_DIM, lanes), lambda d, j: (d, blk(d, j), 0, 0))
    return pl.pallas_call(
        _scan_kernel,
        grid=(2, nb),
        in_specs=[shared, shared, shared, per_dir, per_dir, per_dir],
        out_specs=per_dir,
        out_shape=jax.ShapeDtypeStruct((2, t, HEAD_DIM, lanes), F32),
        scratch_shapes=[pltpu.VMEM((HEAD_DIM, HEAD_DIM, lanes), F32)],
        compiler_params=_cparams(("arbitrary", "arbitrary")),
        name="wkv_scan",
    )(r, kap, v, w, kd, b)


def _rwkv_out_kernel(wkv_ref, bonus_ref, g_ref, gw_ref, gb_ref, e_ref, et_ref, o_ref):
    x = wkv_ref[0] + wkv_ref[1]
    inv_n = 1.0 / HEAD_DIM
    mu = _seg_sum(x, e_ref, et_ref) * inv_n
    xc = x - mu
    var = _seg_sum(xc * xc, e_ref, et_ref) * inv_n
    y = xc * lax.rsqrt(var + GN_EPS) * gw_ref[...] + gb_ref[...]
    o_ref[...] = ((y + bonus_ref[...]) * g_ref[...]).astype(o_ref.dtype)


def _rwkv_out(wkv, bonus, g, lw, consts):
    n, c = bonus.shape
    full = lambda shape: pl.BlockSpec(shape, lambda i: (0,) * len(shape))
    row_c = pl.BlockSpec((ROW_BLK, c), lambda i: (i, 0))
    return pl.pallas_call(
        _rwkv_out_kernel,
        grid=(n // ROW_BLK,),
        in_specs=[pl.BlockSpec((2, ROW_BLK, c), lambda i: (0, i, 0)), row_c, row_c,
                  full((1, c)), full((1, c)), full((c, LANE)), full((LANE, c))],
        out_specs=row_c,
        out_shape=jax.ShapeDtypeStruct((n, c), BF16),
        compiler_params=_cparams(("parallel",)),
        name="rwkv_out",
    )(wkv, bonus, g, lw["gn_w"], lw["gn_b"], consts["seg"], consts["seg_t"])


def _conv_kernel(nbb, bg_ref, cg_ref, x_ref, cgp_ref, xp_ref, cgn_ref, xn_ref, w_ref, o_ref):
    i = pl.program_id(0)
    u = cg_ref[...] * x_ref[...]
    u_prev, u_next = _shifted(u, cgp_ref[...] * xp_ref[...], cgn_ref[...] * xn_ref[...], i, nbb)
    y = w_ref[0:1, :] * u_prev + w_ref[1:2, :] * u + w_ref[2:3, :] * u_next
    o_ref[...] = (bg_ref[...] * y).astype(o_ref.dtype)


def _short_conv(proj, conv_w, col0, nbb):
    n = proj.shape[0]
    cw = MM_TN
    nc = SC_WIDTH // cw
    assert col0 % cw == 0
    cb = col0 // cw
    nblk = n // ROW_BLK
    blk = lambda s: pl.BlockSpec((ROW_BLK, cw), lambda i, j: (i, cb + s * nc + j))

    def halos(s):
        prev_spec, next_spec = _halo_specs(cw, 0, nblk)
        pm, nm = prev_spec.index_map, next_spec.index_map
        return (pl.BlockSpec((8, cw), lambda i, j: (pm(i)[0], cb + s * nc + j)),
                pl.BlockSpec((8, cw), lambda i, j: (nm(i)[0], cb + s * nc + j)))

    cg_prev, cg_next = halos(1)
    x_prev, x_next = halos(2)
    return pl.pallas_call(
        functools.partial(_conv_kernel, nbb),
        grid=(nblk, nc),
        in_specs=[blk(0), blk(1), blk(2), cg_prev, x_prev, cg_next, x_next,
                  pl.BlockSpec((8, cw), lambda i, j: (0, j))],
        out_specs=pl.BlockSpec((ROW_BLK, cw), lambda i, j: (i, j)),
        out_shape=jax.ShapeDtypeStruct((n, SC_WIDTH), BF16),
        compiler_params=_cparams(("parallel", "arbitrary")),
        name="short_conv",
    )(proj, proj, proj, proj, proj, proj, proj, conv_w)


def _na_tables(rows):
    kr = min(NA_WIN_ROWS, rows)
    n_groups = rows // NA_GROUP
    krows = min(NA_KROWS, rows)
    idx = np.zeros((n_groups, NA_GROUP, GRID_W, krows, GRID_W), np.int32)
    valid = np.zeros((n_groups, NA_GROUP, GRID_W, krows, GRID_W), bool)
    bases = np.zeros((n_groups,), np.int32)
    qc = np.arange(GRID_W)[:, None]
    kc = np.arange(GRID_W)[None, :]
    wstart = np.clip(qc - NA_WIN_COLS // 2, 0, GRID_W - NA_WIN_COLS)
    col_ok = (kc >= wstart) & (kc < wstart + NA_WIN_COLS)
    dc = np.clip(kc - qc + NA_WIN_COLS - 1, 0, 2 * NA_WIN_COLS - 2)
    for g in range(n_groups):
        r0 = g * NA_GROUP
        base = int(np.clip(np.clip(r0 - kr // 2, 0, rows - kr), 0, rows - krows))
        bases[g] = base
        for rl in range(NA_GROUP):
            r = r0 + rl
            rs = int(np.clip(r - kr // 2, 0, rows - kr))
            for j in range(krows):
                krow = base + j
                if rs <= krow < rs + kr:
                    dr = krow - r + NA_WIN_ROWS - 1
                    idx[g, rl, :, j, :] = dr * (2 * NA_WIN_COLS - 1) + dc
                    valid[g, rl, :, j, :] = col_ok
            assert valid[g, rl, 0].any(axis=-1).sum() == kr
    nq = NA_GROUP * GRID_W
    nk = krows * GRID_W
    idx = idx.reshape(n_groups, nq, nk)
    valid = valid.reshape(n_groups, nq, nk)
    uniq, table_of = [], []
    for g in range(n_groups):
        for u, gu in enumerate(uniq):
            if np.array_equal(idx[g], idx[gu]) and np.array_equal(valid[g], valid[gu]):
                table_of.append(u)
                break
        else:
            table_of.append(len(uniq))
            uniq.append(g)
    return idx[uniq], valid[uniq], tuple(int(b) for b in bases), tuple(table_of)


def _na_kernel(ctx_len, bases, table_of, q_ref, k_ref, v_ref, bias_ref, o_ref):
    scale = HEAD_DIM ** -0.5
    nq = NA_GROUP * GRID_W
    nk = bias_ref.shape[3]
    outs_heads = []
    for h in range(2):
        sl = slice(h * HEAD_DIM, (h + 1) * HEAD_DIM)
        q = (q_ref[:, sl] * scale).astype(BF16)
        k = k_ref[:, sl].astype(BF16)
        v = v_ref[:, sl].astype(BF16)
        kc, vc = k[:ctx_len], v[:ctx_len]
        dn = (((1,), (1,)), ((), ()))
        s = lax.dot_general(q[:ctx_len], kc, dn, preferred_element_type=F32)
        s = s - jnp.max(s, axis=-1, keepdims=True)
        e = jnp.exp(s)
        p = (e / jnp.sum(e, axis=-1, keepdims=True)).astype(BF16)
        pieces = [jnp.dot(p, vc, preferred_element_type=F32)]
        for g, base in enumerate(bases):
            q0 = ctx_len + g * nq
            k0 = ctx_len + base * GRID_W
            qg = q[q0:q0 + nq]
            s_win = (lax.dot_general(qg, k[k0:k0 + nk], dn, preferred_element_type=F32)
                     + bias_ref[table_of[g], h])
            s_ctx = lax.dot_general(qg, kc, dn, preferred_element_type=F32)
            m = jnp.maximum(jnp.max(s_win, axis=-1, keepdims=True), jnp.max(s_ctx, axis=-1, keepdims=True))
            e_win = jnp.exp(s_win - m)
            e_ctx = jnp.exp(s_ctx - m)
            inv = 1.0 / (jnp.sum(e_win, axis=-1, keepdims=True) + jnp.sum(e_ctx, axis=-1, keepdims=True))
            acc = jnp.dot((e_win * inv).astype(BF16), v[k0:k0 + nk], preferred_element_type=F32)
            acc = acc + jnp.dot((e_ctx * inv).astype(BF16), vc, preferred_element_type=F32)
            pieces.append(acc)
        outs_heads.append(jnp.concatenate(pieces, axis=0))
    o_ref[...] = jnp.concatenate(outs_heads, axis=1).astype(o_ref.dtype)


def _na_bias(rpb, tables):
    idx, valid = tables[0], tables[1]
    flat = rpb.reshape(NA_HEADS, -1)
    bias = jnp.where(valid[None], flat[:, idx], NEG_INF)
    return jnp.transpose(bias, (1, 0, 2, 3))


def _na_attention(proj, bias, tables, col0, batch, t_len, ctx_len):
    n = proj.shape[0]
    cb = col0 // LANE
    hp = NA_HEADS // 2
    n_tab, _, nq, nk = bias.shape
    seq = lambda j: pl.BlockSpec((t_len, LANE), lambda h, b: (b, cb + j * hp + h))
    return pl.pallas_call(
        functools.partial(_na_kernel, ctx_len, tables[2], tables[3]),
        grid=(hp, batch),
        in_specs=[seq(0), seq(1), seq(2),
                  pl.BlockSpec((n_tab, 2, nq, nk), lambda h, b: (0, h, 0, 0))],
        out_specs=pl.BlockSpec((t_len, LANE), lambda h, b: (b, h)),
        out_shape=jax.ShapeDtypeStruct((n, NA_WIDTH), BF16),
        compiler_params=_cparams(("arbitrary", "arbitrary")),
        name="na_attention",
    )(proj, proj, proj, bias)


def _merge1_kernel(yr_ref, yn_ref, ys_ref, wr_ref, wn_ref, ws_ref, g0_ref, g1_ref, g2_ref,
                   b0_ref, b1_ref, b2_ref, o_ref):
    m = _sigmoid(g0_ref[...] + b0_ref[...]) * jnp.dot(yr_ref[...], wr_ref[...], preferred_element_type=F32)
    m = m + _sigmoid(g1_ref[...] + b1_ref[...]) * jnp.dot(yn_ref[...], wn_ref[...], preferred_element_type=F32)
    m = m + _sigmoid(g2_ref[...] + b2_ref[...]) * jnp.dot(ys_ref[...], ws_ref[...], preferred_element_type=F32)
    o_ref[...] = m.astype(o_ref.dtype)


def _merge1(y_rw, y_na, y_sc, lw, proj, gate_col0, d):
    n, c = y_rw.shape
    tm, tn = 512, MM_TN
    gb = gate_col0 // tn
    nd = d // tn
    ysp = pl.BlockSpec((tm, c), lambda i, j: (i, 0))
    wsp = pl.BlockSpec((c, tn), lambda i, j: (0, j))
    gsp = lambda br: pl.BlockSpec((tm, tn), lambda i, j: (i, gb + br * nd + j))
    bsp = lambda br: pl.BlockSpec((1, tn), lambda i, j: (0, br * nd + j))
    return pl.pallas_call(
        _merge1_kernel,
        grid=(n // tm, nd),
        in_specs=[ysp, ysp, ysp, wsp, wsp, wsp, gsp(0), gsp(1), gsp(2), bsp(0), bsp(1), bsp(2)],
        out_specs=pl.BlockSpec((tm, tn), lambda i, j: (i, j)),
        out_shape=jax.ShapeDtypeStruct((n, d), BF16),
        compiler_params=_cparams(("parallel", "arbitrary")),
        name="merge_branches",
    )(y_rw, y_na, y_sc, lw["w_out_rw"], lw["w_out_na"], lw["w_out_sc"], proj, proj, proj,
      lw["b_gate"], lw["b_gate"], lw["b_gate"])


def _merge2_kernel(alpha, m_ref, w_ref, x_ref, g1_ref, lg_ref, lb_ref, sh_ref, sc_ref, rw_ref, rb_ref,
                   x1_ref, h2_ref, lo_ref):
    mix = jnp.dot(m_ref[...], w_ref[...], preferred_element_type=F32)
    x1 = _ln(alpha * x_ref[...] + g1_ref[0] * mix) * lg_ref[...] + lb_ref[...]
    x1_ref[...] = x1
    h2 = _ln(x1) * (1.0 + sc_ref[0]) + sh_ref[0]
    h2_ref[...] = h2.astype(h2_ref.dtype)
    lo_ref[...] = jnp.dot(h2, rw_ref[...], precision=HI, preferred_element_type=F32) + rb_ref[...]


def _merge2(m, x, mods3, lw, nbb, alpha):
    n, d = x.shape
    full = lambda shape: pl.BlockSpec(shape, lambda i: (0,) * len(shape))
    row = pl.BlockSpec((ROW_BLK, d), lambda i: (i, 0))
    return pl.pallas_call(
        functools.partial(_merge2_kernel, alpha),
        grid=(n // ROW_BLK,),
        in_specs=[row, full((d, d)), row, _mod_spec(2, d, nbb), full((1, d)), full((1, d)),
                  _mod_spec(3, d, nbb), _mod_spec(4, d, nbb), full((d, LANE)), full((1, LANE))],
        out_specs=[row, row, pl.BlockSpec((ROW_BLK, LANE), lambda i: (i, 0))],
        out_shape=[jax.ShapeDtypeStruct((n, d), F32), jax.ShapeDtypeStruct((n, d), BF16),
                   jax.ShapeDtypeStruct((n, LANE), F32)],
        compiler_params=_cparams(("parallel",)),
        name="merge_out_ln",
    )(m, lw["w_merge"], x, mods3, lw["ln1_g"], lw["ln1_b"], mods3, mods3, lw["router_w"], lw["router_b"])


def _route_kernel(lo_ref, gate_ref, idx_ref):
    x = lo_ref[...]
    lanes = lax.broadcasted_iota(jnp.int32, x.shape, 1).astype(F32)
    x = jnp.where(lanes < N_EXPERTS, x, -jnp.inf)
    vals, idxs = [], []
    for _ in range(TOP_K):
        m = jnp.max(x, axis=-1, keepdims=True)
        sel = jnp.min(jnp.where(x == m, lanes, float(LANE)), axis=-1, keepdims=True)
        vals.append(m)
        idxs.append(sel)
        x = jnp.where(lanes == sel, -jnp.inf, x)
    es = [jnp.exp(v - vals[0]) for v in vals]
    tot = es[0] + es[1] + es[2] + es[3]
    gate = jnp.zeros(lo_ref.shape, F32)
    idx = jnp.zeros(lo_ref.shape, F32)
    for j in range(TOP_K):
        gate = jnp.where(lanes == j, es[j] / tot, gate)
        idx = jnp.where(lanes == j, idxs[j], idx)
    gate_ref[...] = gate
    idx_ref[...] = idx.astype(jnp.int32)


def _route(logits):
    n = logits.shape[0]
    row = pl.BlockSpec((ROW_BLK, LANE), lambda i: (i, 0))
    return pl.pallas_call(
        _route_kernel,
        grid=(n // ROW_BLK,),
        in_specs=[row],
        out_specs=[row, row],
        out_shape=[jax.ShapeDtypeStruct((n, LANE), F32), jax.ShapeDtypeStruct((n, LANE), jnp.int32)],
        compiler_params=_cparams(("parallel",)),
        name="moe_route",
    )(logits)


def _expert_kernel(be_ref, x_ref, w1_ref, b1_ref, w2_ref, b2_ref, sw_ref, o_ref):
    f = D_EXPERT
    z = jnp.dot(x_ref[...], w1_ref[0], preferred_element_type=F32) + b1_ref[0]
    z_glu = jnp.minimum(z[:, :f], SWIGLU_LIMIT)
    z_lin = jnp.clip(z[:, f:], -SWIGLU_LIMIT, SWIGLU_LIMIT)
    act = z_glu * _sigmoid(SWIGLU_ALPHA * z_glu) * (z_lin + 1.0)
    y = jnp.dot(act.astype(BF16), w2_ref[0], preferred_element_type=F32) + b2_ref[0]
    o_ref[...] = y * sw_ref[...]


def _experts(xs, block_e, slot_w, lw):
    n_slots, d = xs.shape
    n_blocks = n_slots // MOE_BLOCK
    f2 = 2 * D_EXPERT
    grid_spec = pltpu.PrefetchScalarGridSpec(
        num_scalar_prefetch=1,
        grid=(n_blocks,),
        in_specs=[pl.BlockSpec((MOE_BLOCK, d), lambda i, be: (i, 0)),
                  pl.BlockSpec((1, d, f2), lambda i, be: (be[i], 0, 0)),
                  pl.BlockSpec((1, 1, f2), lambda i, be: (be[i], 0, 0)),
                  pl.BlockSpec((1, D_EXPERT, d), lambda i, be: (be[i], 0, 0)),
                  pl.BlockSpec((1, 1, d), lambda i, be: (be[i], 0, 0)),
                  pl.BlockSpec((MOE_BLOCK, 1), lambda i, be: (i, 0))],
        out_specs=pl.BlockSpec((MOE_BLOCK, d), lambda i, be: (i, 0)),
    )
    return pl.pallas_call(
        _expert_kernel,
        grid_spec=grid_spec,
        out_shape=jax.ShapeDtypeStruct((n_slots, d), F32),
        compiler_params=_cparams(("arbitrary",)),
        name="moe_experts",
    )(block_e, xs, lw["moe_w1"], lw["moe_b1"], lw["moe_w2"], lw["moe_b2"], slot_w)


def _moe(h2, gate, idx, lw):
    n, d = h2.shape
    a = n * TOP_K
    flat_e = idx[:, :TOP_K].reshape(a)
    flat_g = gate[:, :TOP_K].reshape(a)
    order = jnp.argsort(flat_e)
    e_sorted = flat_e[order]
    counts = jnp.zeros(N_EXPERTS, jnp.int32).at[flat_e].add(1)
    padded = (counts + MOE_BLOCK - 1) // MOE_BLOCK * MOE_BLOCK
    pad_end = jnp.cumsum(padded)
    pad_start = pad_end - padded
    start = jnp.cumsum(counts) - counts
    dest_sorted = pad_start[e_sorted] + jnp.arange(a, dtype=jnp.int32) - start[e_sorted]
    n_blocks = -(-a // MOE_BLOCK) + N_EXPERTS
    n_slots = n_blocks * MOE_BLOCK
    slot_tok = jnp.zeros(n_slots, jnp.int32).at[dest_sorted].set((order // TOP_K).astype(jnp.int32))
    slot_w = jnp.zeros(n_slots, F32).at[dest_sorted].set(flat_g[order])
    dest = jnp.zeros(a, jnp.int32).at[order].set(dest_sorted)
    block_e = jnp.minimum(
        jnp.searchsorted(pad_end, jnp.arange(n_blocks, dtype=jnp.int32) * MOE_BLOCK, side="right"),
        N_EXPERTS - 1).astype(jnp.int32)
    xs = h2[slot_tok]
    ys = _experts(xs, block_e, slot_w.reshape(n_slots, 1), lw)
    return jnp.sum(ys[dest.reshape(n, TOP_K)], axis=1)


def _final_kernel(alpha, x_ref, y_ref, g2_ref, lg_ref, lb_ref, sh_ref, sc_ref, x2_ref, h_ref):
    x2 = _ln(alpha * x_ref[...] + g2_ref[0] * y_ref[...]) * lg_ref[...] + lb_ref[...]
    x2_ref[...] = x2
    h_ref[...] = (_ln(x2) * (1.0 + sc_ref[0]) + sh_ref[0]).astype(h_ref.dtype)


def _final(x1, y, mods3, mods3_next, lw, nbb, alpha):
    n, d = x1.shape
    full = lambda shape: pl.BlockSpec(shape, lambda i: (0,) * len(shape))
    row = pl.BlockSpec((ROW_BLK, d), lambda i: (i, 0))
    return pl.pallas_call(
        functools.partial(_final_kernel, alpha),
        grid=(n // ROW_BLK,),
        in_specs=[row, row, _mod_spec(5, d, nbb), full((1, d)), full((1, d)),
                  _mod_spec(0, d, nbb), _mod_spec(1, d, nbb)],
        out_specs=[row, row],
        out_shape=[jax.ShapeDtypeStruct((n, d), F32), jax.ShapeDtypeStruct((n, d), BF16)],
        compiler_params=_cparams(("parallel",)),
        name="residual_ln",
    )(x1, y, mods3, lw["ln2_g"], lw["ln2_b"], mods3_next, mods3_next)


def _rope_tables(ctx_len, seq):
    n_freq = HEAD_DIM // 4
    t = np.arange(seq)
    inv = np.power(ROPE_BASE, -np.arange(n_freq, dtype=np.float32) / n_freq).astype(np.float32)
    row = (t // GRID_W).astype(np.float32)
    col = (t % GRID_W).astype(np.float32)
    ang = jnp.asarray(np.concatenate([row[:, None] * inv, col[:, None] * inv], -1))
    cos = jnp.cos(ang)
    sin = jnp.sin(ang)
    cos_h = jnp.concatenate([cos, cos], -1)
    sin_h = jnp.concatenate([-sin, sin], -1)
    cos_t = jnp.concatenate([jnp.ones((ctx_len, HEAD_DIM), F32), cos_h], 0)
    sin_t = jnp.concatenate([jnp.zeros((ctx_len, HEAD_DIM), F32), sin_h], 0)
    return jnp.tile(cos_t, (1, LANE // HEAD_DIM)), jnp.tile(sin_t, (1, LANE // HEAD_DIM))


def _to_scan_layout(x, batch, t_len):
    lead = x.shape[:-2]
    x = x.reshape(lead + (batch, t_len, RW_HEADS, HEAD_DIM))
    nl = len(lead)
    perm = tuple(range(nl)) + (nl + 1, nl + 3, nl + 0, nl + 2)
    return jnp.transpose(x, perm).reshape(lead + (t_len, HEAD_DIM, batch * RW_HEADS))


def _pad_dir_lora(w):
    z = jnp.zeros_like(w[0])
    return jnp.stack([jnp.concatenate([w[0], z], 0), jnp.concatenate([z, w[1]], 0)])


def _from_scan_layout(o, batch, t_len):
    o = o.reshape(2, t_len, HEAD_DIM, batch, RW_HEADS)
    return jnp.transpose(o, (0, 3, 1, 4, 2)).reshape(2, batch * t_len, RW_WIDTH)


def kernel(x, c, ctx, c_ctx, w_mod, b_mod, w_in, rw_mu, rw_w0, rw_w_lora, rw_a0, rw_a_lora, rw_g_lora, rw_k_k, rw_k_a, rw_r_k, rw_gn_w, rw_gn_b, w_out_rw, na_rpb, w_out_na, sc_conv, w_out_sc, b_gate, w_merge, ln1_g, ln1_b, router_w, router_b, moe_w1, moe_b1, moe_w2, moe_b2, ln2_g, ln2_b):
    batch, seq, d = x.shape
    ctx_len = ctx.shape[1]
    depth = w_in.shape[0]
    t_len = ctx_len + seq
    n = batch * t_len
    nbb = t_len // ROW_BLK
    assert batch <= 8 and ctx_len == ROW_BLK and seq % (NA_GROUP * GRID_W) == 0
    alpha = (2 * depth) ** 0.25

    cc = jnp.zeros((16, d), F32).at[:batch].set(c).at[8].set(c_ctx)
    mods = _modulation(cc, w_mod, b_mod)
    mods3 = [mods[l].reshape(16, 1, 6 * d) for l in range(depth)]

    cos_t, sin_t = _rope_tables(ctx_len, seq)
    seg = (np.arange(RW_WIDTH)[:, None] // HEAD_DIM == np.arange(LANE)[None, :]).astype(np.float32)
    consts = dict(cos=cos_t, sin=sin_t, seg=jnp.asarray(seg), seg_t=jnp.asarray(seg.T))
    tables = _na_tables(seq // GRID_W)

    o_na = RW_COLS_PAD
    o_sc = o_na + 3 * NA_WIDTH
    o_gate = o_sc + 3 * SC_WIDTH
    pad_cols = RW_COLS_PAD - RW_COLS

    xs = jnp.concatenate([ctx, x], axis=1).reshape(n, d)
    h = _lnmod(xs, mods3[0], nbb)
    for l in range(depth):
        w_in_p = jnp.concatenate([w_in[l, :, :RW_COLS], jnp.zeros((d, pad_cols), F32), w_in[l, :, RW_COLS:]],
                                 axis=1).astype(BF16)
        w1 = moe_w1[l]
        lw = dict(
            mu=jnp.pad(rw_mu[l], ((0, 0), (0, pad_cols))), w0=rw_w0[l], w_lora=_pad_dir_lora(rw_w_lora[l]),
            a0=rw_a0[l], a_lora=_pad_dir_lora(rw_a_lora[l]), g_lora=jnp.pad(rw_g_lora[l], ((0, GATE_LORA_PAD - GATE_LORA), (0, 0))),
            k_k=rw_k_k[l].reshape(1, -1), k_a=rw_k_a[l].reshape(1, -1), r_k=rw_r_k[l].reshape(1, -1),
            gn_w=rw_gn_w[l].reshape(1, -1), gn_b=rw_gn_b[l].reshape(1, -1),
            w_out_rw=w_out_rw[l].astype(BF16), w_out_na=w_out_na[l].astype(BF16),
            w_out_sc=w_out_sc[l].astype(BF16), b_gate=b_gate[l].reshape(1, -1),
            w_merge=w_merge[l].astype(BF16), ln1_g=ln1_g[l].reshape(1, -1), ln1_b=ln1_b[l].reshape(1, -1),
            router_w=jnp.pad(router_w[l], ((0, 0), (0, LANE - N_EXPERTS))),
            router_b=jnp.pad(router_b[l], (0, LANE - N_EXPERTS)).reshape(1, -1),
            moe_w1=jnp.concatenate([w1[..., 0::2], w1[..., 1::2]], axis=-1).astype(BF16),
            moe_b1=jnp.concatenate([moe_b1[l][:, 0::2], moe_b1[l][:, 1::2]], axis=-1).reshape(N_EXPERTS, 1, -1),
            moe_w2=moe_w2[l].astype(BF16), moe_b2=moe_b2[l].reshape(N_EXPERTS, 1, -1),
            ln2_g=ln2_g[l].reshape(1, -1), ln2_b=ln2_b[l].reshape(1, -1),
        )
        proj = _matmul(h, w_in_p)
        r_s, kap, v, w, kd, b, bonus, g = _rwkv_prepare(proj, lw, consts, nbb)
        tl = functools.partial(_to_scan_layout, batch=batch, t_len=t_len)
        wkv = _wkv_scan(tl(r_s), tl(kap), tl(v), tl(w), tl(kd), tl(b), ctx_len)
        y_rw = _rwkv_out(_from_scan_layout(wkv, batch, t_len), bonus, g, lw, consts)
        y_na = _na_attention(proj, _na_bias(na_rpb[l], tables), tables, o_na, batch, t_len, ctx_len)
        y_sc = _short_conv(proj, jnp.pad(sc_conv[l], ((0, 5), (0, 0))), o_sc, nbb)
        m = _merge1(y_rw, y_na, y_sc, lw, proj, o_gate, d)
        x1, h2, logits = _merge2(m, xs, mods3[l], lw, nbb, alpha)
        gate, idx = _route(logits)
        y = _moe(h2, gate, idx, lw)
        xs, h = _final(x1, y, mods3[l], mods3[min(l + 1, depth - 1)], lw, nbb, alpha)
    return xs.reshape(batch, t_len, d)[:, ctx_len:]
```

```python
import functools
import math

import numpy as np
import jax
import jax.numpy as jnp
from jax import lax
from jax.experimental import pallas as pl
from jax.experimental.pallas import tpu as pltpu

GRID_W = 64
RW_HEADS = 16
HEAD_DIM = 64
RW_WIDTH = RW_HEADS * HEAD_DIM
DECAY_LORA = 64
ICLR_LORA = 64
GATE_LORA = 160
GATE_LORA_PAD = 256
GN_EPS = 64e-5
NA_HEADS = 16
NA_WIDTH = NA_HEADS * HEAD_DIM
NA_WIN_ROWS = 8
NA_WIN_COLS = 16
SC_WIDTH = 1024
RW_COLS = 3 * RW_WIDTH + 2 * DECAY_LORA + 2 * ICLR_LORA + GATE_LORA
RW_COLS_PAD = 3 * RW_WIDTH + 2 * DECAY_LORA + 2 * ICLR_LORA + GATE_LORA_PAD
N_EXPERTS = 32
TOP_K = 4
D_EXPERT = 512
SWIGLU_ALPHA = 1.702
SWIGLU_LIMIT = 7.0
MOE_BLOCK = 256
ROPE_BASE = 10000.0
LN_EPS = 1e-6
NEG_INF = -1e30

LANE = 128
ROW_BLK = 256
MM_TM = 1024
MM_TN = 512
SCAN_TT = 32
NA_GROUP = 4
NA_KROWS = 12
VMEM_LIMIT = 56 * 1024 * 1024

HI = lax.Precision.HIGHEST
F32 = jnp.float32
BF16 = jnp.bfloat16


def _cparams(sem):
    return pltpu.CompilerParams(dimension_semantics=sem, vmem_limit_bytes=VMEM_LIMIT)


def _ln(x):
    mu = jnp.mean(x, axis=-1, keepdims=True)
    xc = x - mu
    var = jnp.mean(xc * xc, axis=-1, keepdims=True)
    return xc * lax.rsqrt(var + LN_EPS)


def _sigmoid(x):
    return 1.0 / (1.0 + jnp.exp(-x))


def _mod_kernel(c_ref, w_ref, b_ref, o_ref):
    c = c_ref[...]
    s = (c * _sigmoid(c)).astype(BF16)
    o_ref[0] = jnp.dot(s, w_ref[0].astype(BF16), preferred_element_type=F32) + b_ref[0]


def _modulation(cc, w_mod, b_mod):
    depth, d, n = w_mod.shape
    tn = 1024
    return pl.pallas_call(
        _mod_kernel,
        grid=(depth, n // tn),
        in_specs=[pl.BlockSpec((16, d), lambda l, j: (0, 0)),
                  pl.BlockSpec((1, d, tn), lambda l, j: (l, 0, j)),
                  pl.BlockSpec((1, 1, tn), lambda l, j: (l, 0, j))],
        out_specs=pl.BlockSpec((1, 16, tn), lambda l, j: (l, 0, j)),
        out_shape=jax.ShapeDtypeStruct((depth, 16, n), F32),
        compiler_params=_cparams(("arbitrary", "arbitrary")),
        name="modulation",
    )(cc, w_mod, b_mod.reshape(depth, 1, n))


def _mod_spec(chunk, d, nblk_per_batch):
    def imap(i):
        row = jnp.where(i % nblk_per_batch == 0, 8, i // nblk_per_batch)
        return (row, 0, chunk)
    return pl.BlockSpec((1, 1, d), imap)


def _lnmod_kernel(x_ref, sh_ref, sc_ref, o_ref):
    o_ref[...] = (_ln(x_ref[...]) * (1.0 + sc_ref[0]) + sh_ref[0]).astype(o_ref.dtype)


def _lnmod(x, mods3, nbb):
    n, d = x.shape
    return pl.pallas_call(
        _lnmod_kernel,
        grid=(n // ROW_BLK,),
        in_specs=[pl.BlockSpec((ROW_BLK, d), lambda i: (i, 0)),
                  _mod_spec(0, d, nbb), _mod_spec(1, d, nbb)],
        out_specs=pl.BlockSpec((ROW_BLK, d), lambda i: (i, 0)),
        out_shape=jax.ShapeDtypeStruct((n, d), BF16),
        compiler_params=_cparams(("parallel",)),
        name="ln_mod",
    )(x, mods3, mods3)


def _mm_kernel(a_ref, b_ref, o_ref):
    o_ref[...] = jnp.dot(a_ref[...], b_ref[...], preferred_element_type=F32).astype(o_ref.dtype)


def _matmul(a, b, out_dtype=F32):
    m, k = a.shape
    _, n = b.shape
    tm = MM_TM if m % MM_TM == 0 else ROW_BLK
    return pl.pallas_call(
        _mm_kernel,
        grid=(m // tm, n // MM_TN),
        in_specs=[pl.BlockSpec((tm, k), lambda i, j: (i, 0)),
                  pl.BlockSpec((k, MM_TN), lambda i, j: (0, j))],
        out_specs=pl.BlockSpec((tm, MM_TN), lambda i, j: (i, j)),
        out_shape=jax.ShapeDtypeStruct((m, n), out_dtype),
        compiler_params=_cparams(("parallel", "arbitrary")),
        name="in_proj",
    )(a, b)


def _halo_specs(width, col_blk, nblk):
    per = ROW_BLK // 8

    def prev_map(i):
        return (jnp.maximum(i * per - 1, 0), col_blk)

    def next_map(i):
        return (jnp.minimum((i + 1) * per, nblk * per - 1), col_blk)

    return (pl.BlockSpec((8, width), prev_map), pl.BlockSpec((8, width), next_map))


def _shifted(p, prev8, next8, i, nbb):
    pos = i % nbb
    has_prev = pos >= 2
    has_next = jnp.logical_and(pos >= 1, pos <= nbb - 2)
    rows = lax.broadcasted_iota(jnp.int32, p.shape, 0)
    prow = jnp.where(has_prev, prev8[7:8, :], 0.0)
    nrow = jnp.where(has_next, next8[0:1, :], 0.0)
    p_prev = jnp.where(rows == 0, prow, pltpu.roll(p, 1, 0))
    p_next = jnp.where(rows == p.shape[0] - 1, nrow, pltpu.roll(p, p.shape[0] - 1, 0))
    return p_prev, p_next


def _seg_sum(x, e_ref, et_ref):
    s = jnp.dot(x, e_ref[...], precision=HI, preferred_element_type=F32)
    return jnp.dot(s, et_ref[...], precision=HI, preferred_element_type=F32)


def _swap_halves(x):
    lanes = lax.broadcasted_iota(jnp.int32, x.shape, 1)
    first = (lanes % HEAD_DIM) < (HEAD_DIM // 2)
    n = x.shape[1]
    return jnp.where(first, pltpu.roll(x, n - HEAD_DIM // 2, 1), pltpu.roll(x, HEAD_DIM // 2, 1))


def _prep_kernel(nbb, p_ref, pp_ref, pn_ref, mu_ref, w0_ref, wl_ref, a0_ref, al_ref, gl_ref,
                 kk_ref, ka_ref, rk_ref, cos_ref, sin_ref, e_ref, et_ref,
                 r_o, kap_o, v_o, w_o, kd_o, b_o, bonus_o, g_o):
    i = pl.program_id(0)
    c = RW_WIDTH
    p = p_ref[...]
    p_prev, p_next = _shifted(p, pp_ref[...], pn_ref[...], i, nbb)
    pm = p + mu_ref[0:1, :] * (p_prev - p) + mu_ref[1:2, :] * (p_next - p)
    r, k, v = pm[:, :c], pm[:, c:2 * c], pm[:, 2 * c:3 * c]
    o = 3 * c
    wl = jnp.tanh(pm[:, o:o + 2 * DECAY_LORA])
    o += 2 * DECAY_LORA
    al = pm[:, o:o + 2 * ICLR_LORA]
    o += 2 * ICLR_LORA
    gl = _sigmoid(pm[:, o:o + GATE_LORA_PAD])
    g_o[...] = jnp.dot(gl, gl_ref[...], precision=HI, preferred_element_type=F32)
    kk = k * kk_ref[...]
    ss = _seg_sum(kk * kk, e_ref, et_ref)
    kk = kk / jnp.maximum(jnp.sqrt(ss), 1e-12)
    cos = jnp.concatenate([cos_ref[...]] * (c // LANE), axis=1)
    sin = jnp.concatenate([sin_ref[...]] * (c // LANE), axis=1)

    def rope(x):
        return x * cos + _swap_halves(x) * sin

    kap = rope(kk)
    r_o[...] = rope(r)
    kap_o[...] = kap
    v_o[...] = v
    k_sum = jnp.zeros_like(k)
    for d in range(2):
        w_raw = w0_ref[d:d + 1, :] + jnp.dot(wl, wl_ref[d], precision=HI, preferred_element_type=F32)
        w_o[d] = jnp.exp(-math.exp(-0.5) * _sigmoid(w_raw))
        a = _sigmoid(a0_ref[d:d + 1, :] + jnp.dot(al, al_ref[d], precision=HI, preferred_element_type=F32))
        kd = k * (1.0 + (a - 1.0) * ka_ref[...])
        k_sum = k_sum + kd
        kd_o[d] = rope(kd)
        b_o[d] = kap * a
    bonus_o[...] = _seg_sum(r * k_sum * rk_ref[...], e_ref, et_ref) * v


def _rwkv_prepare(proj, lw, consts, nbb):
    n = proj.shape[0]
    c = RW_WIDTH
    nblk = n // ROW_BLK
    prev_spec, next_spec = _halo_specs(RW_COLS_PAD, 0, nblk)
    full = lambda shape: pl.BlockSpec(shape, lambda i: (0,) * len(shape))
    row_c = pl.BlockSpec((ROW_BLK, c), lambda i: (i, 0))
    row_2c = pl.BlockSpec((2, ROW_BLK, c), lambda i: (0, i, 0))
    tab = pl.BlockSpec((ROW_BLK, LANE), lambda i: (i % nbb, 0))
    outs = pl.pallas_call(
        functools.partial(_prep_kernel, nbb),
        grid=(nblk,),
        in_specs=[pl.BlockSpec((ROW_BLK, RW_COLS_PAD), lambda i: (i, 0)), prev_spec, next_spec,
                  full((2, RW_COLS_PAD)), full((2, c)), full((2, 2 * DECAY_LORA, c)), full((2, c)),
                  full((2, 2 * ICLR_LORA, c)), full((GATE_LORA_PAD, c)), full((1, c)), full((1, c)),
                  full((1, c)), tab, tab, full((c, LANE)), full((LANE, c))],
        out_specs=[row_c, row_c, row_c, row_2c, row_2c, row_2c, row_c, row_c],
        out_shape=[jax.ShapeDtypeStruct((n, c), F32)] * 3 + [jax.ShapeDtypeStruct((2, n, c), F32)] * 3
        + [jax.ShapeDtypeStruct((n, c), F32)] * 2,
        compiler_params=_cparams(("parallel",)),
        name="rwkv_prepare",
    )(proj, proj, proj, lw["mu"], lw["w0"], lw["w_lora"], lw["a0"], lw["a_lora"], lw["g_lora"],
      lw["k_k"], lw["k_a"], lw["r_k"], consts["cos"], consts["sin"], consts["seg"], consts["seg_t"])
    return outs


def _scan_kernel(r_ref, kap_ref, v_ref, w_ref, kd_ref, b_ref, o_ref, s_ref):
    d = pl.program_id(0)
    tt = r_ref.shape[0]

    @pl.when(pl.program_id(1) == 0)
    def _():
        s_ref[...] = jnp.zeros_like(s_ref)

    def step(i, carry):
        t = jnp.where(d == 0, i, tt - 1 - i)
        skk = s_ref[0] * kap_ref[t, 0:1, :]
        for k in range(1, HEAD_DIM):
            skk = skk + s_ref[k] * kap_ref[t, k:k + 1, :]
        v = v_ref[t]
        out = None
        for k in range(HEAD_DIM):
            s_new = (s_ref[k] * w_ref[0, t, k:k + 1, :] - skk * b_ref[0, t, k:k + 1, :]
                     + v * kd_ref[0, t, k:k + 1, :])
            s_ref[k] = s_new
            term = s_new * r_ref[t, k:k + 1, :]
            out = term if out is None else out + term
        o_ref[0, t] = out
        return carry

    lax.fori_loop(0, tt, step, 0)


def _scan_block(n_ctx_blk, nb):
    def blk(d, j):
        rev = jnp.where(j < n_ctx_blk, n_ctx_blk - 1 - j, nb - 1 - (j - n_ctx_blk))
        return jnp.where(d == 0, j, rev)
    return blk


def _wkv_scan(r, kap, v, w, kd, b, ctx_len):
    t, _, lanes = r.shape
    nb = t // SCAN_TT
    blk = _scan_block(ctx_len // SCAN_TT, nb)
    shared = pl.BlockSpec((SCAN_TT, HEAD_DIM, lanes), lambda d, j: (blk(d, j), 0, 0))
    per_dir = pl.BlockSpec((1, SCAN_TT, HEAD_DIM, lanes), lambda d, j: (d, blk(d, j), 0, 0))
    return pl.pallas_call(
        _scan_kernel,
        grid=(2, nb),
        in_specs=[shared, shared, shared, per_dir, per_dir, per_dir],
        out_specs=per_dir,
        out_shape=jax.ShapeDtypeStruct((2, t, HEAD_DIM, lanes), F32),
        scratch_shapes=[pltpu.VMEM((HEAD_DIM, HEAD_DIM, lanes), F32)],
        compiler_params=_cparams(("arbitrary", "arbitrary")),
        name="wkv_scan",
    )(r, kap, v, w, kd, b)


def _rwkv_out_kernel(wkv_ref, bonus_ref, g_ref, gw_ref, gb_ref, e_ref, et_ref, o_ref):
    x = wkv_ref[0] + wkv_ref[1]
    inv_n = 1.0 / HEAD_DIM
    mu = _seg_sum(x, e_ref, et_ref) * inv_n
    xc = x - mu
    var = _seg_sum(xc * xc, e_ref, et_ref) * inv_n
    y = xc * lax.rsqrt(var + GN_EPS) * gw_ref[...] + gb_ref[...]
    o_ref[...] = ((y + bonus_ref[...]) * g_ref[...]).astype(o_ref.dtype)


def _rwkv_out(wkv, bonus, g, lw, consts):
    n, c = bonus.shape
    full = lambda shape: pl.BlockSpec(shape, lambda i: (0,) * len(shape))
    row_c = pl.BlockSpec((ROW_BLK, c), lambda i: (i, 0))
    return pl.pallas_call(
        _rwkv_out_kernel,
        grid=(n // ROW_BLK,),
        in_specs=[pl.BlockSpec((2, ROW_BLK, c), lambda i: (0, i, 0)), row_c, row_c,
                  full((1, c)), full((1, c)), full((c, LANE)), full((LANE, c))],
        out_specs=row_c,
        out_shape=jax.ShapeDtypeStruct((n, c), BF16),
        compiler_params=_cparams(("parallel",)),
        name="rwkv_out",
    )(wkv, bonus, g, lw["gn_w"], lw["gn_b"], consts["seg"], consts["seg_t"])


def _conv_kernel(nbb, bg_ref, cg_ref, x_ref, cgp_ref, xp_ref, cgn_ref, xn_ref, w_ref, o_ref):
    i = pl.program_id(0)
    u = cg_ref[...] * x_ref[...]
    u_prev, u_next = _shifted(u, cgp_ref[...] * xp_ref[...], cgn_ref[...] * xn_ref[...], i, nbb)
    y = w_ref[0:1, :] * u_prev + w_ref[1:2, :] * u + w_ref[2:3, :] * u_next
    o_ref[...] = (bg_ref[...] * y).astype(o_ref.dtype)


def _short_conv(proj, conv_w, col0, nbb):
    n = proj.shape[0]
    cw = MM_TN
    nc = SC_WIDTH // cw
    assert col0 % cw == 0
    cb = col0 // cw
    nblk = n // ROW_BLK
    blk = lambda s: pl.BlockSpec((ROW_BLK, cw), lambda i, j: (i, cb + s * nc + j))

    def halos(s):
        prev_spec, next_spec = _halo_specs(cw, 0, nblk)
        pm, nm = prev_spec.index_map, next_spec.index_map
        return (pl.BlockSpec((8, cw), lambda i, j: (pm(i)[0], cb + s * nc + j)),
                pl.BlockSpec((8, cw), lambda i, j: (nm(i)[0], cb + s * nc + j)))

    cg_prev, cg_next = halos(1)
    x_prev, x_next = halos(2)
    return pl.pallas_call(
        functools.partial(_conv_kernel, nbb),
        grid=(nblk, nc),
        in_specs=[blk(0), blk(1), blk(2), cg_prev, x_prev, cg_next, x_next,
                  pl.BlockSpec((8, cw), lambda i, j: (0, j))],
        out_specs=pl.BlockSpec((ROW_BLK, cw), lambda i, j: (i, j)),
        out_shape=jax.ShapeDtypeStruct((n, SC_WIDTH), BF16),
        compiler_params=_cparams(("parallel", "arbitrary")),
        name="short_conv",
    )(proj, proj, proj, proj, proj, proj, proj, conv_w)


def _na_tables(rows):
    kr = min(NA_WIN_ROWS, rows)
    n_groups = rows // NA_GROUP
    krows = min(NA_KROWS, rows)
    n_dr = 2 * NA_WIN_ROWS - 1
    dr = np.full((n_groups, NA_GROUP, krows), n_dr, np.int32)
    bases = []
    for g in range(n_groups):
        r0 = g * NA_GROUP
        base = int(np.clip(np.clip(r0 - kr // 2, 0, rows - kr), 0, rows - krows))
        bases.append(base)
        for rl in range(NA_GROUP):
            r = r0 + rl
            rs = int(np.clip(r - kr // 2, 0, rows - kr))
            for j in range(krows):
                if rs <= base + j < rs + kr:
                    dr[g, rl, j] = base + j - r + NA_WIN_ROWS - 1
            assert (dr[g, rl] < n_dr).sum() == kr
    uniq, table_of = [], []
    for g in range(n_groups):
        for u, gu in enumerate(uniq):
            if np.array_equal(dr[g], dr[gu]):
                table_of.append(u)
                break
        else:
            table_of.append(len(uniq))
            uniq.append(g)
    return dr[uniq], tuple(bases), tuple(table_of)


def _na_kernel(ctx_len, bases, table_of, q_ref, k_ref, v_ref, bias_ref, o_ref):
    scale = HEAD_DIM ** -0.5
    nq = NA_GROUP * GRID_W
    nk = bias_ref.shape[3]
    outs_heads = []
    for h in range(2):
        sl = slice(h * HEAD_DIM, (h + 1) * HEAD_DIM)
        q = (q_ref[:, sl] * scale).astype(BF16)
        k = k_ref[:, sl].astype(BF16)
        v = v_ref[:, sl].astype(BF16)
        kc, vc = k[:ctx_len], v[:ctx_len]
        dn = (((1,), (1,)), ((), ()))
        s = lax.dot_general(q[:ctx_len], kc, dn, preferred_element_type=F32)
        s = s - jnp.max(s, axis=-1, keepdims=True)
        e = jnp.exp(s)
        p = (e / jnp.sum(e, axis=-1, keepdims=True)).astype(BF16)
        pieces = [jnp.dot(p, vc, preferred_element_type=F32)]
        for g, base in enumerate(bases):
            q0 = ctx_len + g * nq
            k0 = ctx_len + base * GRID_W
            qg = q[q0:q0 + nq]
            s_win = (lax.dot_general(qg, k[k0:k0 + nk], dn, preferred_element_type=F32)
                     + bias_ref[table_of[g], h])
            s_ctx = lax.dot_general(qg, kc, dn, preferred_element_type=F32)
            m = jnp.maximum(jnp.max(s_win, axis=-1, keepdims=True), jnp.max(s_ctx, axis=-1, keepdims=True))
            e_win = jnp.exp(s_win - m)
            e_ctx = jnp.exp(s_ctx - m)
            inv = 1.0 / (jnp.sum(e_win, axis=-1, keepdims=True) + jnp.sum(e_ctx, axis=-1, keepdims=True))
            acc = jnp.dot((e_win * inv).astype(BF16), v[k0:k0 + nk], preferred_element_type=F32)
            acc = acc + jnp.dot((e_ctx * inv).astype(BF16), vc, preferred_element_type=F32)
            pieces.append(acc)
        outs_heads.append(jnp.concatenate(pieces, axis=0))
    o_ref[...] = jnp.concatenate(outs_heads, axis=1).astype(o_ref.dtype)


def _na_bias(rpb, tables):
    dr = tables[0]
    n_tab, _, krows = dr.shape
    qc = np.arange(GRID_W)[:, None]
    kc = np.arange(GRID_W)[None, :]
    wstart = np.clip(qc - NA_WIN_COLS // 2, 0, GRID_W - NA_WIN_COLS)
    col_ok = (kc >= wstart) & (kc < wstart + NA_WIN_COLS)
    dc = np.clip(kc - qc + NA_WIN_COLS - 1, 0, 2 * NA_WIN_COLS - 2)
    blocks = jnp.where(col_ok, rpb[:, :, dc], NEG_INF)
    blocks = jnp.concatenate([blocks, jnp.full_like(blocks[:, :1], NEG_INF)], axis=1)
    bias = blocks[:, dr.reshape(-1)].reshape(NA_HEADS, n_tab, NA_GROUP, krows, GRID_W, GRID_W)
    bias = jnp.transpose(bias, (1, 0, 2, 4, 3, 5))
    return bias.reshape(n_tab, NA_HEADS, NA_GROUP * GRID_W, krows * GRID_W)


def _na_attention(proj, bias, tables, col0, batch, t_len, ctx_len):
    n = proj.shape[0]
    cb = col0 // LANE
    hp = NA_HEADS // 2
    n_tab, _, nq, nk = bias.shape
    seq = lambda j: pl.BlockSpec((t_len, LANE), lambda h, b: (b, cb + j * hp + h))
    return pl.pallas_call(
        functools.partial(_na_kernel, ctx_len, tables[1], tables[2]),
        grid=(hp, batch),
        in_specs=[seq(0), seq(1), seq(2),
                  pl.BlockSpec((n_tab, 2, nq, nk), lambda h, b: (0, h, 0, 0))],
        out_specs=pl.BlockSpec((t_len, LANE), lambda h, b: (b, h)),
        out_shape=jax.ShapeDtypeStruct((n, NA_WIDTH), BF16),
        compiler_params=_cparams(("arbitrary", "arbitrary")),
        name="na_attention",
    )(proj, proj, proj, bias)


def _merge1_kernel(yr_ref, yn_ref, ys_ref, wr_ref, wn_ref, ws_ref, g0_ref, g1_ref, g2_ref,
                   b0_ref, b1_ref, b2_ref, o_ref):
    m = _sigmoid(g0_ref[...] + b0_ref[...]) * jnp.dot(yr_ref[...], wr_ref[...], preferred_element_type=F32)
    m = m + _sigmoid(g1_ref[...] + b1_ref[...]) * jnp.dot(yn_ref[...], wn_ref[...], preferred_element_type=F32)
    m = m + _sigmoid(g2_ref[...] + b2_ref[...]) * jnp.dot(ys_ref[...], ws_ref[...], preferred_element_type=F32)
    o_ref[...] = m.astype(o_ref.dtype)


def _merge1(y_rw, y_na, y_sc, lw, proj, gate_col0, d):
    n, c = y_rw.shape
    tm, tn = 512, MM_TN
    gb = gate_col0 // tn
    nd = d // tn
    ysp = pl.BlockSpec((tm, c), lambda i, j: (i, 0))
    wsp = pl.BlockSpec((c, tn), lambda i, j: (0, j))
    gsp = lambda br: pl.BlockSpec((tm, tn), lambda i, j: (i, gb + br * nd + j))
    bsp = lambda br: pl.BlockSpec((1, tn), lambda i, j: (0, br * nd + j))
    return pl.pallas_call(
        _merge1_kernel,
        grid=(n // tm, nd),
        in_specs=[ysp, ysp, ysp, wsp, wsp, wsp, gsp(0), gsp(1), gsp(2), bsp(0), bsp(1), bsp(2)],
        out_specs=pl.BlockSpec((tm, tn), lambda i, j: (i, j)),
        out_shape=jax.ShapeDtypeStruct((n, d), BF16),
        compiler_params=_cparams(("parallel", "arbitrary")),
        name="merge_branches",
    )(y_rw, y_na, y_sc, lw["w_out_rw"], lw["w_out_na"], lw["w_out_sc"], proj, proj, proj,
      lw["b_gate"], lw["b_gate"], lw["b_gate"])


def _merge2_kernel(alpha, m_ref, w_ref, x_ref, g1_ref, lg_ref, lb_ref, sh_ref, sc_ref, rw_ref, rb_ref,
                   x1_ref, h2_ref, lo_ref):
    mix = jnp.dot(m_ref[...], w_ref[...], preferred_element_type=F32)
    x1 = _ln(alpha * x_ref[...] + g1_ref[0] * mix) * lg_ref[...] + lb_ref[...]
    x1_ref[...] = x1
    h2 = _ln(x1) * (1.0 + sc_ref[0]) + sh_ref[0]
    h2_ref[...] = h2.astype(h2_ref.dtype)
    lo_ref[...] = jnp.dot(h2, rw_ref[...], precision=HI, preferred_element_type=F32) + rb_ref[...]


def _merge2(m, x, mods3, lw, nbb, alpha):
    n, d = x.shape
    full = lambda shape: pl.BlockSpec(shape, lambda i: (0,) * len(shape))
    row = pl.BlockSpec((ROW_BLK, d), lambda i: (i, 0))
    return pl.pallas_call(
        functools.partial(_merge2_kernel, alpha),
        grid=(n // ROW_BLK,),
        in_specs=[row, full((d, d)), row, _mod_spec(2, d, nbb), full((1, d)), full((1, d)),
                  _mod_spec(3, d, nbb), _mod_spec(4, d, nbb), full((d, LANE)), full((1, LANE))],
        out_specs=[row, row, pl.BlockSpec((ROW_BLK, LANE), lambda i: (i, 0))],
        out_shape=[jax.ShapeDtypeStruct((n, d), F32), jax.ShapeDtypeStruct((n, d), BF16),
                   jax.ShapeDtypeStruct((n, LANE), F32)],
        compiler_params=_cparams(("parallel",)),
        name="merge_out_ln",
    )(m, lw["w_merge"], x, mods3, lw["ln1_g"], lw["ln1_b"], mods3, mods3, lw["router_w"], lw["router_b"])


def _route_kernel(lo_ref, gate_ref, idx_ref, rank_ref, counts_ref, carry_ref):
    x = lo_ref[...]
    lanes = lax.broadcasted_iota(jnp.int32, x.shape, 1).astype(F32)
    x = jnp.where(lanes < N_EXPERTS, x, -jnp.inf)
    vals, idxs = [], []
    for _ in range(TOP_K):
        m = jnp.max(x, axis=-1, keepdims=True)
        sel = jnp.min(jnp.where(x == m, lanes, float(LANE)), axis=-1, keepdims=True)
        vals.append(m)
        idxs.append(sel)
        x = jnp.where(lanes == sel, -jnp.inf, x)
    es = [jnp.exp(v - vals[0]) for v in vals]
    tot = es[0] + es[1] + es[2] + es[3]
    @pl.when(pl.program_id(0) == 0)
    def _():
        carry_ref[...] = jnp.zeros_like(carry_ref)

    onehot = jnp.zeros(lo_ref.shape, F32)
    for j in range(TOP_K):
        onehot = onehot + jnp.where(lanes == idxs[j], 1.0, 0.0)
    nr = lo_ref.shape[0]
    below = (lax.broadcasted_iota(jnp.int32, (nr, nr), 1) < lax.broadcasted_iota(jnp.int32, (nr, nr), 0))
    prefix = jnp.dot(below.astype(BF16), onehot.astype(BF16), preferred_element_type=F32) + carry_ref[...]
    gate = jnp.zeros(lo_ref.shape, F32)
    idx = jnp.zeros(lo_ref.shape, F32)
    rank = jnp.zeros(lo_ref.shape, F32)
    for j in range(TOP_K):
        gate = jnp.where(lanes == j, es[j] / tot, gate)
        idx = jnp.where(lanes == j, idxs[j], idx)
        rank_j = jnp.sum(jnp.where(lanes == idxs[j], prefix, 0.0), axis=-1, keepdims=True)
        rank = jnp.where(lanes == j, rank_j, rank)
    gate_ref[...] = gate
    idx_ref[...] = idx.astype(jnp.int32)
    rank_ref[...] = rank.astype(jnp.int32)
    carry_ref[...] = carry_ref[...] + jnp.sum(onehot, axis=0, keepdims=True)
    counts_ref[...] = carry_ref[...].astype(jnp.int32)


def _route(logits):
    n = logits.shape[0]
    row = pl.BlockSpec((ROW_BLK, LANE), lambda i: (i, 0))
    return pl.pallas_call(
        _route_kernel,
        grid=(n // ROW_BLK,),
        in_specs=[row],
        out_specs=[row, row, row, pl.BlockSpec((1, LANE), lambda i: (0, 0))],
        out_shape=[jax.ShapeDtypeStruct((n, LANE), F32), jax.ShapeDtypeStruct((n, LANE), jnp.int32),
                   jax.ShapeDtypeStruct((n, LANE), jnp.int32), jax.ShapeDtypeStruct((1, LANE), jnp.int32)],
        scratch_shapes=[pltpu.VMEM((1, LANE), F32)],
        compiler_params=_cparams(("arbitrary",)),
        name="moe_route",
    )(logits)


def _expert_kernel(be_ref, x_ref, w1_ref, b1_ref, w2_ref, b2_ref, sel_ref, o_ref, w1b_ref, w2b_ref):
    i = pl.program_id(0)

    @pl.when(jnp.logical_or(i == 0, be_ref[i] != be_ref[jnp.maximum(i - 1, 0)]))
    def _():
        w1b_ref[...] = w1_ref[0].astype(BF16)
        w2b_ref[...] = w2_ref[0].astype(BF16)

    z = jnp.dot(x_ref[...], w1b_ref[...], preferred_element_type=F32) + b1_ref[0]
    z_glu = jnp.minimum(z, SWIGLU_LIMIT)
    z_lin = jnp.clip(pltpu.roll(z, z.shape[1] - 1, 1), -SWIGLU_LIMIT, SWIGLU_LIMIT)
    act = z_glu * _sigmoid(SWIGLU_ALPHA * z_glu) * (z_lin + 1.0)
    lanes = lax.broadcasted_iota(jnp.int32, act.shape, 1)
    act = jnp.where(lanes % 2 == 0, act, 0.0).astype(BF16)
    act = jnp.dot(act, sel_ref[...], preferred_element_type=F32).astype(BF16)
    o_ref[...] = jnp.dot(act, w2b_ref[...], preferred_element_type=F32) + b2_ref[0]


def _experts(xs, block_e, lw):
    n_slots, d = xs.shape
    n_blocks = n_slots // MOE_BLOCK
    f2 = 2 * D_EXPERT
    sel = jnp.asarray(np.arange(f2)[:, None] == 2 * np.arange(D_EXPERT)[None, :], BF16)
    grid_spec = pltpu.PrefetchScalarGridSpec(
        num_scalar_prefetch=1,
        grid=(n_blocks,),
        in_specs=[pl.BlockSpec((MOE_BLOCK, d), lambda i, be: (i, 0)),
                  pl.BlockSpec((1, d, f2), lambda i, be: (be[i], 0, 0)),
                  pl.BlockSpec((1, 1, f2), lambda i, be: (be[i], 0, 0)),
                  pl.BlockSpec((1, D_EXPERT, d), lambda i, be: (be[i], 0, 0)),
                  pl.BlockSpec((1, 1, d), lambda i, be: (be[i], 0, 0)),
                  pl.BlockSpec((f2, D_EXPERT), lambda i, be: (0, 0))],
        out_specs=pl.BlockSpec((MOE_BLOCK, d), lambda i, be: (i, 0)),
        scratch_shapes=[pltpu.VMEM((d, f2), BF16), pltpu.VMEM((D_EXPERT, d), BF16)],
    )
    return pl.pallas_call(
        _expert_kernel,
        grid_spec=grid_spec,
        out_shape=jax.ShapeDtypeStruct((n_slots, d), F32),
        compiler_params=_cparams(("arbitrary",)),
        name="moe_experts",
    )(block_e, xs, lw["moe_w1"], lw["moe_b1"], lw["moe_w2"], lw["moe_b2"], sel)


def _moe(h2, gate, idx, rank, counts, lw):
    n, d = h2.shape
    a = n * TOP_K
    counts = counts[0, :N_EXPERTS]
    padded = (counts + MOE_BLOCK - 1) // MOE_BLOCK * MOE_BLOCK
    pad_end = jnp.cumsum(padded)
    pad_start = pad_end - padded
    dest = pad_start[idx[:, :TOP_K]] + rank[:, :TOP_K]
    n_blocks = -(-a // MOE_BLOCK) + N_EXPERTS
    n_slots = n_blocks * MOE_BLOCK
    slot_tok = jnp.zeros(n_slots, jnp.int32).at[dest.reshape(a)].set(jnp.arange(a, dtype=jnp.int32) // TOP_K)
    block_start = jnp.arange(n_blocks, dtype=jnp.int32) * MOE_BLOCK
    block_e = jnp.minimum(jnp.sum(block_start[:, None] >= pad_end[None, :], axis=1), N_EXPERTS - 1).astype(jnp.int32)
    xs = h2[slot_tok]
    ys = _experts(xs, block_e + lw["moe_row0"], lw)
    return jnp.sum(ys[dest] * gate[:, :TOP_K, None], axis=1)


def _final_kernel(alpha, x_ref, y_ref, g2_ref, lg_ref, lb_ref, sh_ref, sc_ref, x2_ref, h_ref):
    x2 = _ln(alpha * x_ref[...] + g2_ref[0] * y_ref[...]) * lg_ref[...] + lb_ref[...]
    x2_ref[...] = x2
    h_ref[...] = (_ln(x2) * (1.0 + sc_ref[0]) + sh_ref[0]).astype(h_ref.dtype)


def _final(x1, y, mods3, mods3_next, lw, nbb, alpha):
    n, d = x1.shape
    full = lambda shape: pl.BlockSpec(shape, lambda i: (0,) * len(shape))
    row = pl.BlockSpec((ROW_BLK, d), lambda i: (i, 0))
    return pl.pallas_call(
        functools.partial(_final_kernel, alpha),
        grid=(n // ROW_BLK,),
        in_specs=[row, row, _mod_spec(5, d, nbb), full((1, d)), full((1, d)),
                  _mod_spec(0, d, nbb), _mod_spec(1, d, nbb)],
        out_specs=[row, row],
        out_shape=[jax.ShapeDtypeStruct((n, d), F32), jax.ShapeDtypeStruct((n, d), BF16)],
        compiler_params=_cparams(("parallel",)),
        name="residual_ln",
    )(x1, y, mods3, lw["ln2_g"], lw["ln2_b"], mods3_next, mods3_next)


def _rope_tables(ctx_len, seq):
    n_freq = HEAD_DIM // 4
    t = np.arange(seq)
    inv = np.power(ROPE_BASE, -np.arange(n_freq, dtype=np.float32) / n_freq).astype(np.float32)
    row = (t // GRID_W).astype(np.float32)
    col = (t % GRID_W).astype(np.float32)
    ang = jnp.asarray(np.concatenate([row[:, None] * inv, col[:, None] * inv], -1))
    cos = jnp.cos(ang)
    sin = jnp.sin(ang)
    cos_h = jnp.concatenate([cos, cos], -1)
    sin_h = jnp.concatenate([-sin, sin], -1)
    cos_t = jnp.concatenate([jnp.ones((ctx_len, HEAD_DIM), F32), cos_h], 0)
    sin_t = jnp.concatenate([jnp.zeros((ctx_len, HEAD_DIM), F32), sin_h], 0)
    return jnp.tile(cos_t, (1, LANE // HEAD_DIM)), jnp.tile(sin_t, (1, LANE // HEAD_DIM))


def _to_scan_layout(x, batch, t_len):
    lead = x.shape[:-2]
    x = x.reshape(lead + (batch, t_len, RW_HEADS, HEAD_DIM))
    nl = len(lead)
    perm = tuple(range(nl)) + (nl + 1, nl + 3, nl + 0, nl + 2)
    return jnp.transpose(x, perm).reshape(lead + (t_len, HEAD_DIM, batch * RW_HEADS))


def _pad_dir_lora(w):
    z = jnp.zeros_like(w[0])
    return jnp.stack([jnp.concatenate([w[0], z], 0), jnp.concatenate([z, w[1]], 0)])


def _from_scan_layout(o, batch, t_len):
    o = o.reshape(2, t_len, HEAD_DIM, batch, RW_HEADS)
    return jnp.transpose(o, (0, 3, 1, 4, 2)).reshape(2, batch * t_len, RW_WIDTH)


def kernel(x, c, ctx, c_ctx, w_mod, b_mod, w_in, rw_mu, rw_w0, rw_w_lora, rw_a0, rw_a_lora, rw_g_lora, rw_k_k, rw_k_a, rw_r_k, rw_gn_w, rw_gn_b, w_out_rw, na_rpb, w_out_na, sc_conv, w_out_sc, b_gate, w_merge, ln1_g, ln1_b, router_w, router_b, moe_w1, moe_b1, moe_w2, moe_b2, ln2_g, ln2_b):
    batch, seq, d = x.shape
    ctx_len = ctx.shape[1]
    depth = w_in.shape[0]
    t_len = ctx_len + seq
    n = batch * t_len
    nbb = t_len // ROW_BLK
    assert batch <= 8 and ctx_len == ROW_BLK and seq % (NA_GROUP * GRID_W) == 0
    alpha = (2 * depth) ** 0.25

    cc = jnp.zeros((16, d), F32).at[:batch].set(c).at[8].set(c_ctx)
    mods = _modulation(cc, w_mod, b_mod)
    mods3 = [mods[l].reshape(16, 1, 6 * d) for l in range(depth)]

    cos_t, sin_t = _rope_tables(ctx_len, seq)
    seg = (np.arange(RW_WIDTH)[:, None] // HEAD_DIM == np.arange(LANE)[None, :]).astype(np.float32)
    consts = dict(cos=cos_t, sin=sin_t, seg=jnp.asarray(seg), seg_t=jnp.asarray(seg.T))
    tables = _na_tables(seq // GRID_W)

    o_na = RW_COLS_PAD
    o_sc = o_na + 3 * NA_WIDTH
    o_gate = o_sc + 3 * SC_WIDTH
    pad_cols = RW_COLS_PAD - RW_COLS

    xs = jnp.concatenate([ctx, x], axis=1).reshape(n, d)
    h = _lnmod(xs, mods3[0], nbb)
    for l in range(depth):
        w_in_p = jnp.concatenate([w_in[l, :, :RW_COLS], jnp.zeros((d, pad_cols), F32), w_in[l, :, RW_COLS:]],
                                 axis=1).astype(BF16)
        lw = dict(
            mu=jnp.pad(rw_mu[l], ((0, 0), (0, pad_cols))), w0=rw_w0[l], w_lora=_pad_dir_lora(rw_w_lora[l]),
            a0=rw_a0[l], a_lora=_pad_dir_lora(rw_a_lora[l]), g_lora=jnp.pad(rw_g_lora[l], ((0, GATE_LORA_PAD - GATE_LORA), (0, 0))),
            k_k=rw_k_k[l].reshape(1, -1), k_a=rw_k_a[l].reshape(1, -1), r_k=rw_r_k[l].reshape(1, -1),
            gn_w=rw_gn_w[l].reshape(1, -1), gn_b=rw_gn_b[l].reshape(1, -1),
            w_out_rw=w_out_rw[l].astype(BF16), w_out_na=w_out_na[l].astype(BF16),
            w_out_sc=w_out_sc[l].astype(BF16), b_gate=b_gate[l].reshape(1, -1),
            w_merge=w_merge[l].astype(BF16), ln1_g=ln1_g[l].reshape(1, -1), ln1_b=ln1_b[l].reshape(1, -1),
            router_w=jnp.pad(router_w[l], ((0, 0), (0, LANE - N_EXPERTS))),
            router_b=jnp.pad(router_b[l], (0, LANE - N_EXPERTS)).reshape(1, -1),
            moe_row0=l * N_EXPERTS,
            moe_w1=moe_w1.reshape((depth * N_EXPERTS,) + moe_w1.shape[2:]),
            moe_b1=moe_b1.reshape(depth * N_EXPERTS, 1, -1),
            moe_w2=moe_w2.reshape((depth * N_EXPERTS,) + moe_w2.shape[2:]),
            moe_b2=moe_b2.reshape(depth * N_EXPERTS, 1, -1),
            ln2_g=ln2_g[l].reshape(1, -1), ln2_b=ln2_b[l].reshape(1, -1),
        )
        proj = _matmul(h, w_in_p)
        r_s, kap, v, w, kd, b, bonus, g = _rwkv_prepare(proj, lw, consts, nbb)
        tl = functools.partial(_to_scan_layout, batch=batch, t_len=t_len)
        wkv = _wkv_scan(tl(r_s), tl(kap), tl(v), tl(w), tl(kd), tl(b), ctx_len)
        y_rw = _rwkv_out(_from_scan_layout(wkv, batch, t_len), bonus, g, lw, consts)
        y_na = _na_attention(proj, _na_bias(na_rpb[l], tables), tables, o_na, batch, t_len, ctx_len)
        y_sc = _short_conv(proj, jnp.pad(sc_conv[l], ((0, 5), (0, 0))), o_sc, nbb)
        m = _merge1(y_rw, y_na, y_sc, lw, proj, o_gate, d)
        x1, h2, logits = _merge2(m, xs, mods3[l], lw, nbb, alpha)
        gate, idx, rank, counts = _route(logits)
        y = _moe(h2, gate, idx, rank, counts, lw)
        xs, h = _final(x1, y, mods3[l], mods3[min(l + 1, depth - 1)], lw, nbb, alpha)
    return xs.reshape(batch, t_len, d)[:, ctx_len:]
```

```python
import functools
import math

import numpy as np
import jax
import jax.numpy as jnp
from jax import lax
from jax.experimental import pallas as pl
from jax.experimental.pallas import tpu as pltpu

GRID_W = 64
RW_HEADS = 16
HEAD_DIM = 64
RW_WIDTH = RW_HEADS * HEAD_DIM
DECAY_LORA = 64
ICLR_LORA = 64
GATE_LORA = 160
GATE_LORA_PAD = 256
GN_EPS = 64e-5
NA_HEADS = 16
NA_WIDTH = NA_HEADS * HEAD_DIM
NA_WIN_ROWS = 8
NA_WIN_COLS = 16
SC_WIDTH = 1024
RW_COLS = 3 * RW_WIDTH + 2 * DECAY_LORA + 2 * ICLR_LORA + GATE_LORA
RW_COLS_PAD = 3 * RW_WIDTH + 2 * DECAY_LORA + 2 * ICLR_LORA + GATE_LORA_PAD
N_EXPERTS = 32
TOP_K = 4
D_EXPERT = 512
SWIGLU_ALPHA = 1.702
SWIGLU_LIMIT = 7.0
MOE_BLOCK = 256
ROPE_BASE = 10000.0
LN_EPS = 1e-6
NEG_INF = -1e30

LANE = 128
ROW_BLK = 256
MM_TM = 1024
MM_TN = 512
SCAN_TT = 32
SCAN_VSPLIT = 2
SCAN_NACC = 4
SCAN_RELAYOUT_UNROLL = 4
NA_GROUP = 4
NA_KROWS = 12
VMEM_LIMIT = 56 * 1024 * 1024

HI = lax.Precision.HIGHEST
F32 = jnp.float32
BF16 = jnp.bfloat16


def _cparams(sem):
    return pltpu.CompilerParams(dimension_semantics=sem, vmem_limit_bytes=VMEM_LIMIT)


def _ln(x):
    mu = jnp.mean(x, axis=-1, keepdims=True)
    xc = x - mu
    var = jnp.mean(xc * xc, axis=-1, keepdims=True)
    return xc * lax.rsqrt(var + LN_EPS)


def _sigmoid(x):
    return 1.0 / (1.0 + jnp.exp(-x))


def _mod_kernel(c_ref, w_ref, b_ref, o_ref):
    c = c_ref[...]
    s = (c * _sigmoid(c)).astype(BF16)
    o_ref[0] = jnp.dot(s, w_ref[0].astype(BF16), preferred_element_type=F32) + b_ref[0]


def _modulation(cc, w_mod, b_mod):
    depth, d, n = w_mod.shape
    tn = 1024
    return pl.pallas_call(
        _mod_kernel,
        grid=(depth, n // tn),
        in_specs=[pl.BlockSpec((16, d), lambda l, j: (0, 0)),
                  pl.BlockSpec((1, d, tn), lambda l, j: (l, 0, j)),
                  pl.BlockSpec((1, 1, tn), lambda l, j: (l, 0, j))],
        out_specs=pl.BlockSpec((1, 16, tn), lambda l, j: (l, 0, j)),
        out_shape=jax.ShapeDtypeStruct((depth, 16, n), F32),
        compiler_params=_cparams(("arbitrary", "arbitrary")),
        name="modulation",
    )(cc, w_mod, b_mod.reshape(depth, 1, n))


def _mod_spec(chunk, d, nblk_per_batch):
    def imap(i):
        row = jnp.where(i % nblk_per_batch == 0, 8, i // nblk_per_batch)
        return (row, 0, chunk)
    return pl.BlockSpec((1, 1, d), imap)


def _lnmod_kernel(x_ref, sh_ref, sc_ref, o_ref):
    o_ref[...] = (_ln(x_ref[...]) * (1.0 + sc_ref[0]) + sh_ref[0]).astype(o_ref.dtype)


def _lnmod(x, mods3, nbb):
    n, d = x.shape
    return pl.pallas_call(
        _lnmod_kernel,
        grid=(n // ROW_BLK,),
        in_specs=[pl.BlockSpec((ROW_BLK, d), lambda i: (i, 0)),
                  _mod_spec(0, d, nbb), _mod_spec(1, d, nbb)],
        out_specs=pl.BlockSpec((ROW_BLK, d), lambda i: (i, 0)),
        out_shape=jax.ShapeDtypeStruct((n, d), BF16),
        compiler_params=_cparams(("parallel",)),
        name="ln_mod",
    )(x, mods3, mods3)


def _mm_kernel(a_ref, b_ref, o_ref):
    o_ref[...] = jnp.dot(a_ref[...], b_ref[...], preferred_element_type=F32).astype(o_ref.dtype)


def _matmul(a, b, out_dtype=F32):
    m, k = a.shape
    _, n = b.shape
    tm = MM_TM if m % MM_TM == 0 else ROW_BLK
    return pl.pallas_call(
        _mm_kernel,
        grid=(m // tm, n // MM_TN),
        in_specs=[pl.BlockSpec((tm, k), lambda i, j: (i, 0)),
                  pl.BlockSpec((k, MM_TN), lambda i, j: (0, j))],
        out_specs=pl.BlockSpec((tm, MM_TN), lambda i, j: (i, j)),
        out_shape=jax.ShapeDtypeStruct((m, n), out_dtype),
        compiler_params=_cparams(("parallel", "arbitrary")),
        name="in_proj",
    )(a, b)


def _halo_specs(width, col_blk, nblk):
    per = ROW_BLK // 8

    def prev_map(i):
        return (jnp.maximum(i * per - 1, 0), col_blk)

    def next_map(i):
        return (jnp.minimum((i + 1) * per, nblk * per - 1), col_blk)

    return (pl.BlockSpec((8, width), prev_map), pl.BlockSpec((8, width), next_map))


def _shifted(p, prev8, next8, i, nbb):
    pos = i % nbb
    has_prev = pos >= 2
    has_next = jnp.logical_and(pos >= 1, pos <= nbb - 2)
    rows = lax.broadcasted_iota(jnp.int32, p.shape, 0)
    prow = jnp.where(has_prev, prev8[7:8, :], 0.0)
    nrow = jnp.where(has_next, next8[0:1, :], 0.0)
    p_prev = jnp.where(rows == 0, prow, pltpu.roll(p, 1, 0))
    p_next = jnp.where(rows == p.shape[0] - 1, nrow, pltpu.roll(p, p.shape[0] - 1, 0))
    return p_prev, p_next


def _seg_sum(x, e_ref, et_ref):
    s = jnp.dot(x, e_ref[...], precision=HI, preferred_element_type=F32)
    return jnp.dot(s, et_ref[...], precision=HI, preferred_element_type=F32)


def _swap_halves(x):
    lanes = lax.broadcasted_iota(jnp.int32, x.shape, 1)
    first = (lanes % HEAD_DIM) < (HEAD_DIM // 2)
    n = x.shape[1]
    return jnp.where(first, pltpu.roll(x, n - HEAD_DIM // 2, 1), pltpu.roll(x, HEAD_DIM // 2, 1))


def _prep_kernel(nbb, p_ref, pp_ref, pn_ref, mu_ref, w0_ref, wl_ref, a0_ref, al_ref, gl_ref,
                 kk_ref, ka_ref, rk_ref, cos_ref, sin_ref, e_ref, et_ref,
                 r_o, kap_o, v_o, w_o, kd_o, b_o, bonus_o, g_o):
    i = pl.program_id(0)
    c = RW_WIDTH
    p = p_ref[...]
    p_prev, p_next = _shifted(p, pp_ref[...], pn_ref[...], i, nbb)
    pm = p + mu_ref[0:1, :] * (p_prev - p) + mu_ref[1:2, :] * (p_next - p)
    r, k, v = pm[:, :c], pm[:, c:2 * c], pm[:, 2 * c:3 * c]
    o = 3 * c
    wl = jnp.tanh(pm[:, o:o + 2 * DECAY_LORA])
    o += 2 * DECAY_LORA
    al = pm[:, o:o + 2 * ICLR_LORA]
    o += 2 * ICLR_LORA
    gl = _sigmoid(pm[:, o:o + GATE_LORA_PAD])
    g_o[...] = jnp.dot(gl, gl_ref[...], precision=HI, preferred_element_type=F32)
    kk = k * kk_ref[...]
    ss = _seg_sum(kk * kk, e_ref, et_ref)
    kk = kk / jnp.maximum(jnp.sqrt(ss), 1e-12)
    cos = jnp.concatenate([cos_ref[...]] * (c // LANE), axis=1)
    sin = jnp.concatenate([sin_ref[...]] * (c // LANE), axis=1)

    def rope(x):
        return x * cos + _swap_halves(x) * sin

    kap = rope(kk)
    r_o[...] = rope(r)
    kap_o[...] = kap
    v_o[...] = v
    k_sum = jnp.zeros_like(k)
    for d in range(2):
        w_raw = w0_ref[d:d + 1, :] + jnp.dot(wl, wl_ref[d], precision=HI, preferred_element_type=F32)
        w_o[d] = jnp.exp(-math.exp(-0.5) * _sigmoid(w_raw))
        a = _sigmoid(a0_ref[d:d + 1, :] + jnp.dot(al, al_ref[d], precision=HI, preferred_element_type=F32))
        kd = k * (1.0 + (a - 1.0) * ka_ref[...])
        k_sum = k_sum + kd
        kd_o[d] = rope(kd)
        b_o[d] = kap * a
    bonus_o[...] = _seg_sum(r * k_sum * rk_ref[...], e_ref, et_ref) * v


def _rwkv_prepare(proj, lw, consts, nbb):
    n = proj.shape[0]
    c = RW_WIDTH
    nblk = n // ROW_BLK
    prev_spec, next_spec = _halo_specs(RW_COLS_PAD, 0, nblk)
    full = lambda shape: pl.BlockSpec(shape, lambda i: (0,) * len(shape))
    row_c = pl.BlockSpec((ROW_BLK, c), lambda i: (i, 0))
    row_2c = pl.BlockSpec((2, ROW_BLK, c), lambda i: (0, i, 0))
    tab = pl.BlockSpec((ROW_BLK, LANE), lambda i: (i % nbb, 0))
    outs = pl.pallas_call(
        functools.partial(_prep_kernel, nbb),
        grid=(nblk,),
        in_specs=[pl.BlockSpec((ROW_BLK, RW_COLS_PAD), lambda i: (i, 0)), prev_spec, next_spec,
                  full((2, RW_COLS_PAD)), full((2, c)), full((2, 2 * DECAY_LORA, c)), full((2, c)),
                  full((2, 2 * ICLR_LORA, c)), full((GATE_LORA_PAD, c)), full((1, c)), full((1, c)),
                  full((1, c)), tab, tab, full((c, LANE)), full((LANE, c))],
        out_specs=[row_c, row_c, row_c, row_2c, row_2c, row_2c, row_c, row_c],
        out_shape=[jax.ShapeDtypeStruct((n, c), F32)] * 3 + [jax.ShapeDtypeStruct((2, n, c), F32)] * 3
        + [jax.ShapeDtypeStruct((n, c), F32)] * 2,
        compiler_params=_cparams(("parallel",)),
        name="rwkv_prepare",
    )(proj, proj, proj, lw["mu"], lw["w0"], lw["w_lora"], lw["a0"], lw["a_lora"], lw["g_lora"],
      lw["k_k"], lw["k_a"], lw["r_k"], consts["cos"], consts["sin"], consts["seg"], consts["seg_t"])
    return outs


def _pair_to_lanes(xa, xb):
    nl = xa.shape[0]
    res = jnp.concatenate([xa, xb], axis=0).T
    top, bot = res[:HEAD_DIM], res[HEAD_DIM:]
    first = lax.broadcasted_iota(jnp.int32, top.shape, 1) < nl
    tile_a = jnp.where(first, top, pltpu.roll(bot, nl, 1))
    tile_b = jnp.where(first, pltpu.roll(top, nl, 1), bot)
    return tile_a, tile_b


def _scan_kernel(r_ref, kap_ref, v_ref, w_ref, kd_ref, b_ref, o_ref, s_ref, vec_ref, out_ref):
    d = pl.program_id(0)
    nb, tt = r_ref.shape[0], r_ref.shape[1]
    n_pair = r_ref.shape[2] // (2 * HEAD_DIM)
    nl = nb * n_pair
    vh = HEAD_DIM // SCAN_VSPLIT

    @pl.when(pl.program_id(1) == 0)
    def _():
        s_ref[...] = jnp.zeros_like(s_ref)

    def to_lanes(t, carry):
        slab = lambda x: jnp.concatenate([x[:, p * 2 * HEAD_DIM:(p + 1) * 2 * HEAD_DIM] for p in range(n_pair)], axis=0)
        pairs = ((kap_ref[:, t, :], r_ref[:, t, :]), (w_ref[0, :, t, :], kd_ref[0, :, t, :]),
                 (b_ref[0, :, t, :], v_ref[:, t, :]))
        for j, (xa, xb) in enumerate(pairs):
            tile_a, tile_b = _pair_to_lanes(slab(xa), slab(xb))
            vec_ref[t, 2 * j] = tile_a
            vec_ref[t, 2 * j + 1] = tile_b
        return carry

    lax.fori_loop(0, tt, to_lanes, 0, unroll=SCAN_RELAYOUT_UNROLL)

    def step(i, carry):
        t = jnp.where(d == 0, i, tt - 1 - i)
        for h in range(SCAN_VSPLIT):
            rows = slice(h * vh, (h + 1) * vh)
            acc = [None] * SCAN_NACC
            for k in range(HEAD_DIM):
                term = s_ref[k, rows] * vec_ref[t, 0, k:k + 1, :]
                acc[k % SCAN_NACC] = term if acc[k % SCAN_NACC] is None else acc[k % SCAN_NACC] + term
            skk = functools.reduce(lambda a, b: a + b, acc)
            v = vec_ref[t, 5, rows]
            acc = [None] * SCAN_NACC
            for k in range(HEAD_DIM):
                s_new = (s_ref[k, rows] * vec_ref[t, 2, k:k + 1, :] - skk * vec_ref[t, 4, k:k + 1, :]
                         + v * vec_ref[t, 3, k:k + 1, :])
                s_ref[k, rows] = s_new
                term = s_new * vec_ref[t, 1, k:k + 1, :]
                acc[k % SCAN_NACC] = term if acc[k % SCAN_NACC] is None else acc[k % SCAN_NACC] + term
            out_ref[t, rows] = functools.reduce(lambda a, b: a + b, acc)
        return carry

    lax.fori_loop(0, tt, step, 0)

    def to_tokens(t, carry):
        out = out_ref[t]
        back = jnp.concatenate([out, pltpu.roll(out, nl, 1)], axis=0).T[:nl]
        o_ref[0, :, t, :] = jnp.concatenate([back[p * nb:(p + 1) * nb] for p in range(n_pair)], axis=1)
        return carry

    lax.fori_loop(0, tt, to_tokens, 0, unroll=SCAN_RELAYOUT_UNROLL)


def _scan_block(n_ctx_blk, nb):
    def blk(d, j):
        rev = jnp.where(j < n_ctx_blk, n_ctx_blk - 1 - j, nb - 1 - (j - n_ctx_blk))
        return jnp.where(d == 0, j, rev)
    return blk


def _wkv_scan(r, kap, v, w, kd, b, ctx_len):
    batch, t, width = r.shape
    lanes = batch * width // HEAD_DIM
    nb = t // SCAN_TT
    blk = _scan_block(ctx_len // SCAN_TT, nb)
    shared = pl.BlockSpec((batch, SCAN_TT, width), lambda d, j: (0, blk(d, j), 0))
    per_dir = pl.BlockSpec((1, batch, SCAN_TT, width), lambda d, j: (d, 0, blk(d, j), 0))
    return pl.pallas_call(
        _scan_kernel,
        grid=(2, nb),
        in_specs=[shared, shared, shared, per_dir, per_dir, per_dir],
        out_specs=per_dir,
        out_shape=jax.ShapeDtypeStruct((2, batch, t, width), F32),
        scratch_shapes=[pltpu.VMEM((HEAD_DIM, HEAD_DIM, lanes), F32),
                        pltpu.VMEM((SCAN_TT, 6, HEAD_DIM, lanes), F32),
                        pltpu.VMEM((SCAN_TT, HEAD_DIM, lanes), F32)],
        compiler_params=_cparams(("arbitrary", "arbitrary")),
        name="wkv_scan",
    )(r, kap, v, w, kd, b)


def _rwkv_out_kernel(wkv_ref, bonus_ref, g_ref, gw_ref, gb_ref, e_ref, et_ref, o_ref):
    x = wkv_ref[0] + wkv_ref[1]
    inv_n = 1.0 / HEAD_DIM
    mu = _seg_sum(x, e_ref, et_ref) * inv_n
    xc = x - mu
    var = _seg_sum(xc * xc, e_ref, et_ref) * inv_n
    y = xc * lax.rsqrt(var + GN_EPS) * gw_ref[...] + gb_ref[...]
    o_ref[...] = ((y + bonus_ref[...]) * g_ref[...]).astype(o_ref.dtype)


def _rwkv_out(wkv, bonus, g, lw, consts):
    n, c = bonus.shape
    full = lambda shape: pl.BlockSpec(shape, lambda i: (0,) * len(shape))
    row_c = pl.BlockSpec((ROW_BLK, c), lambda i: (i, 0))
    return pl.pallas_call(
        _rwkv_out_kernel,
        grid=(n // ROW_BLK,),
        in_specs=[pl.BlockSpec((2, ROW_BLK, c), lambda i: (0, i, 0)), row_c, row_c,
                  full((1, c)), full((1, c)), full((c, LANE)), full((LANE, c))],
        out_specs=row_c,
        out_shape=jax.ShapeDtypeStruct((n, c), BF16),
        compiler_params=_cparams(("parallel",)),
        name="rwkv_out",
    )(wkv, bonus, g, lw["gn_w"], lw["gn_b"], consts["seg"], consts["seg_t"])


def _conv_kernel(nbb, bg_ref, cg_ref, x_ref, cgp_ref, xp_ref, cgn_ref, xn_ref, w_ref, o_ref):
    i = pl.program_id(0)
    u = cg_ref[...] * x_ref[...]
    u_prev, u_next = _shifted(u, cgp_ref[...] * xp_ref[...], cgn_ref[...] * xn_ref[...], i, nbb)
    y = w_ref[0:1, :] * u_prev + w_ref[1:2, :] * u + w_ref[2:3, :] * u_next
    o_ref[...] = (bg_ref[...] * y).astype(o_ref.dtype)


def _short_conv(proj, conv_w, col0, nbb):
    n = proj.shape[0]
    cw = MM_TN
    nc = SC_WIDTH // cw
    assert col0 % cw == 0
    cb = col0 // cw
    nblk = n // ROW_BLK
    blk = lambda s: pl.BlockSpec((ROW_BLK, cw), lambda i, j: (i, cb + s * nc + j))

    def halos(s):
        prev_spec, next_spec = _halo_specs(cw, 0, nblk)
        pm, nm = prev_spec.index_map, next_spec.index_map
        return (pl.BlockSpec((8, cw), lambda i, j: (pm(i)[0], cb + s * nc + j)),
                pl.BlockSpec((8, cw), lambda i, j: (nm(i)[0], cb + s * nc + j)))

    cg_prev, cg_next = halos(1)
    x_prev, x_next = halos(2)
    return pl.pallas_call(
        functools.partial(_conv_kernel, nbb),
        grid=(nblk, nc),
        in_specs=[blk(0), blk(1), blk(2), cg_prev, x_prev, cg_next, x_next,
                  pl.BlockSpec((8, cw), lambda i, j: (0, j))],
        out_specs=pl.BlockSpec((ROW_BLK, cw), lambda i, j: (i, j)),
        out_shape=jax.ShapeDtypeStruct((n, SC_WIDTH), BF16),
        compiler_params=_cparams(("parallel", "arbitrary")),
        name="short_conv",
    )(proj, proj, proj, proj, proj, proj, proj, conv_w)


def _na_tables(rows):
    kr = min(NA_WIN_ROWS, rows)
    n_groups = rows // NA_GROUP
    krows = min(NA_KROWS, rows)
    n_dr = 2 * NA_WIN_ROWS - 1
    dr = np.full((n_groups, NA_GROUP, krows), n_dr, np.int32)
    bases = []
    for g in range(n_groups):
        r0 = g * NA_GROUP
        base = int(np.clip(np.clip(r0 - kr // 2, 0, rows - kr), 0, rows - krows))
        bases.append(base)
        for rl in range(NA_GROUP):
            r = r0 + rl
            rs = int(np.clip(r - kr // 2, 0, rows - kr))
            for j in range(krows):
                if rs <= base + j < rs + kr:
                    dr[g, rl, j] = base + j - r + NA_WIN_ROWS - 1
            assert (dr[g, rl] < n_dr).sum() == kr
    uniq, table_of = [], []
    for g in range(n_groups):
        for u, gu in enumerate(uniq):
            if np.array_equal(dr[g], dr[gu]):
                table_of.append(u)
                break
        else:
            table_of.append(len(uniq))
            uniq.append(g)
    return dr[uniq], tuple(bases), tuple(table_of)


def _na_kernel(ctx_len, bases, table_of, q_ref, k_ref, v_ref, bias_ref, o_ref):
    scale = HEAD_DIM ** -0.5
    nq = NA_GROUP * GRID_W
    nk = bias_ref.shape[3]
    outs_heads = []
    for h in range(2):
        sl = slice(h * HEAD_DIM, (h + 1) * HEAD_DIM)
        q = (q_ref[:, sl] * scale).astype(BF16)
        k = k_ref[:, sl].astype(BF16)
        v = v_ref[:, sl].astype(BF16)
        kc, vc = k[:ctx_len], v[:ctx_len]
        dn = (((1,), (1,)), ((), ()))
        s = lax.dot_general(q[:ctx_len], kc, dn, preferred_element_type=F32)
        s = s - jnp.max(s, axis=-1, keepdims=True)
        e = jnp.exp(s)
        p = (e / jnp.sum(e, axis=-1, keepdims=True)).astype(BF16)
        pieces = [jnp.dot(p, vc, preferred_element_type=F32)]
        for g, base in enumerate(bases):
            q0 = ctx_len + g * nq
            k0 = ctx_len + base * GRID_W
            qg = q[q0:q0 + nq]
            s_win = (lax.dot_general(qg, k[k0:k0 + nk], dn, preferred_element_type=F32)
                     + bias_ref[table_of[g], h])
            s_ctx = lax.dot_general(qg, kc, dn, preferred_element_type=F32)
            m = jnp.maximum(jnp.max(s_win, axis=-1, keepdims=True), jnp.max(s_ctx, axis=-1, keepdims=True))
            e_win = jnp.exp(s_win - m)
            e_ctx = jnp.exp(s_ctx - m)
            inv = 1.0 / (jnp.sum(e_win, axis=-1, keepdims=True) + jnp.sum(e_ctx, axis=-1, keepdims=True))
            acc = jnp.dot((e_win * inv).astype(BF16), v[k0:k0 + nk], preferred_element_type=F32)
            acc = acc + jnp.dot((e_ctx * inv).astype(BF16), vc, preferred_element_type=F32)
            pieces.append(acc)
        outs_heads.append(jnp.concatenate(pieces, axis=0))
    o_ref[...] = jnp.concatenate(outs_heads, axis=1).astype(o_ref.dtype)


def _na_bias(rpb, tables):
    dr = tables[0]
    n_tab, _, krows = dr.shape
    qc = np.arange(GRID_W)[:, None]
    kc = np.arange(GRID_W)[None, :]
    wstart = np.clip(qc - NA_WIN_COLS // 2, 0, GRID_W - NA_WIN_COLS)
    col_ok = (kc >= wstart) & (kc < wstart + NA_WIN_COLS)
    dc = np.clip(kc - qc + NA_WIN_COLS - 1, 0, 2 * NA_WIN_COLS - 2)
    blocks = jnp.where(col_ok, rpb[:, :, dc], NEG_INF)
    blocks = jnp.concatenate([blocks, jnp.full_like(blocks[:, :1], NEG_INF)], axis=1)
    bias = blocks[:, dr.reshape(-1)].reshape(NA_HEADS, n_tab, NA_GROUP, krows, GRID_W, GRID_W)
    bias = jnp.transpose(bias, (1, 0, 2, 4, 3, 5))
    return bias.reshape(n_tab, NA_HEADS, NA_GROUP * GRID_W, krows * GRID_W)


def _na_attention(proj, bias, tables, col0, batch, t_len, ctx_len):
    n = proj.shape[0]
    cb = col0 // LANE
    hp = NA_HEADS // 2
    n_tab, _, nq, nk = bias.shape
    seq = lambda j: pl.BlockSpec((t_len, LANE), lambda h, b: (b, cb + j * hp + h))
    return pl.pallas_call(
        functools.partial(_na_kernel, ctx_len, tables[1], tables[2]),
        grid=(hp, batch),
        in_specs=[seq(0), seq(1), seq(2),
                  pl.BlockSpec((n_tab, 2, nq, nk), lambda h, b: (0, h, 0, 0))],
        out_specs=pl.BlockSpec((t_len, LANE), lambda h, b: (b, h)),
        out_shape=jax.ShapeDtypeStruct((n, NA_WIDTH), BF16),
        compiler_params=_cparams(("arbitrary", "arbitrary")),
        name="na_attention",
    )(proj, proj, proj, bias)


def _merge1_kernel(yr_ref, yn_ref, ys_ref, wr_ref, wn_ref, ws_ref, g0_ref, g1_ref, g2_ref,
                   b0_ref, b1_ref, b2_ref, o_ref):
    m = _sigmoid(g0_ref[...] + b0_ref[...]) * jnp.dot(yr_ref[...], wr_ref[...], preferred_element_type=F32)
    m = m + _sigmoid(g1_ref[...] + b1_ref[...]) * jnp.dot(yn_ref[...], wn_ref[...], preferred_element_type=F32)
    m = m + _sigmoid(g2_ref[...] + b2_ref[...]) * jnp.dot(ys_ref[...], ws_ref[...], preferred_element_type=F32)
    o_ref[...] = m.astype(o_ref.dtype)


def _merge1(y_rw, y_na, y_sc, lw, proj, gate_col0, d):
    n, c = y_rw.shape
    tm, tn = 512, MM_TN
    gb = gate_col0 // tn
    nd = d // tn
    ysp = pl.BlockSpec((tm, c), lambda i, j: (i, 0))
    wsp = pl.BlockSpec((c, tn), lambda i, j: (0, j))
    gsp = lambda br: pl.BlockSpec((tm, tn), lambda i, j: (i, gb + br * nd + j))
    bsp = lambda br: pl.BlockSpec((1, tn), lambda i, j: (0, br * nd + j))
    return pl.pallas_call(
        _merge1_kernel,
        grid=(n // tm, nd),
        in_specs=[ysp, ysp, ysp, wsp, wsp, wsp, gsp(0), gsp(1), gsp(2), bsp(0), bsp(1), bsp(2)],
        out_specs=pl.BlockSpec((tm, tn), lambda i, j: (i, j)),
        out_shape=jax.ShapeDtypeStruct((n, d), BF16),
        compiler_params=_cparams(("parallel", "arbitrary")),
        name="merge_branches",
    )(y_rw, y_na, y_sc, lw["w_out_rw"], lw["w_out_na"], lw["w_out_sc"], proj, proj, proj,
      lw["b_gate"], lw["b_gate"], lw["b_gate"])


def _merge2_kernel(alpha, m_ref, w_ref, x_ref, g1_ref, lg_ref, lb_ref, sh_ref, sc_ref, rw_ref, rb_ref,
                   x1_ref, h2_ref, lo_ref):
    mix = jnp.dot(m_ref[...], w_ref[...], preferred_element_type=F32)
    x1 = _ln(alpha * x_ref[...] + g1_ref[0] * mix) * lg_ref[...] + lb_ref[...]
    x1_ref[...] = x1
    h2 = _ln(x1) * (1.0 + sc_ref[0]) + sh_ref[0]
    h2_ref[...] = h2.astype(h2_ref.dtype)
    lo_ref[...] = jnp.dot(h2, rw_ref[...], precision=HI, preferred_element_type=F32) + rb_ref[...]


def _merge2(m, x, mods3, lw, nbb, alpha):
    n, d = x.shape
    full = lambda shape: pl.BlockSpec(shape, lambda i: (0,) * len(shape))
    row = pl.BlockSpec((ROW_BLK, d), lambda i: (i, 0))
    return pl.pallas_call(
        functools.partial(_merge2_kernel, alpha),
        grid=(n // ROW_BLK,),
        in_specs=[row, full((d, d)), row, _mod_spec(2, d, nbb), full((1, d)), full((1, d)),
                  _mod_spec(3, d, nbb), _mod_spec(4, d, nbb), full((d, LANE)), full((1, LANE))],
        out_specs=[row, row, pl.BlockSpec((ROW_BLK, LANE), lambda i: (i, 0))],
        out_shape=[jax.ShapeDtypeStruct((n, d), F32), jax.ShapeDtypeStruct((n, d), BF16),
                   jax.ShapeDtypeStruct((n, LANE), F32)],
        compiler_params=_cparams(("parallel",)),
        name="merge_out_ln",
    )(m, lw["w_merge"], x, mods3, lw["ln1_g"], lw["ln1_b"], mods3, mods3, lw["router_w"], lw["router_b"])


def _route_kernel(lo_ref, gate_ref, idx_ref, rank_ref, counts_ref, carry_ref):
    x = lo_ref[...]
    lanes = lax.broadcasted_iota(jnp.int32, x.shape, 1).astype(F32)
    x = jnp.where(lanes < N_EXPERTS, x, -jnp.inf)
    vals, idxs = [], []
    for _ in range(TOP_K):
        m = jnp.max(x, axis=-1, keepdims=True)
        sel = jnp.min(jnp.where(x == m, lanes, float(LANE)), axis=-1, keepdims=True)
        vals.append(m)
        idxs.append(sel)
        x = jnp.where(lanes == sel, -jnp.inf, x)
    es = [jnp.exp(v - vals[0]) for v in vals]
    tot = es[0] + es[1] + es[2] + es[3]
    @pl.when(pl.program_id(0) == 0)
    def _():
        carry_ref[...] = jnp.zeros_like(carry_ref)

    onehot = jnp.zeros(lo_ref.shape, F32)
    for j in range(TOP_K):
        onehot = onehot + jnp.where(lanes == idxs[j], 1.0, 0.0)
    nr = lo_ref.shape[0]
    below = (lax.broadcasted_iota(jnp.int32, (nr, nr), 1) < lax.broadcasted_iota(jnp.int32, (nr, nr), 0))
    prefix = jnp.dot(below.astype(BF16), onehot.astype(BF16), preferred_element_type=F32) + carry_ref[...]
    gate = jnp.zeros(lo_ref.shape, F32)
    idx = jnp.zeros(lo_ref.shape, F32)
    rank = jnp.zeros(lo_ref.shape, F32)
    for j in range(TOP_K):
        gate = jnp.where(lanes == j, es[j] / tot, gate)
        idx = jnp.where(lanes == j, idxs[j], idx)
        rank_j = jnp.sum(jnp.where(lanes == idxs[j], prefix, 0.0), axis=-1, keepdims=True)
        rank = jnp.where(lanes == j, rank_j, rank)
    gate_ref[...] = gate
    idx_ref[...] = idx.astype(jnp.int32)
    rank_ref[...] = rank.astype(jnp.int32)
    carry_ref[...] = carry_ref[...] + jnp.sum(onehot, axis=0, keepdims=True)
    counts_ref[...] = carry_ref[...].astype(jnp.int32)


def _route(logits):
    n = logits.shape[0]
    row = pl.BlockSpec((ROW_BLK, LANE), lambda i: (i, 0))
    return pl.pallas_call(
        _route_kernel,
        grid=(n // ROW_BLK,),
        in_specs=[row],
        out_specs=[row, row, row, pl.BlockSpec((1, LANE), lambda i: (0, 0))],
        out_shape=[jax.ShapeDtypeStruct((n, LANE), F32), jax.ShapeDtypeStruct((n, LANE), jnp.int32),
                   jax.ShapeDtypeStruct((n, LANE), jnp.int32), jax.ShapeDtypeStruct((1, LANE), jnp.int32)],
        scratch_shapes=[pltpu.VMEM((1, LANE), F32)],
        compiler_params=_cparams(("arbitrary",)),
        name="moe_route",
    )(logits)


def _expert_kernel(be_ref, x_ref, w1_ref, b1_ref, w2_ref, b2_ref, sel_ref, o_ref, w1b_ref, w2b_ref):
    i = pl.program_id(0)

    @pl.when(jnp.logical_or(i == 0, be_ref[i] != be_ref[jnp.maximum(i - 1, 0)]))
    def _():
        w1b_ref[...] = w1_ref[0].astype(BF16)
        w2b_ref[...] = w2_ref[0].astype(BF16)

    z = jnp.dot(x_ref[...], w1b_ref[...], preferred_element_type=F32) + b1_ref[0]
    z_glu = jnp.minimum(z, SWIGLU_LIMIT)
    z_lin = jnp.clip(pltpu.roll(z, z.shape[1] - 1, 1), -SWIGLU_LIMIT, SWIGLU_LIMIT)
    act = z_glu * _sigmoid(SWIGLU_ALPHA * z_glu) * (z_lin + 1.0)
    lanes = lax.broadcasted_iota(jnp.int32, act.shape, 1)
    act = jnp.where(lanes % 2 == 0, act, 0.0).astype(BF16)
    act = jnp.dot(act, sel_ref[...], preferred_element_type=F32).astype(BF16)
    o_ref[...] = jnp.dot(act, w2b_ref[...], preferred_element_type=F32) + b2_ref[0]


def _experts(xs, block_e, lw):
    n_slots, d = xs.shape
    n_blocks = n_slots // MOE_BLOCK
    f2 = 2 * D_EXPERT
    sel = jnp.asarray(np.arange(f2)[:, None] == 2 * np.arange(D_EXPERT)[None, :], BF16)
    grid_spec = pltpu.PrefetchScalarGridSpec(
        num_scalar_prefetch=1,
        grid=(n_blocks,),
        in_specs=[pl.BlockSpec((MOE_BLOCK, d), lambda i, be: (i, 0)),
                  pl.BlockSpec((1, d, f2), lambda i, be: (be[i], 0, 0)),
                  pl.BlockSpec((1, 1, f2), lambda i, be: (be[i], 0, 0)),
                  pl.BlockSpec((1, D_EXPERT, d), lambda i, be: (be[i], 0, 0)),
                  pl.BlockSpec((1, 1, d), lambda i, be: (be[i], 0, 0)),
                  pl.BlockSpec((f2, D_EXPERT), lambda i, be: (0, 0))],
        out_specs=pl.BlockSpec((MOE_BLOCK, d), lambda i, be: (i, 0)),
        scratch_shapes=[pltpu.VMEM((d, f2), BF16), pltpu.VMEM((D_EXPERT, d), BF16)],
    )
    return pl.pallas_call(
        _expert_kernel,
        grid_spec=grid_spec,
        out_shape=jax.ShapeDtypeStruct((n_slots, d), F32),
        compiler_params=_cparams(("arbitrary",)),
        name="moe_experts",
    )(block_e, xs, lw["moe_w1"], lw["moe_b1"], lw["moe_w2"], lw["moe_b2"], sel)


def _moe(h2, gate, idx, rank, counts, lw):
    n, d = h2.shape
    a = n * TOP_K
    counts = counts[0, :N_EXPERTS]
    padded = (counts + MOE_BLOCK - 1) // MOE_BLOCK * MOE_BLOCK
    pad_end = jnp.cumsum(padded)
    pad_start = pad_end - padded
    dest = pad_start[idx[:, :TOP_K]] + rank[:, :TOP_K]
    n_blocks = -(-a // MOE_BLOCK) + N_EXPERTS
    n_slots = n_blocks * MOE_BLOCK
    slot_tok = jnp.zeros(n_slots, jnp.int32).at[dest.reshape(a)].set(jnp.arange(a, dtype=jnp.int32) // TOP_K)
    block_start = jnp.arange(n_blocks, dtype=jnp.int32) * MOE_BLOCK
    block_e = jnp.minimum(jnp.sum(block_start[:, None] >= pad_end[None, :], axis=1), N_EXPERTS - 1).astype(jnp.int32)
    xs = h2[slot_tok]
    ys = _experts(xs, block_e + lw["moe_row0"], lw)
    return jnp.sum(ys[dest] * gate[:, :TOP_K, None], axis=1)


def _final_kernel(alpha, x_ref, y_ref, g2_ref, lg_ref, lb_ref, sh_ref, sc_ref, x2_ref, h_ref):
    x2 = _ln(alpha * x_ref[...] + g2_ref[0] * y_ref[...]) * lg_ref[...] + lb_ref[...]
    x2_ref[...] = x2
    h_ref[...] = (_ln(x2) * (1.0 + sc_ref[0]) + sh_ref[0]).astype(h_ref.dtype)


def _final(x1, y, mods3, mods3_next, lw, nbb, alpha):
    n, d = x1.shape
    full = lambda shape: pl.BlockSpec(shape, lambda i: (0,) * len(shape))
    row = pl.BlockSpec((ROW_BLK, d), lambda i: (i, 0))
    return pl.pallas_call(
        functools.partial(_final_kernel, alpha),
        grid=(n // ROW_BLK,),
        in_specs=[row, row, _mod_spec(5, d, nbb), full((1, d)), full((1, d)),
                  _mod_spec(0, d, nbb), _mod_spec(1, d, nbb)],
        out_specs=[row, row],
        out_shape=[jax.ShapeDtypeStruct((n, d), F32), jax.ShapeDtypeStruct((n, d), BF16)],
        compiler_params=_cparams(("parallel",)),
        name="residual_ln",
    )(x1, y, mods3, lw["ln2_g"], lw["ln2_b"], mods3_next, mods3_next)


def _rope_tables(ctx_len, seq):
    n_freq = HEAD_DIM // 4
    t = np.arange(seq)
    inv = np.power(ROPE_BASE, -np.arange(n_freq, dtype=np.float32) / n_freq).astype(np.float32)
    row = (t // GRID_W).astype(np.float32)
    col = (t % GRID_W).astype(np.float32)
    ang = jnp.asarray(np.concatenate([row[:, None] * inv, col[:, None] * inv], -1))
    cos = jnp.cos(ang)
    sin = jnp.sin(ang)
    cos_h = jnp.concatenate([cos, cos], -1)
    sin_h = jnp.concatenate([-sin, sin], -1)
    cos_t = jnp.concatenate([jnp.ones((ctx_len, HEAD_DIM), F32), cos_h], 0)
    sin_t = jnp.concatenate([jnp.zeros((ctx_len, HEAD_DIM), F32), sin_h], 0)
    return jnp.tile(cos_t, (1, LANE // HEAD_DIM)), jnp.tile(sin_t, (1, LANE // HEAD_DIM))


def _to_scan_layout(x, batch, t_len):
    return x.reshape(x.shape[:-2] + (batch, t_len, x.shape[-1]))


def _pad_dir_lora(w):
    z = jnp.zeros_like(w[0])
    return jnp.stack([jnp.concatenate([w[0], z], 0), jnp.concatenate([z, w[1]], 0)])


def _from_scan_layout(o, batch, t_len):
    return o.reshape(2, batch * t_len, RW_WIDTH)


def kernel(x, c, ctx, c_ctx, w_mod, b_mod, w_in, rw_mu, rw_w0, rw_w_lora, rw_a0, rw_a_lora, rw_g_lora, rw_k_k, rw_k_a, rw_r_k, rw_gn_w, rw_gn_b, w_out_rw, na_rpb, w_out_na, sc_conv, w_out_sc, b_gate, w_merge, ln1_g, ln1_b, router_w, router_b, moe_w1, moe_b1, moe_w2, moe_b2, ln2_g, ln2_b):
    batch, seq, d = x.shape
    ctx_len = ctx.shape[1]
    depth = w_in.shape[0]
    t_len = ctx_len + seq
    n = batch * t_len
    nbb = t_len // ROW_BLK
    assert batch <= 8 and ctx_len == ROW_BLK and seq % (NA_GROUP * GRID_W) == 0
    alpha = (2 * depth) ** 0.25

    cc = jnp.zeros((16, d), F32).at[:batch].set(c).at[8].set(c_ctx)
    mods = _modulation(cc, w_mod, b_mod)
    mods3 = [mods[l].reshape(16, 1, 6 * d) for l in range(depth)]

    cos_t, sin_t = _rope_tables(ctx_len, seq)
    seg = (np.arange(RW_WIDTH)[:, None] // HEAD_DIM == np.arange(LANE)[None, :]).astype(np.float32)
    consts = dict(cos=cos_t, sin=sin_t, seg=jnp.asarray(seg), seg_t=jnp.asarray(seg.T))
    tables = _na_tables(seq // GRID_W)

    o_na = RW_COLS_PAD
    o_sc = o_na + 3 * NA_WIDTH
    o_gate = o_sc + 3 * SC_WIDTH
    pad_cols = RW_COLS_PAD - RW_COLS

    xs = jnp.concatenate([ctx, x], axis=1).reshape(n, d)
    h = _lnmod(xs, mods3[0], nbb)
    for l in range(depth):
        w_in_p = jnp.concatenate([w_in[l, :, :RW_COLS], jnp.zeros((d, pad_cols), F32), w_in[l, :, RW_COLS:]],
                                 axis=1).astype(BF16)
        lw = dict(
            mu=jnp.pad(rw_mu[l], ((0, 0), (0, pad_cols))), w0=rw_w0[l], w_lora=_pad_dir_lora(rw_w_lora[l]),
            a0=rw_a0[l], a_lora=_pad_dir_lora(rw_a_lora[l]), g_lora=jnp.pad(rw_g_lora[l], ((0, GATE_LORA_PAD - GATE_LORA), (0, 0))),
            k_k=rw_k_k[l].reshape(1, -1), k_a=rw_k_a[l].reshape(1, -1), r_k=rw_r_k[l].reshape(1, -1),
            gn_w=rw_gn_w[l].reshape(1, -1), gn_b=rw_gn_b[l].reshape(1, -1),
            w_out_rw=w_out_rw[l].astype(BF16), w_out_na=w_out_na[l].astype(BF16),
            w_out_sc=w_out_sc[l].astype(BF16), b_gate=b_gate[l].reshape(1, -1),
            w_merge=w_merge[l].astype(BF16), ln1_g=ln1_g[l].reshape(1, -1), ln1_b=ln1_b[l].reshape(1, -1),
            router_w=jnp.pad(router_w[l], ((0, 0), (0, LANE - N_EXPERTS))),
            router_b=jnp.pad(router_b[l], (0, LANE - N_EXPERTS)).reshape(1, -1),
            moe_row0=l * N_EXPERTS,
            moe_w1=moe_w1.reshape((depth * N_EXPERTS,) + moe_w1.shape[2:]),
            moe_b1=moe_b1.reshape(depth * N_EXPERTS, 1, -1),
            moe_w2=moe_w2.reshape((depth * N_EXPERTS,) + moe_w2.shape[2:]),
            moe_b2=moe_b2.reshape(depth * N_EXPERTS, 1, -1),
            ln2_g=ln2_g[l].reshape(1, -1), ln2_b=ln2_b[l].reshape(1, -1),
        )
        proj = _matmul(h, w_in_p)
        r_s, kap, v, w, kd, b, bonus, g = _rwkv_prepare(proj, lw, consts, nbb)
        tl = functools.partial(_to_scan_layout, batch=batch, t_len=t_len)
        wkv = _wkv_scan(tl(r_s), tl(kap), tl(v), tl(w), tl(kd), tl(b), ctx_len)
        y_rw = _rwkv_out(_from_scan_layout(wkv, batch, t_len), bonus, g, lw, consts)
        y_na = _na_attention(proj, _na_bias(na_rpb[l], tables), tables, o_na, batch, t_len, ctx_len)
        y_sc = _short_conv(proj, jnp.pad(sc_conv[l], ((0, 5), (0, 0))), o_sc, nbb)
        m = _merge1(y_rw, y_na, y_sc, lw, proj, o_gate, d)
        x1, h2, logits = _merge2(m, xs, mods3[l], lw, nbb, alpha)
        gate, idx, rank, counts = _route(logits)
        y = _moe(h2, gate, idx, rank, counts, lw)
        xs, h = _final(x1, y, mods3[l], mods3[min(l + 1, depth - 1)], lw, nbb, alpha)
    return xs.reshape(batch, t_len, d)[:, ctx_len:]
```

```python
import functools
import math

import numpy as np
import jax
import jax.numpy as jnp
from jax import lax
from jax.experimental import pallas as pl
from jax.experimental.pallas import tpu as pltpu

GRID_W = 64
RW_HEADS = 16
HEAD_DIM = 64
RW_WIDTH = RW_HEADS * HEAD_DIM
DECAY_LORA = 64
ICLR_LORA = 64
GATE_LORA = 160
GATE_LORA_PAD = 256
GN_EPS = 64e-5
NA_HEADS = 16
NA_WIDTH = NA_HEADS * HEAD_DIM
NA_WIN_ROWS = 8
NA_WIN_COLS = 16
SC_WIDTH = 1024
RW_COLS = 3 * RW_WIDTH + 2 * DECAY_LORA + 2 * ICLR_LORA + GATE_LORA
RW_COLS_PAD = 3 * RW_WIDTH + 2 * DECAY_LORA + 2 * ICLR_LORA + GATE_LORA_PAD
N_EXPERTS = 32
TOP_K = 4
D_EXPERT = 512
SWIGLU_ALPHA = 1.702
SWIGLU_LIMIT = 7.0
MOE_BLOCK = 256
ROPE_BASE = 10000.0
LN_EPS = 1e-6
NEG_INF = -1e30

LANE = 128
ROW_BLK = 256
MM_TM = 1024
MM_TN = 512
SCAN_TT = 32
SCAN_VSPLIT = 2
SCAN_NACC = 4
SCAN_RELAYOUT_UNROLL = 4
NA_GROUP = 4
NA_KROWS = 12
VMEM_LIMIT = 56 * 1024 * 1024

HI = lax.Precision.HIGHEST
F32 = jnp.float32
BF16 = jnp.bfloat16


def _cparams(sem):
    return pltpu.CompilerParams(dimension_semantics=sem, vmem_limit_bytes=VMEM_LIMIT)


def _ln(x):
    mu = jnp.mean(x, axis=-1, keepdims=True)
    xc = x - mu
    var = jnp.mean(xc * xc, axis=-1, keepdims=True)
    return xc * lax.rsqrt(var + LN_EPS)


def _sigmoid(x):
    return 1.0 / (1.0 + jnp.exp(-x))


def _mod_kernel(c_ref, w_ref, b_ref, o_ref):
    c = c_ref[...]
    s = (c * _sigmoid(c)).astype(BF16)
    o_ref[0] = jnp.dot(s, w_ref[0].astype(BF16), preferred_element_type=F32) + b_ref[0]


def _modulation(cc, w_mod, b_mod):
    depth, d, n = w_mod.shape
    tn = 1024
    return pl.pallas_call(
        _mod_kernel,
        grid=(depth, n // tn),
        in_specs=[pl.BlockSpec((16, d), lambda l, j: (0, 0)),
                  pl.BlockSpec((1, d, tn), lambda l, j: (l, 0, j)),
                  pl.BlockSpec((1, 1, tn), lambda l, j: (l, 0, j))],
        out_specs=pl.BlockSpec((1, 16, tn), lambda l, j: (l, 0, j)),
        out_shape=jax.ShapeDtypeStruct((depth, 16, n), F32),
        compiler_params=_cparams(("arbitrary", "arbitrary")),
        name="modulation",
    )(cc, w_mod, b_mod.reshape(depth, 1, n))


def _mod_spec(chunk, d, nblk_per_batch):
    def imap(i):
        row = jnp.where(i % nblk_per_batch == 0, 8, i // nblk_per_batch)
        return (row, 0, chunk)
    return pl.BlockSpec((1, 1, d), imap)


def _lnmod_kernel(x_ref, sh_ref, sc_ref, o_ref):
    o_ref[...] = (_ln(x_ref[...]) * (1.0 + sc_ref[0]) + sh_ref[0]).astype(o_ref.dtype)


def _lnmod(x, mods3, nbb):
    n, d = x.shape
    return pl.pallas_call(
        _lnmod_kernel,
        grid=(n // ROW_BLK,),
        in_specs=[pl.BlockSpec((ROW_BLK, d), lambda i: (i, 0)),
                  _mod_spec(0, d, nbb), _mod_spec(1, d, nbb)],
        out_specs=pl.BlockSpec((ROW_BLK, d), lambda i: (i, 0)),
        out_shape=jax.ShapeDtypeStruct((n, d), BF16),
        compiler_params=_cparams(("parallel",)),
        name="ln_mod",
    )(x, mods3, mods3)


def _mm_kernel(a_ref, b_ref, o_ref):
    o_ref[...] = jnp.dot(a_ref[...], b_ref[...], preferred_element_type=F32).astype(o_ref.dtype)


def _matmul(a, b, out_dtype=F32):
    m, k = a.shape
    _, n = b.shape
    tm = MM_TM if m % MM_TM == 0 else ROW_BLK
    return pl.pallas_call(
        _mm_kernel,
        grid=(m // tm, n // MM_TN),
        in_specs=[pl.BlockSpec((tm, k), lambda i, j: (i, 0)),
                  pl.BlockSpec((k, MM_TN), lambda i, j: (0, j))],
        out_specs=pl.BlockSpec((tm, MM_TN), lambda i, j: (i, j)),
        out_shape=jax.ShapeDtypeStruct((m, n), out_dtype),
        compiler_params=_cparams(("parallel", "arbitrary")),
        name="in_proj",
    )(a, b)


def _halo_specs(width, col_blk, nblk):
    per = ROW_BLK // 8

    def prev_map(i):
        return (jnp.maximum(i * per - 1, 0), col_blk)

    def next_map(i):
        return (jnp.minimum((i + 1) * per, nblk * per - 1), col_blk)

    return (pl.BlockSpec((8, width), prev_map), pl.BlockSpec((8, width), next_map))


def _shifted(p, prev8, next8, i, nbb):
    pos = i % nbb
    has_prev = pos >= 2
    has_next = jnp.logical_and(pos >= 1, pos <= nbb - 2)
    rows = lax.broadcasted_iota(jnp.int32, p.shape, 0)
    prow = jnp.where(has_prev, prev8[7:8, :], 0.0)
    nrow = jnp.where(has_next, next8[0:1, :], 0.0)
    p_prev = jnp.where(rows == 0, prow, pltpu.roll(p, 1, 0))
    p_next = jnp.where(rows == p.shape[0] - 1, nrow, pltpu.roll(p, p.shape[0] - 1, 0))
    return p_prev, p_next


def _seg_sum(x, e_ref, et_ref):
    s = jnp.dot(x, e_ref[...], precision=HI, preferred_element_type=F32)
    return jnp.dot(s, et_ref[...], precision=HI, preferred_element_type=F32)


def _swap_halves(x):
    lanes = lax.broadcasted_iota(jnp.int32, x.shape, 1)
    first = (lanes % HEAD_DIM) < (HEAD_DIM // 2)
    n = x.shape[1]
    return jnp.where(first, pltpu.roll(x, n - HEAD_DIM // 2, 1), pltpu.roll(x, HEAD_DIM // 2, 1))


def _prep_kernel(nbb, p_ref, pp_ref, pn_ref, mu_ref, w0_ref, wl_ref, a0_ref, al_ref, gl_ref,
                 kk_ref, ka_ref, rk_ref, cos_ref, sin_ref, e_ref, et_ref,
                 r_o, kap_o, v_o, w_o, kd_o, b_o, bonus_o, g_o):
    i = pl.program_id(0)
    c = RW_WIDTH
    p = p_ref[...]
    p_prev, p_next = _shifted(p, pp_ref[...], pn_ref[...], i, nbb)
    pm = p + mu_ref[0:1, :] * (p_prev - p) + mu_ref[1:2, :] * (p_next - p)
    r, k, v = pm[:, :c], pm[:, c:2 * c], pm[:, 2 * c:3 * c]
    o = 3 * c
    wl = jnp.tanh(pm[:, o:o + 2 * DECAY_LORA])
    o += 2 * DECAY_LORA
    al = pm[:, o:o + 2 * ICLR_LORA]
    o += 2 * ICLR_LORA
    gl = _sigmoid(pm[:, o:o + GATE_LORA_PAD])
    g_o[...] = jnp.dot(gl, gl_ref[...], precision=HI, preferred_element_type=F32)
    kk = k * kk_ref[...]
    ss = _seg_sum(kk * kk, e_ref, et_ref)
    kk = kk / jnp.maximum(jnp.sqrt(ss), 1e-12)
    cos = jnp.concatenate([cos_ref[...]] * (c // LANE), axis=1)
    sin = jnp.concatenate([sin_ref[...]] * (c // LANE), axis=1)

    def rope(x):
        return x * cos + _swap_halves(x) * sin

    kap = rope(kk)
    r_o[...] = rope(r)
    kap_o[...] = kap
    v_o[...] = v
    k_sum = jnp.zeros_like(k)
    for d in range(2):
        w_raw = w0_ref[d:d + 1, :] + jnp.dot(wl, wl_ref[d], precision=HI, preferred_element_type=F32)
        w_o[d] = jnp.exp(-math.exp(-0.5) * _sigmoid(w_raw))
        a = _sigmoid(a0_ref[d:d + 1, :] + jnp.dot(al, al_ref[d], precision=HI, preferred_element_type=F32))
        kd = k * (1.0 + (a - 1.0) * ka_ref[...])
        k_sum = k_sum + kd
        kd_o[d] = rope(kd)
        b_o[d] = kap * a
    bonus_o[...] = _seg_sum(r * k_sum * rk_ref[...], e_ref, et_ref) * v


def _rwkv_prepare(proj, lw, consts, nbb):
    n = proj.shape[0]
    c = RW_WIDTH
    nblk = n // ROW_BLK
    prev_spec, next_spec = _halo_specs(RW_COLS_PAD, 0, nblk)
    full = lambda shape: pl.BlockSpec(shape, lambda i: (0,) * len(shape))
    row_c = pl.BlockSpec((ROW_BLK, c), lambda i: (i, 0))
    row_2c = pl.BlockSpec((2, ROW_BLK, c), lambda i: (0, i, 0))
    tab = pl.BlockSpec((ROW_BLK, LANE), lambda i: (i % nbb, 0))
    outs = pl.pallas_call(
        functools.partial(_prep_kernel, nbb),
        grid=(nblk,),
        in_specs=[pl.BlockSpec((ROW_BLK, RW_COLS_PAD), lambda i: (i, 0)), prev_spec, next_spec,
                  full((2, RW_COLS_PAD)), full((2, c)), full((2, 2 * DECAY_LORA, c)), full((2, c)),
                  full((2, 2 * ICLR_LORA, c)), full((GATE_LORA_PAD, c)), full((1, c)), full((1, c)),
                  full((1, c)), tab, tab, full((c, LANE)), full((LANE, c))],
        out_specs=[row_c, row_c, row_c, row_2c, row_2c, row_2c, row_c, row_c],
        out_shape=[jax.ShapeDtypeStruct((n, c), F32)] * 3 + [jax.ShapeDtypeStruct((2, n, c), F32)] * 3
        + [jax.ShapeDtypeStruct((n, c), F32)] * 2,
        compiler_params=_cparams(("parallel",)),
        name="rwkv_prepare",
    )(proj, proj, proj, lw["mu"], lw["w0"], lw["w_lora"], lw["a0"], lw["a_lora"], lw["g_lora"],
      lw["k_k"], lw["k_a"], lw["r_k"], consts["cos"], consts["sin"], consts["seg"], consts["seg_t"])
    return outs


def _pair_to_lanes(xa, xb):
    nl = xa.shape[0]
    res = jnp.concatenate([xa, xb], axis=0).T
    top, bot = res[:HEAD_DIM], res[HEAD_DIM:]
    first = lax.broadcasted_iota(jnp.int32, top.shape, 1) < nl
    tile_a = jnp.where(first, top, pltpu.roll(bot, nl, 1))
    tile_b = jnp.where(first, pltpu.roll(top, nl, 1), bot)
    return tile_a, tile_b


def _scan_kernel(r_ref, kap_ref, v_ref, w_ref, kd_ref, b_ref, o_ref, s_ref, vec_ref, out_ref):
    d = pl.program_id(0)
    nb, tt = r_ref.shape[0], r_ref.shape[1]
    n_pair = r_ref.shape[2] // (2 * HEAD_DIM)
    nl = nb * n_pair
    vh = HEAD_DIM // SCAN_VSPLIT

    @pl.when(pl.program_id(1) == 0)
    def _():
        s_ref[...] = jnp.zeros_like(s_ref)

    def to_lanes(t, carry):
        slab = lambda x: jnp.concatenate([x[:, p * 2 * HEAD_DIM:(p + 1) * 2 * HEAD_DIM] for p in range(n_pair)], axis=0)
        pairs = ((kap_ref[:, t, :], r_ref[:, t, :]), (w_ref[0, :, t, :], kd_ref[0, :, t, :]),
                 (b_ref[0, :, t, :], v_ref[:, t, :]))
        for j, (xa, xb) in enumerate(pairs):
            tile_a, tile_b = _pair_to_lanes(slab(xa), slab(xb))
            vec_ref[t, 2 * j] = tile_a
            vec_ref[t, 2 * j + 1] = tile_b
        return carry

    lax.fori_loop(0, tt, to_lanes, 0, unroll=SCAN_RELAYOUT_UNROLL)

    def step(i, carry):
        t = jnp.where(d == 0, i, tt - 1 - i)
        for h in range(SCAN_VSPLIT):
            rows = slice(h * vh, (h + 1) * vh)
            acc = [None] * SCAN_NACC
            for k in range(HEAD_DIM):
                term = s_ref[k, rows] * vec_ref[t, 0, k:k + 1, :]
                acc[k % SCAN_NACC] = term if acc[k % SCAN_NACC] is None else acc[k % SCAN_NACC] + term
            skk = functools.reduce(lambda a, b: a + b, acc)
            v = vec_ref[t, 5, rows]
            acc = [None] * SCAN_NACC
            for k in range(HEAD_DIM):
                s_new = (s_ref[k, rows] * vec_ref[t, 2, k:k + 1, :] - skk * vec_ref[t, 4, k:k + 1, :]
                         + v * vec_ref[t, 3, k:k + 1, :])
                s_ref[k, rows] = s_new
                term = s_new * vec_ref[t, 1, k:k + 1, :]
                acc[k % SCAN_NACC] = term if acc[k % SCAN_NACC] is None else acc[k % SCAN_NACC] + term
            out_ref[t, rows] = functools.reduce(lambda a, b: a + b, acc)
        return carry

    lax.fori_loop(0, tt, step, 0)

    def to_tokens(t, carry):
        out = out_ref[t]
        back = jnp.concatenate([out, pltpu.roll(out, nl, 1)], axis=0).T[:nl]
        o_ref[0, :, t, :] = jnp.concatenate([back[p * nb:(p + 1) * nb] for p in range(n_pair)], axis=1)
        return carry

    lax.fori_loop(0, tt, to_tokens, 0, unroll=SCAN_RELAYOUT_UNROLL)


def _scan_block(n_ctx_blk, nb):
    def blk(d, j):
        rev = jnp.where(j < n_ctx_blk, n_ctx_blk - 1 - j, nb - 1 - (j - n_ctx_blk))
        return jnp.where(d == 0, j, rev)
    return blk


def _wkv_scan(r, kap, v, w, kd, b, ctx_len):
    batch, t, width = r.shape
    lanes = batch * width // HEAD_DIM
    nb = t // SCAN_TT
    blk = _scan_block(ctx_len // SCAN_TT, nb)
    shared = pl.BlockSpec((batch, SCAN_TT, width), lambda d, j: (0, blk(d, j), 0))
    per_dir = pl.BlockSpec((1, batch, SCAN_TT, width), lambda d, j: (d, 0, blk(d, j), 0))
    return pl.pallas_call(
        _scan_kernel,
        grid=(2, nb),
        in_specs=[shared, shared, shared, per_dir, per_dir, per_dir],
        out_specs=per_dir,
        out_shape=jax.ShapeDtypeStruct((2, batch, t, width), F32),
        scratch_shapes=[pltpu.VMEM((HEAD_DIM, HEAD_DIM, lanes), F32),
                        pltpu.VMEM((SCAN_TT, 6, HEAD_DIM, lanes), F32),
                        pltpu.VMEM((SCAN_TT, HEAD_DIM, lanes), F32)],
        compiler_params=_cparams(("arbitrary", "arbitrary")),
        name="wkv_scan",
    )(r, kap, v, w, kd, b)


def _rwkv_out_kernel(wkv_ref, bonus_ref, g_ref, gw_ref, gb_ref, e_ref, et_ref, o_ref):
    x = wkv_ref[0] + wkv_ref[1]
    inv_n = 1.0 / HEAD_DIM
    mu = _seg_sum(x, e_ref, et_ref) * inv_n
    xc = x - mu
    var = _seg_sum(xc * xc, e_ref, et_ref) * inv_n
    y = xc * lax.rsqrt(var + GN_EPS) * gw_ref[...] + gb_ref[...]
    o_ref[...] = ((y + bonus_ref[...]) * g_ref[...]).astype(o_ref.dtype)


def _rwkv_out(wkv, bonus, g, lw, consts):
    n, c = bonus.shape
    full = lambda shape: pl.BlockSpec(shape, lambda i: (0,) * len(shape))
    row_c = pl.BlockSpec((ROW_BLK, c), lambda i: (i, 0))
    return pl.pallas_call(
        _rwkv_out_kernel,
        grid=(n // ROW_BLK,),
        in_specs=[pl.BlockSpec((2, ROW_BLK, c), lambda i: (0, i, 0)), row_c, row_c,
                  full((1, c)), full((1, c)), full((c, LANE)), full((LANE, c))],
        out_specs=row_c,
        out_shape=jax.ShapeDtypeStruct((n, c), BF16),
        compiler_params=_cparams(("parallel",)),
        name="rwkv_out",
    )(wkv, bonus, g, lw["gn_w"], lw["gn_b"], consts["seg"], consts["seg_t"])


def _conv_kernel(nbb, bg_ref, cg_ref, x_ref, cgp_ref, xp_ref, cgn_ref, xn_ref, w_ref, o_ref):
    i = pl.program_id(0)
    u = cg_ref[...] * x_ref[...]
    u_prev, u_next = _shifted(u, cgp_ref[...] * xp_ref[...], cgn_ref[...] * xn_ref[...], i, nbb)
    y = w_ref[0:1, :] * u_prev + w_ref[1:2, :] * u + w_ref[2:3, :] * u_next
    o_ref[...] = (bg_ref[...] * y).astype(o_ref.dtype)


def _short_conv(proj, conv_w, col0, nbb):
    n = proj.shape[0]
    cw = MM_TN
    nc = SC_WIDTH // cw
    assert col0 % cw == 0
    cb = col0 // cw
    nblk = n // ROW_BLK
    blk = lambda s: pl.BlockSpec((ROW_BLK, cw), lambda i, j: (i, cb + s * nc + j))

    def halos(s):
        prev_spec, next_spec = _halo_specs(cw, 0, nblk)
        pm, nm = prev_spec.index_map, next_spec.index_map
        return (pl.BlockSpec((8, cw), lambda i, j: (pm(i)[0], cb + s * nc + j)),
                pl.BlockSpec((8, cw), lambda i, j: (nm(i)[0], cb + s * nc + j)))

    cg_prev, cg_next = halos(1)
    x_prev, x_next = halos(2)
    return pl.pallas_call(
        functools.partial(_conv_kernel, nbb),
        grid=(nblk, nc),
        in_specs=[blk(0), blk(1), blk(2), cg_prev, x_prev, cg_next, x_next,
                  pl.BlockSpec((8, cw), lambda i, j: (0, j))],
        out_specs=pl.BlockSpec((ROW_BLK, cw), lambda i, j: (i, j)),
        out_shape=jax.ShapeDtypeStruct((n, SC_WIDTH), BF16),
        compiler_params=_cparams(("parallel", "arbitrary")),
        name="short_conv",
    )(proj, proj, proj, proj, proj, proj, proj, conv_w)


def _na_tables(rows):
    kr = min(NA_WIN_ROWS, rows)
    n_groups = rows // NA_GROUP
    krows = min(NA_KROWS, rows)
    n_dr = 2 * NA_WIN_ROWS - 1
    dr = np.full((n_groups, NA_GROUP, krows), n_dr, np.int32)
    bases = []
    for g in range(n_groups):
        r0 = g * NA_GROUP
        base = int(np.clip(np.clip(r0 - kr // 2, 0, rows - kr), 0, rows - krows))
        bases.append(base)
        for rl in range(NA_GROUP):
            r = r0 + rl
            rs = int(np.clip(r - kr // 2, 0, rows - kr))
            for j in range(krows):
                if rs <= base + j < rs + kr:
                    dr[g, rl, j] = base + j - r + NA_WIN_ROWS - 1
            assert (dr[g, rl] < n_dr).sum() == kr
    uniq, table_of = [], []
    for g in range(n_groups):
        for u, gu in enumerate(uniq):
            if np.array_equal(dr[g], dr[gu]):
                table_of.append(u)
                break
        else:
            table_of.append(len(uniq))
            uniq.append(g)
    return dr[uniq], tuple(bases), tuple(table_of)


def _na_kernel(ctx_len, dr, bases, table_of, q_ref, k_ref, v_ref, bias_ref, o_ref):
    scale = HEAD_DIM ** -0.5
    nq = NA_GROUP * GRID_W
    krows = dr.shape[2]
    nk = krows * GRID_W
    first = lax.broadcasted_iota(jnp.int32, (GRID_W, 2 * GRID_W), 1) < GRID_W

    def bias_of(table, h):
        rows = []
        for rl in range(NA_GROUP):
            tiles = [jnp.where(first, bias_ref[h, int(dr[table, rl, j])], bias_ref[h, int(dr[table, rl, j + 1])])
                     for j in range(0, krows, 2)]
            rows.append(jnp.concatenate(tiles, axis=1))
        return jnp.concatenate(rows, axis=0)

    outs_heads = []
    for h in range(2):
        sl = slice(h * HEAD_DIM, (h + 1) * HEAD_DIM)
        q = (q_ref[:, sl] * scale).astype(BF16)
        k = k_ref[:, sl].astype(BF16)
        v = v_ref[:, sl].astype(BF16)
        kc, vc = k[:ctx_len], v[:ctx_len]
        dn = (((1,), (1,)), ((), ()))
        s = lax.dot_general(q[:ctx_len], kc, dn, preferred_element_type=F32)
        s = s - jnp.max(s, axis=-1, keepdims=True)
        e = jnp.exp(s)
        p = (e / jnp.sum(e, axis=-1, keepdims=True)).astype(BF16)
        pieces = [jnp.dot(p, vc, preferred_element_type=F32)]
        biases = [bias_of(t, h) for t in range(dr.shape[0])]
        for g, base in enumerate(bases):
            q0 = ctx_len + g * nq
            k0 = ctx_len + base * GRID_W
            qg = q[q0:q0 + nq]
            s_win = (lax.dot_general(qg, k[k0:k0 + nk], dn, preferred_element_type=F32)
                     + biases[table_of[g]])
            s_ctx = lax.dot_general(qg, kc, dn, preferred_element_type=F32)
            m = jnp.maximum(jnp.max(s_win, axis=-1, keepdims=True), jnp.max(s_ctx, axis=-1, keepdims=True))
            e_win = jnp.exp(s_win - m)
            e_ctx = jnp.exp(s_ctx - m)
            inv = 1.0 / (jnp.sum(e_win, axis=-1, keepdims=True) + jnp.sum(e_ctx, axis=-1, keepdims=True))
            acc = jnp.dot((e_win * inv).astype(BF16), v[k0:k0 + nk], preferred_element_type=F32)
            acc = acc + jnp.dot((e_ctx * inv).astype(BF16), vc, preferred_element_type=F32)
            pieces.append(acc)
        outs_heads.append(jnp.concatenate(pieces, axis=0))
    o_ref[...] = jnp.concatenate(outs_heads, axis=1).astype(o_ref.dtype)


def _na_bias(rpb):
    qc = np.arange(GRID_W)[:, None]
    kc = np.arange(GRID_W)[None, :]
    wstart = np.clip(qc - NA_WIN_COLS // 2, 0, GRID_W - NA_WIN_COLS)
    col_ok = (kc >= wstart) & (kc < wstart + NA_WIN_COLS)
    dc = np.clip(kc - qc + NA_WIN_COLS - 1, 0, 2 * NA_WIN_COLS - 2)
    blocks = jnp.where(col_ok, rpb[:, :, dc], NEG_INF)
    blocks = jnp.concatenate([blocks, jnp.full_like(blocks[:, :1], NEG_INF)], axis=1)
    return jnp.concatenate([blocks, blocks], axis=-1)


def _na_attention(proj, bias, tables, col0, batch, t_len, ctx_len):
    n = proj.shape[0]
    cb = col0 // LANE
    hp = NA_HEADS // 2
    seq = lambda j: pl.BlockSpec((t_len, LANE), lambda h, b: (b, cb + j * hp + h))
    return pl.pallas_call(
        functools.partial(_na_kernel, ctx_len, *tables),
        grid=(hp, batch),
        in_specs=[seq(0), seq(1), seq(2),
                  pl.BlockSpec((2,) + bias.shape[1:], lambda h, b: (h, 0, 0, 0))],
        out_specs=pl.BlockSpec((t_len, LANE), lambda h, b: (b, h)),
        out_shape=jax.ShapeDtypeStruct((n, NA_WIDTH), BF16),
        compiler_params=_cparams(("arbitrary", "arbitrary")),
        name="na_attention",
    )(proj, proj, proj, bias)


def _merge1_kernel(yr_ref, yn_ref, ys_ref, wr_ref, wn_ref, ws_ref, g0_ref, g1_ref, g2_ref,
                   b0_ref, b1_ref, b2_ref, o_ref):
    m = _sigmoid(g0_ref[...] + b0_ref[...]) * jnp.dot(yr_ref[...], wr_ref[...], preferred_element_type=F32)
    m = m + _sigmoid(g1_ref[...] + b1_ref[...]) * jnp.dot(yn_ref[...], wn_ref[...], preferred_element_type=F32)
    m = m + _sigmoid(g2_ref[...] + b2_ref[...]) * jnp.dot(ys_ref[...], ws_ref[...], preferred_element_type=F32)
    o_ref[...] = m.astype(o_ref.dtype)


def _merge1(y_rw, y_na, y_sc, lw, proj, gate_col0, d):
    n, c = y_rw.shape
    tm, tn = 512, MM_TN
    gb = gate_col0 // tn
    nd = d // tn
    ysp = pl.BlockSpec((tm, c), lambda i, j: (i, 0))
    wsp = pl.BlockSpec((c, tn), lambda i, j: (0, j))
    gsp = lambda br: pl.BlockSpec((tm, tn), lambda i, j: (i, gb + br * nd + j))
    bsp = lambda br: pl.BlockSpec((1, tn), lambda i, j: (0, br * nd + j))
    return pl.pallas_call(
        _merge1_kernel,
        grid=(n // tm, nd),
        in_specs=[ysp, ysp, ysp, wsp, wsp, wsp, gsp(0), gsp(1), gsp(2), bsp(0), bsp(1), bsp(2)],
        out_specs=pl.BlockSpec((tm, tn), lambda i, j: (i, j)),
        out_shape=jax.ShapeDtypeStruct((n, d), BF16),
        compiler_params=_cparams(("parallel", "arbitrary")),
        name="merge_branches",
    )(y_rw, y_na, y_sc, lw["w_out_rw"], lw["w_out_na"], lw["w_out_sc"], proj, proj, proj,
      lw["b_gate"], lw["b_gate"], lw["b_gate"])


def _merge2_kernel(alpha, m_ref, w_ref, x_ref, g1_ref, lg_ref, lb_ref, sh_ref, sc_ref, rw_ref, rb_ref,
                   x1_ref, h2_ref, lo_ref):
    mix = jnp.dot(m_ref[...], w_ref[...], preferred_element_type=F32)
    x1 = _ln(alpha * x_ref[...] + g1_ref[0] * mix) * lg_ref[...] + lb_ref[...]
    x1_ref[...] = x1
    h2 = _ln(x1) * (1.0 + sc_ref[0]) + sh_ref[0]
    h2_ref[...] = h2.astype(h2_ref.dtype)
    lo_ref[...] = jnp.dot(h2, rw_ref[...], precision=HI, preferred_element_type=F32) + rb_ref[...]


def _merge2(m, x, mods3, lw, nbb, alpha):
    n, d = x.shape
    full = lambda shape: pl.BlockSpec(shape, lambda i: (0,) * len(shape))
    row = pl.BlockSpec((ROW_BLK, d), lambda i: (i, 0))
    return pl.pallas_call(
        functools.partial(_merge2_kernel, alpha),
        grid=(n // ROW_BLK,),
        in_specs=[row, full((d, d)), row, _mod_spec(2, d, nbb), full((1, d)), full((1, d)),
                  _mod_spec(3, d, nbb), _mod_spec(4, d, nbb), full((d, LANE)), full((1, LANE))],
        out_specs=[row, row, pl.BlockSpec((ROW_BLK, LANE), lambda i: (i, 0))],
        out_shape=[jax.ShapeDtypeStruct((n, d), F32), jax.ShapeDtypeStruct((n, d), BF16),
                   jax.ShapeDtypeStruct((n, LANE), F32)],
        compiler_params=_cparams(("parallel",)),
        name="merge_out_ln",
    )(m, lw["w_merge"], x, mods3, lw["ln1_g"], lw["ln1_b"], mods3, mods3, lw["router_w"], lw["router_b"])


def _route_kernel(lo_ref, gate_ref, idx_ref, rank_ref, counts_ref, carry_ref):
    x = lo_ref[...]
    lanes = lax.broadcasted_iota(jnp.int32, x.shape, 1).astype(F32)
    x = jnp.where(lanes < N_EXPERTS, x, -jnp.inf)
    vals, idxs = [], []
    for _ in range(TOP_K):
        m = jnp.max(x, axis=-1, keepdims=True)
        sel = jnp.min(jnp.where(x == m, lanes, float(LANE)), axis=-1, keepdims=True)
        vals.append(m)
        idxs.append(sel)
        x = jnp.where(lanes == sel, -jnp.inf, x)
    es = [jnp.exp(v - vals[0]) for v in vals]
    tot = es[0] + es[1] + es[2] + es[3]
    @pl.when(pl.program_id(0) == 0)
    def _():
        carry_ref[...] = jnp.zeros_like(carry_ref)

    onehot = jnp.zeros(lo_ref.shape, F32)
    for j in range(TOP_K):
        onehot = onehot + jnp.where(lanes == idxs[j], 1.0, 0.0)
    nr = lo_ref.shape[0]
    below = (lax.broadcasted_iota(jnp.int32, (nr, nr), 1) < lax.broadcasted_iota(jnp.int32, (nr, nr), 0))
    prefix = jnp.dot(below.astype(BF16), onehot.astype(BF16), preferred_element_type=F32) + carry_ref[...]
    gate = jnp.zeros(lo_ref.shape, F32)
    idx = jnp.zeros(lo_ref.shape, F32)
    rank = jnp.zeros(lo_ref.shape, F32)
    for j in range(TOP_K):
        gate = jnp.where(lanes == j, es[j] / tot, gate)
        idx = jnp.where(lanes == j, idxs[j], idx)
        rank_j = jnp.sum(jnp.where(lanes == idxs[j], prefix, 0.0), axis=-1, keepdims=True)
        rank = jnp.where(lanes == j, rank_j, rank)
    gate_ref[...] = gate
    idx_ref[...] = idx.astype(jnp.int32)
    rank_ref[...] = rank.astype(jnp.int32)
    carry_ref[...] = carry_ref[...] + jnp.sum(onehot, axis=0, keepdims=True)
    counts_ref[...] = carry_ref[...].astype(jnp.int32)


def _route(logits):
    n = logits.shape[0]
    row = pl.BlockSpec((ROW_BLK, LANE), lambda i: (i, 0))
    return pl.pallas_call(
        _route_kernel,
        grid=(n // ROW_BLK,),
        in_specs=[row],
        out_specs=[row, row, row, pl.BlockSpec((1, LANE), lambda i: (0, 0))],
        out_shape=[jax.ShapeDtypeStruct((n, LANE), F32), jax.ShapeDtypeStruct((n, LANE), jnp.int32),
                   jax.ShapeDtypeStruct((n, LANE), jnp.int32), jax.ShapeDtypeStruct((1, LANE), jnp.int32)],
        scratch_shapes=[pltpu.VMEM((1, LANE), F32)],
        compiler_params=_cparams(("arbitrary",)),
        name="moe_route",
    )(logits)


def _expert_kernel(be_ref, x_ref, w1_ref, b1_ref, w2_ref, b2_ref, sel_ref, o_ref, w1b_ref, w2b_ref):
    i = pl.program_id(0)

    @pl.when(jnp.logical_or(i == 0, be_ref[i] != be_ref[jnp.maximum(i - 1, 0)]))
    def _():
        w1b_ref[...] = w1_ref[0].astype(BF16)
        w2b_ref[...] = w2_ref[0].astype(BF16)

    z = jnp.dot(x_ref[...], w1b_ref[...], preferred_element_type=F32) + b1_ref[0]
    z_glu = jnp.minimum(z, SWIGLU_LIMIT)
    z_lin = jnp.clip(pltpu.roll(z, z.shape[1] - 1, 1), -SWIGLU_LIMIT, SWIGLU_LIMIT)
    act = z_glu * _sigmoid(SWIGLU_ALPHA * z_glu) * (z_lin + 1.0)
    lanes = lax.broadcasted_iota(jnp.int32, act.shape, 1)
    act = jnp.where(lanes % 2 == 0, act, 0.0).astype(BF16)
    act = jnp.dot(act, sel_ref[...], preferred_element_type=F32).astype(BF16)
    o_ref[...] = jnp.dot(act, w2b_ref[...], preferred_element_type=F32) + b2_ref[0]


def _experts(xs, block_e, lw):
    n_slots, d = xs.shape
    n_blocks = n_slots // MOE_BLOCK
    f2 = 2 * D_EXPERT
    sel = jnp.asarray(np.arange(f2)[:, None] == 2 * np.arange(D_EXPERT)[None, :], BF16)
    grid_spec = pltpu.PrefetchScalarGridSpec(
        num_scalar_prefetch=1,
        grid=(n_blocks,),
        in_specs=[pl.BlockSpec((MOE_BLOCK, d), lambda i, be: (i, 0)),
                  pl.BlockSpec((1, d, f2), lambda i, be: (be[i], 0, 0)),
                  pl.BlockSpec((1, 1, f2), lambda i, be: (be[i], 0, 0)),
                  pl.BlockSpec((1, D_EXPERT, d), lambda i, be: (be[i], 0, 0)),
                  pl.BlockSpec((1, 1, d), lambda i, be: (be[i], 0, 0)),
                  pl.BlockSpec((f2, D_EXPERT), lambda i, be: (0, 0))],
        out_specs=pl.BlockSpec((MOE_BLOCK, d), lambda i, be: (i, 0)),
        scratch_shapes=[pltpu.VMEM((d, f2), BF16), pltpu.VMEM((D_EXPERT, d), BF16)],
    )
    return pl.pallas_call(
        _expert_kernel,
        grid_spec=grid_spec,
        out_shape=jax.ShapeDtypeStruct((n_slots, d), F32),
        compiler_params=_cparams(("arbitrary",)),
        name="moe_experts",
    )(block_e, xs, lw["moe_w1"], lw["moe_b1"], lw["moe_w2"], lw["moe_b2"], sel)


def _moe(h2, idx, rank, counts, lw):
    n, d = h2.shape
    a = n * TOP_K
    counts = counts[0, :N_EXPERTS]
    padded = (counts + MOE_BLOCK - 1) // MOE_BLOCK * MOE_BLOCK
    pad_end = jnp.cumsum(padded)
    pad_start = pad_end - padded
    dest = pad_start[idx[:, :TOP_K]] + rank[:, :TOP_K]
    n_blocks = -(-a // MOE_BLOCK) + N_EXPERTS
    n_slots = n_blocks * MOE_BLOCK
    slot_tok = jnp.zeros(n_slots, jnp.int32).at[dest.reshape(a)].set(jnp.arange(a, dtype=jnp.int32) // TOP_K)
    block_start = jnp.arange(n_blocks, dtype=jnp.int32) * MOE_BLOCK
    block_e = jnp.minimum(jnp.sum(block_start[:, None] >= pad_end[None, :], axis=1), N_EXPERTS - 1).astype(jnp.int32)
    xs = h2[slot_tok]
    ys = _experts(xs, block_e + lw["moe_row0"], lw)
    return [ys[dest[:, j]] for j in range(TOP_K)]


def _final_kernel(alpha, x_ref, y0_ref, y1_ref, y2_ref, y3_ref, gate_ref, g2_ref, lg_ref, lb_ref, sh_ref, sc_ref,
                  x2_ref, h_ref):
    gate = gate_ref[...]
    y = y0_ref[...] * gate[:, 0:1]
    for j, y_ref in enumerate((y1_ref, y2_ref, y3_ref), start=1):
        y = y + y_ref[...] * gate[:, j:j + 1]
    x2 = _ln(alpha * x_ref[...] + g2_ref[0] * y) * lg_ref[...] + lb_ref[...]
    x2_ref[...] = x2
    h_ref[...] = (_ln(x2) * (1.0 + sc_ref[0]) + sh_ref[0]).astype(h_ref.dtype)


def _final(x1, y_rows, gate, mods3, mods3_next, lw, nbb, alpha):
    n, d = x1.shape
    assert len(y_rows) == TOP_K == 4
    full = lambda shape: pl.BlockSpec(shape, lambda i: (0,) * len(shape))
    row = pl.BlockSpec((ROW_BLK, d), lambda i: (i, 0))
    return pl.pallas_call(
        functools.partial(_final_kernel, alpha),
        grid=(n // ROW_BLK,),
        in_specs=[row, row, row, row, row, pl.BlockSpec((ROW_BLK, LANE), lambda i: (i, 0)),
                  _mod_spec(5, d, nbb), full((1, d)), full((1, d)), _mod_spec(0, d, nbb), _mod_spec(1, d, nbb)],
        out_specs=[row, row],
        out_shape=[jax.ShapeDtypeStruct((n, d), F32), jax.ShapeDtypeStruct((n, d), BF16)],
        compiler_params=_cparams(("parallel",)),
        name="moe_combine_residual_ln",
    )(x1, *y_rows, gate, mods3, lw["ln2_g"], lw["ln2_b"], mods3_next, mods3_next)


def _rope_tables(ctx_len, seq):
    n_freq = HEAD_DIM // 4
    t = np.arange(seq)
    inv = np.power(ROPE_BASE, -np.arange(n_freq, dtype=np.float32) / n_freq).astype(np.float32)
    row = (t // GRID_W).astype(np.float32)
    col = (t % GRID_W).astype(np.float32)
    ang = jnp.asarray(np.concatenate([row[:, None] * inv, col[:, None] * inv], -1))
    cos = jnp.cos(ang)
    sin = jnp.sin(ang)
    cos_h = jnp.concatenate([cos, cos], -1)
    sin_h = jnp.concatenate([-sin, sin], -1)
    cos_t = jnp.concatenate([jnp.ones((ctx_len, HEAD_DIM), F32), cos_h], 0)
    sin_t = jnp.concatenate([jnp.zeros((ctx_len, HEAD_DIM), F32), sin_h], 0)
    return jnp.tile(cos_t, (1, LANE // HEAD_DIM)), jnp.tile(sin_t, (1, LANE // HEAD_DIM))


def _to_scan_layout(x, batch, t_len):
    return x.reshape(x.shape[:-2] + (batch, t_len, x.shape[-1]))


def _pad_dir_lora(w):
    z = jnp.zeros_like(w[0])
    return jnp.stack([jnp.concatenate([w[0], z], 0), jnp.concatenate([z, w[1]], 0)])


def _from_scan_layout(o, batch, t_len):
    return o.reshape(2, batch * t_len, RW_WIDTH)


def kernel(x, c, ctx, c_ctx, w_mod, b_mod, w_in, rw_mu, rw_w0, rw_w_lora, rw_a0, rw_a_lora, rw_g_lora, rw_k_k, rw_k_a, rw_r_k, rw_gn_w, rw_gn_b, w_out_rw, na_rpb, w_out_na, sc_conv, w_out_sc, b_gate, w_merge, ln1_g, ln1_b, router_w, router_b, moe_w1, moe_b1, moe_w2, moe_b2, ln2_g, ln2_b):
    batch, seq, d = x.shape
    ctx_len = ctx.shape[1]
    depth = w_in.shape[0]
    t_len = ctx_len + seq
    n = batch * t_len
    nbb = t_len // ROW_BLK
    assert batch <= 8 and ctx_len == ROW_BLK and seq % (NA_GROUP * GRID_W) == 0
    alpha = (2 * depth) ** 0.25

    cc = jnp.zeros((16, d), F32).at[:batch].set(c).at[8].set(c_ctx)
    mods = _modulation(cc, w_mod, b_mod)
    mods3 = [mods[l].reshape(16, 1, 6 * d) for l in range(depth)]

    cos_t, sin_t = _rope_tables(ctx_len, seq)
    seg = (np.arange(RW_WIDTH)[:, None] // HEAD_DIM == np.arange(LANE)[None, :]).astype(np.float32)
    consts = dict(cos=cos_t, sin=sin_t, seg=jnp.asarray(seg), seg_t=jnp.asarray(seg.T))
    tables = _na_tables(seq // GRID_W)

    o_na = RW_COLS_PAD
    o_sc = o_na + 3 * NA_WIDTH
    o_gate = o_sc + 3 * SC_WIDTH
    pad_cols = RW_COLS_PAD - RW_COLS

    xs = jnp.concatenate([ctx, x], axis=1).reshape(n, d)
    h = _lnmod(xs, mods3[0], nbb)
    for l in range(depth):
        w_in_p = jnp.concatenate([w_in[l, :, :RW_COLS], jnp.zeros((d, pad_cols), F32), w_in[l, :, RW_COLS:]],
                                 axis=1).astype(BF16)
        lw = dict(
            mu=jnp.pad(rw_mu[l], ((0, 0), (0, pad_cols))), w0=rw_w0[l], w_lora=_pad_dir_lora(rw_w_lora[l]),
            a0=rw_a0[l], a_lora=_pad_dir_lora(rw_a_lora[l]), g_lora=jnp.pad(rw_g_lora[l], ((0, GATE_LORA_PAD - GATE_LORA), (0, 0))),
            k_k=rw_k_k[l].reshape(1, -1), k_a=rw_k_a[l].reshape(1, -1), r_k=rw_r_k[l].reshape(1, -1),
            gn_w=rw_gn_w[l].reshape(1, -1), gn_b=rw_gn_b[l].reshape(1, -1),
            w_out_rw=w_out_rw[l].astype(BF16), w_out_na=w_out_na[l].astype(BF16),
            w_out_sc=w_out_sc[l].astype(BF16), b_gate=b_gate[l].reshape(1, -1),
            w_merge=w_merge[l].astype(BF16), ln1_g=ln1_g[l].reshape(1, -1), ln1_b=ln1_b[l].reshape(1, -1),
            router_w=jnp.pad(router_w[l], ((0, 0), (0, LANE - N_EXPERTS))),
            router_b=jnp.pad(router_b[l], (0, LANE - N_EXPERTS)).reshape(1, -1),
            moe_row0=l * N_EXPERTS,
            moe_w1=moe_w1.reshape((depth * N_EXPERTS,) + moe_w1.shape[2:]),
            moe_b1=moe_b1.reshape(depth * N_EXPERTS, 1, -1),
            moe_w2=moe_w2.reshape((depth * N_EXPERTS,) + moe_w2.shape[2:]),
            moe_b2=moe_b2.reshape(depth * N_EXPERTS, 1, -1),
            ln2_g=ln2_g[l].reshape(1, -1), ln2_b=ln2_b[l].reshape(1, -1),
        )
        proj = _matmul(h, w_in_p)
        r_s, kap, v, w, kd, b, bonus, g = _rwkv_prepare(proj, lw, consts, nbb)
        tl = functools.partial(_to_scan_layout, batch=batch, t_len=t_len)
        wkv = _wkv_scan(tl(r_s), tl(kap), tl(v), tl(w), tl(kd), tl(b), ctx_len)
        y_rw = _rwkv_out(_from_scan_layout(wkv, batch, t_len), bonus, g, lw, consts)
        y_na = _na_attention(proj, _na_bias(na_rpb[l]), tables, o_na, batch, t_len, ctx_len)
        y_sc = _short_conv(proj, jnp.pad(sc_conv[l], ((0, 5), (0, 0))), o_sc, nbb)
        m = _merge1(y_rw, y_na, y_sc, lw, proj, o_gate, d)
        x1, h2, logits = _merge2(m, xs, mods3[l], lw, nbb, alpha)
        gate, idx, rank, counts = _route(logits)
        y_rows = _moe(h2, idx, rank, counts, lw)
        xs, h = _final(x1, y_rows, gate, mods3[l], mods3[min(l + 1, depth - 1)], lw, nbb, alpha)
    return xs.reshape(batch, t_len, d)[:, ctx_len:]
```

```python
import functools
import math

import numpy as np
import jax
import jax.numpy as jnp
from jax import lax
from jax.experimental import pallas as pl
from jax.experimental.pallas import tpu as pltpu

GRID_W = 64
RW_HEADS = 16
HEAD_DIM = 64
RW_WIDTH = RW_HEADS * HEAD_DIM
DECAY_LORA = 64
ICLR_LORA = 64
GATE_LORA = 160
GATE_LORA_PAD = 256
GN_EPS = 64e-5
NA_HEADS = 16
NA_WIDTH = NA_HEADS * HEAD_DIM
NA_WIN_ROWS = 8
NA_WIN_COLS = 16
SC_WIDTH = 1024
RW_COLS = 3 * RW_WIDTH + 2 * DECAY_LORA + 2 * ICLR_LORA + GATE_LORA
RW_COLS_PAD = 3 * RW_WIDTH + 2 * DECAY_LORA + 2 * ICLR_LORA + GATE_LORA_PAD
N_EXPERTS = 32
TOP_K = 4
D_EXPERT = 512
SWIGLU_ALPHA = 1.702
SWIGLU_LIMIT = 7.0
MOE_BLOCK = 256
ROPE_BASE = 10000.0
LN_EPS = 1e-6
NEG_INF = -1e30

LANE = 128
ROW_BLK = 256
MM_TM = 1024
MM_TN = 512
SCAN_TT = 32
SCAN_VSPLIT = 2
SCAN_NACC = 4
SCAN_RELAYOUT_UNROLL = 4
NA_GROUP = 4
NA_KROWS = 12
VMEM_LIMIT = 56 * 1024 * 1024

F32 = jnp.float32
BF16 = jnp.bfloat16


def _cparams(sem):
    return pltpu.CompilerParams(dimension_semantics=sem, vmem_limit_bytes=VMEM_LIMIT)


def _ln(x):
    mu = jnp.mean(x, axis=-1, keepdims=True)
    xc = x - mu
    var = jnp.mean(xc * xc, axis=-1, keepdims=True)
    return xc * lax.rsqrt(var + LN_EPS)


def _sigmoid(x):
    return 1.0 / (1.0 + jnp.exp(-x))


def _mod_kernel(c_ref, w_ref, b_ref, o_ref):
    c = c_ref[...]
    s = (c * _sigmoid(c)).astype(BF16)
    o_ref[0] = jnp.dot(s, w_ref[0].astype(BF16), preferred_element_type=F32) + b_ref[0]


def _modulation(cc, w_mod, b_mod):
    depth, d, n = w_mod.shape
    tn = 1024
    return pl.pallas_call(
        _mod_kernel,
        grid=(depth, n // tn),
        in_specs=[pl.BlockSpec((16, d), lambda l, j: (0, 0)),
                  pl.BlockSpec((1, d, tn), lambda l, j: (l, 0, j)),
                  pl.BlockSpec((1, 1, tn), lambda l, j: (l, 0, j))],
        out_specs=pl.BlockSpec((1, 16, tn), lambda l, j: (l, 0, j)),
        out_shape=jax.ShapeDtypeStruct((depth, 16, n), F32),
        compiler_params=_cparams(("arbitrary", "arbitrary")),
        name="modulation",
    )(cc, w_mod, b_mod.reshape(depth, 1, n))


def _mod_spec(chunk, d, nblk_per_batch):
    def imap(i):
        row = jnp.where(i % nblk_per_batch == 0, 8, i // nblk_per_batch)
        return (row, 0, chunk)
    return pl.BlockSpec((1, 1, d), imap)


def _lnmod_kernel(x_ref, sh_ref, sc_ref, o_ref):
    o_ref[...] = (_ln(x_ref[...]) * (1.0 + sc_ref[0]) + sh_ref[0]).astype(o_ref.dtype)


def _lnmod(x, mods3, nbb):
    n, d = x.shape
    return pl.pallas_call(
        _lnmod_kernel,
        grid=(n // ROW_BLK,),
        in_specs=[pl.BlockSpec((ROW_BLK, d), lambda i: (i, 0)),
                  _mod_spec(0, d, nbb), _mod_spec(1, d, nbb)],
        out_specs=pl.BlockSpec((ROW_BLK, d), lambda i: (i, 0)),
        out_shape=jax.ShapeDtypeStruct((n, d), BF16),
        compiler_params=_cparams(("parallel",)),
        name="ln_mod",
    )(x, mods3, mods3)


def _mm_kernel(a_ref, b_ref, o_ref):
    o_ref[...] = jnp.dot(a_ref[...], b_ref[...], preferred_element_type=F32).astype(o_ref.dtype)


def _matmul(a, b, out_dtype=F32):
    m, k = a.shape
    _, n = b.shape
    tm = MM_TM if m % MM_TM == 0 else ROW_BLK
    return pl.pallas_call(
        _mm_kernel,
        grid=(m // tm, n // MM_TN),
        in_specs=[pl.BlockSpec((tm, k), lambda i, j: (i, 0)),
                  pl.BlockSpec((k, MM_TN), lambda i, j: (0, j))],
        out_specs=pl.BlockSpec((tm, MM_TN), lambda i, j: (i, j)),
        out_shape=jax.ShapeDtypeStruct((m, n), out_dtype),
        compiler_params=_cparams(("parallel", "arbitrary")),
        name="in_proj",
    )(a, b)


def _halo_specs(width, col_blk, nblk):
    per = ROW_BLK // 8

    def prev_map(i):
        return (jnp.maximum(i * per - 1, 0), col_blk)

    def next_map(i):
        return (jnp.minimum((i + 1) * per, nblk * per - 1), col_blk)

    return (pl.BlockSpec((8, width), prev_map), pl.BlockSpec((8, width), next_map))


def _shifted(p, prev8, next8, i, nbb):
    pos = i % nbb
    has_prev = pos >= 2
    has_next = jnp.logical_and(pos >= 1, pos <= nbb - 2)
    rows = lax.broadcasted_iota(jnp.int32, p.shape, 0)
    prow = jnp.where(has_prev, prev8[7:8, :], 0.0)
    nrow = jnp.where(has_next, next8[0:1, :], 0.0)
    p_prev = jnp.where(rows == 0, prow, pltpu.roll(p, 1, 0))
    p_next = jnp.where(rows == p.shape[0] - 1, nrow, pltpu.roll(p, p.shape[0] - 1, 0))
    return p_prev, p_next


def _split3(x):
    hi = x.astype(BF16)
    r = x - hi.astype(F32)
    mid = r.astype(BF16)
    lo = (r - mid.astype(F32)).astype(BF16)
    return hi, mid, lo


def _dot_indicator(x, w):
    hi, mid, lo = _split3(x)
    return (jnp.dot(hi, w, preferred_element_type=F32) + jnp.dot(mid, w, preferred_element_type=F32)
            + jnp.dot(lo, w, preferred_element_type=F32))


def _dot_split(x, w_ref):
    hi, mid, _ = _split3(x)
    w_hi = w_ref[0]
    return (jnp.dot(hi, w_hi, preferred_element_type=F32) + jnp.dot(mid, w_hi, preferred_element_type=F32)
            + jnp.dot(hi, w_ref[1], preferred_element_type=F32))


def _split_weight(w):
    hi = w.astype(BF16)
    return jnp.stack([hi, (w - hi.astype(F32)).astype(BF16)])


def _seg_sum(x, e_ref, et_ref):
    return _dot_indicator(_dot_indicator(x, e_ref[...]), et_ref[...])


def _swap_halves(x):
    lanes = lax.broadcasted_iota(jnp.int32, x.shape, 1)
    first = (lanes % HEAD_DIM) < (HEAD_DIM // 2)
    n = x.shape[1]
    return jnp.where(first, pltpu.roll(x, n - HEAD_DIM // 2, 1), pltpu.roll(x, HEAD_DIM // 2, 1))


def _prep_kernel(nbb, p_ref, pp_ref, pn_ref, mu_ref, w0_ref, wl_ref, a0_ref, al_ref, gl_ref,
                 kk_ref, ka_ref, rk_ref, cos_ref, sin_ref, e_ref, et_ref,
                 r_o, kap_o, v_o, w_o, kd_o, b_o, bonus_o, g_o):
    i = pl.program_id(0)
    c = RW_WIDTH
    p = p_ref[...]
    p_prev, p_next = _shifted(p, pp_ref[...], pn_ref[...], i, nbb)
    pm = p + mu_ref[0:1, :] * (p_prev - p) + mu_ref[1:2, :] * (p_next - p)
    r, k, v = pm[:, :c], pm[:, c:2 * c], pm[:, 2 * c:3 * c]
    o = 3 * c
    wl = jnp.tanh(pm[:, o:o + 2 * DECAY_LORA])
    o += 2 * DECAY_LORA
    al = pm[:, o:o + 2 * ICLR_LORA]
    o += 2 * ICLR_LORA
    gl = _sigmoid(pm[:, o:o + GATE_LORA_PAD])
    g_o[...] = _dot_split(gl, gl_ref)
    kk = k * kk_ref[...]
    ss = _seg_sum(kk * kk, e_ref, et_ref)
    kk = kk / jnp.maximum(jnp.sqrt(ss), 1e-12)
    cos = jnp.concatenate([cos_ref[...]] * (c // LANE), axis=1)
    sin = jnp.concatenate([sin_ref[...]] * (c // LANE), axis=1)

    def rope(x):
        return x * cos + _swap_halves(x) * sin

    kap = rope(kk)
    r_o[...] = rope(r)
    kap_o[...] = kap
    v_o[...] = v
    k_sum = jnp.zeros_like(k)
    for d in range(2):
        w_raw = w0_ref[d:d + 1, :] + _dot_split(wl, wl_ref.at[d])
        w_o[d] = jnp.exp(-math.exp(-0.5) * _sigmoid(w_raw))
        a = _sigmoid(a0_ref[d:d + 1, :] + _dot_split(al, al_ref.at[d]))
        kd = k * (1.0 + (a - 1.0) * ka_ref[...])
        k_sum = k_sum + kd
        kd_o[d] = rope(kd)
        b_o[d] = kap * a
    bonus_o[...] = _seg_sum(r * k_sum * rk_ref[...], e_ref, et_ref) * v


def _rwkv_prepare(proj, lw, consts, nbb):
    n = proj.shape[0]
    c = RW_WIDTH
    nblk = n // ROW_BLK
    prev_spec, next_spec = _halo_specs(RW_COLS_PAD, 0, nblk)
    full = lambda shape: pl.BlockSpec(shape, lambda i: (0,) * len(shape))
    row_c = pl.BlockSpec((ROW_BLK, c), lambda i: (i, 0))
    row_2c = pl.BlockSpec((2, ROW_BLK, c), lambda i: (0, i, 0))
    tab = pl.BlockSpec((ROW_BLK, LANE), lambda i: (i % nbb, 0))
    outs = pl.pallas_call(
        functools.partial(_prep_kernel, nbb),
        grid=(nblk,),
        in_specs=[pl.BlockSpec((ROW_BLK, RW_COLS_PAD), lambda i: (i, 0)), prev_spec, next_spec,
                  full((2, RW_COLS_PAD)), full((2, c)), full((2, 2, 2 * DECAY_LORA, c)), full((2, c)),
                  full((2, 2, 2 * ICLR_LORA, c)), full((2, GATE_LORA_PAD, c)), full((1, c)), full((1, c)),
                  full((1, c)), tab, tab, full((c, LANE)), full((LANE, c))],
        out_specs=[row_c, row_c, row_c, row_2c, row_2c, row_2c, row_c, row_c],
        out_shape=[jax.ShapeDtypeStruct((n, c), F32)] * 3 + [jax.ShapeDtypeStruct((2, n, c), F32)] * 3
        + [jax.ShapeDtypeStruct((n, c), F32)] * 2,
        compiler_params=_cparams(("parallel",)),
        name="rwkv_prepare",
    )(proj, proj, proj, lw["mu"], lw["w0"], lw["w_lora"], lw["a0"], lw["a_lora"], lw["g_lora"],
      lw["k_k"], lw["k_a"], lw["r_k"], consts["cos"], consts["sin"], consts["seg"], consts["seg_t"])
    return outs


def _pair_to_lanes(xa, xb):
    nl = xa.shape[0]
    res = jnp.concatenate([xa, xb], axis=0).T
    top, bot = res[:HEAD_DIM], res[HEAD_DIM:]
    first = lax.broadcasted_iota(jnp.int32, top.shape, 1) < nl
    tile_a = jnp.where(first, top, pltpu.roll(bot, nl, 1))
    tile_b = jnp.where(first, pltpu.roll(top, nl, 1), bot)
    return tile_a, tile_b


def _scan_kernel(r_ref, kap_ref, v_ref, w_ref, kd_ref, b_ref, o_ref, s_ref, vec_ref, out_ref):
    d = pl.program_id(0)
    nb, tt = r_ref.shape[0], r_ref.shape[1]
    n_pair = r_ref.shape[2] // (2 * HEAD_DIM)
    nl = nb * n_pair
    vh = HEAD_DIM // SCAN_VSPLIT

    @pl.when(pl.program_id(1) == 0)
    def _():
        s_ref[...] = jnp.zeros_like(s_ref)

    def to_lanes(t, carry):
        slab = lambda x: jnp.concatenate([x[:, p * 2 * HEAD_DIM:(p + 1) * 2 * HEAD_DIM] for p in range(n_pair)], axis=0)
        pairs = ((kap_ref[:, t, :], r_ref[:, t, :]), (w_ref[0, :, t, :], kd_ref[0, :, t, :]),
                 (b_ref[0, :, t, :], v_ref[:, t, :]))
        for j, (xa, xb) in enumerate(pairs):
            tile_a, tile_b = _pair_to_lanes(slab(xa), slab(xb))
            vec_ref[t, 2 * j] = tile_a
            vec_ref[t, 2 * j + 1] = tile_b
        return carry

    lax.fori_loop(0, tt, to_lanes, 0, unroll=SCAN_RELAYOUT_UNROLL)

    def step(i, carry):
        t = jnp.where(d == 0, i, tt - 1 - i)
        for h in range(SCAN_VSPLIT):
            rows = slice(h * vh, (h + 1) * vh)
            acc = [None] * SCAN_NACC
            for k in range(HEAD_DIM):
                term = s_ref[k, rows] * vec_ref[t, 0, k:k + 1, :]
                acc[k % SCAN_NACC] = term if acc[k % SCAN_NACC] is None else acc[k % SCAN_NACC] + term
            skk = functools.reduce(lambda a, b: a + b, acc)
            v = vec_ref[t, 5, rows]
            acc = [None] * SCAN_NACC
            for k in range(HEAD_DIM):
                s_new = (s_ref[k, rows] * vec_ref[t, 2, k:k + 1, :] - skk * vec_ref[t, 4, k:k + 1, :]
                         + v * vec_ref[t, 3, k:k + 1, :])
                s_ref[k, rows] = s_new
                term = s_new * vec_ref[t, 1, k:k + 1, :]
                acc[k % SCAN_NACC] = term if acc[k % SCAN_NACC] is None else acc[k % SCAN_NACC] + term
            out_ref[t, rows] = functools.reduce(lambda a, b: a + b, acc)
        return carry

    lax.fori_loop(0, tt, step, 0)

    def to_tokens(t, carry):
        out = out_ref[t]
        back = jnp.concatenate([out, pltpu.roll(out, nl, 1)], axis=0).T[:nl]
        o_ref[0, :, t, :] = jnp.concatenate([back[p * nb:(p + 1) * nb] for p in range(n_pair)], axis=1)
        return carry

    lax.fori_loop(0, tt, to_tokens, 0, unroll=SCAN_RELAYOUT_UNROLL)


def _scan_block(n_ctx_blk, nb):
    def blk(d, j):
        rev = jnp.where(j < n_ctx_blk, n_ctx_blk - 1 - j, nb - 1 - (j - n_ctx_blk))
        return jnp.where(d == 0, j, rev)
    return blk


def _wkv_scan(r, kap, v, w, kd, b, ctx_len):
    batch, t, width = r.shape
    lanes = batch * width // HEAD_DIM
    nb = t // SCAN_TT
    blk = _scan_block(ctx_len // SCAN_TT, nb)
    shared = pl.BlockSpec((batch, SCAN_TT, width), lambda d, j: (0, blk(d, j), 0))
    per_dir = pl.BlockSpec((1, batch, SCAN_TT, width), lambda d, j: (d, 0, blk(d, j), 0))
    return pl.pallas_call(
        _scan_kernel,
        grid=(2, nb),
        in_specs=[shared, shared, shared, per_dir, per_dir, per_dir],
        out_specs=per_dir,
        out_shape=jax.ShapeDtypeStruct((2, batch, t, width), F32),
        scratch_shapes=[pltpu.VMEM((HEAD_DIM, HEAD_DIM, lanes), F32),
                        pltpu.VMEM((SCAN_TT, 6, HEAD_DIM, lanes), F32),
                        pltpu.VMEM((SCAN_TT, HEAD_DIM, lanes), F32)],
        compiler_params=_cparams(("arbitrary", "arbitrary")),
        name="wkv_scan",
    )(r, kap, v, w, kd, b)


def _rwkv_out_kernel(wkv_ref, bonus_ref, g_ref, gw_ref, gb_ref, e_ref, et_ref, o_ref):
    x = wkv_ref[0] + wkv_ref[1]
    inv_n = 1.0 / HEAD_DIM
    mu = _seg_sum(x, e_ref, et_ref) * inv_n
    xc = x - mu
    var = _seg_sum(xc * xc, e_ref, et_ref) * inv_n
    y = xc * lax.rsqrt(var + GN_EPS) * gw_ref[...] + gb_ref[...]
    o_ref[...] = ((y + bonus_ref[...]) * g_ref[...]).astype(o_ref.dtype)


def _rwkv_out(wkv, bonus, g, lw, consts):
    n, c = bonus.shape
    full = lambda shape: pl.BlockSpec(shape, lambda i: (0,) * len(shape))
    row_c = pl.BlockSpec((ROW_BLK, c), lambda i: (i, 0))
    return pl.pallas_call(
        _rwkv_out_kernel,
        grid=(n // ROW_BLK,),
        in_specs=[pl.BlockSpec((2, ROW_BLK, c), lambda i: (0, i, 0)), row_c, row_c,
                  full((1, c)), full((1, c)), full((c, LANE)), full((LANE, c))],
        out_specs=row_c,
        out_shape=jax.ShapeDtypeStruct((n, c), BF16),
        compiler_params=_cparams(("parallel",)),
        name="rwkv_out",
    )(wkv, bonus, g, lw["gn_w"], lw["gn_b"], consts["seg"], consts["seg_t"])


def _conv_kernel(nbb, bg_ref, cg_ref, x_ref, cgp_ref, xp_ref, cgn_ref, xn_ref, w_ref, o_ref):
    i = pl.program_id(0)
    u = cg_ref[...] * x_ref[...]
    u_prev, u_next = _shifted(u, cgp_ref[...] * xp_ref[...], cgn_ref[...] * xn_ref[...], i, nbb)
    y = w_ref[0:1, :] * u_prev + w_ref[1:2, :] * u + w_ref[2:3, :] * u_next
    o_ref[...] = (bg_ref[...] * y).astype(o_ref.dtype)


def _short_conv(proj, conv_w, col0, nbb):
    n = proj.shape[0]
    cw = MM_TN
    nc = SC_WIDTH // cw
    assert col0 % cw == 0
    cb = col0 // cw
    nblk = n // ROW_BLK
    blk = lambda s: pl.BlockSpec((ROW_BLK, cw), lambda i, j: (i, cb + s * nc + j))

    def halos(s):
        prev_spec, next_spec = _halo_specs(cw, 0, nblk)
        pm, nm = prev_spec.index_map, next_spec.index_map
        return (pl.BlockSpec((8, cw), lambda i, j: (pm(i)[0], cb + s * nc + j)),
                pl.BlockSpec((8, cw), lambda i, j: (nm(i)[0], cb + s * nc + j)))

    cg_prev, cg_next = halos(1)
    x_prev, x_next = halos(2)
    return pl.pallas_call(
        functools.partial(_conv_kernel, nbb),
        grid=(nblk, nc),
        in_specs=[blk(0), blk(1), blk(2), cg_prev, x_prev, cg_next, x_next,
                  pl.BlockSpec((8, cw), lambda i, j: (0, j))],
        out_specs=pl.BlockSpec((ROW_BLK, cw), lambda i, j: (i, j)),
        out_shape=jax.ShapeDtypeStruct((n, SC_WIDTH), BF16),
        compiler_params=_cparams(("parallel", "arbitrary")),
        name="short_conv",
    )(proj, proj, proj, proj, proj, proj, proj, conv_w)


def _na_tables(rows):
    kr = min(NA_WIN_ROWS, rows)
    n_groups = rows // NA_GROUP
    krows = min(NA_KROWS, rows)
    n_dr = 2 * NA_WIN_ROWS - 1
    dr = np.full((n_groups, NA_GROUP, krows), n_dr, np.int32)
    bases = []
    for g in range(n_groups):
        r0 = g * NA_GROUP
        base = int(np.clip(np.clip(r0 - kr // 2, 0, rows - kr), 0, rows - krows))
        bases.append(base)
        for rl in range(NA_GROUP):
            r = r0 + rl
            rs = int(np.clip(r - kr // 2, 0, rows - kr))
            for j in range(krows):
                if rs <= base + j < rs + kr:
                    dr[g, rl, j] = base + j - r + NA_WIN_ROWS - 1
            assert (dr[g, rl] < n_dr).sum() == kr
    uniq, table_of = [], []
    for g in range(n_groups):
        for u, gu in enumerate(uniq):
            if np.array_equal(dr[g], dr[gu]):
                table_of.append(u)
                break
        else:
            table_of.append(len(uniq))
            uniq.append(g)
    return dr[uniq], tuple(bases), tuple(table_of)


def _na_kernel(ctx_len, dr, bases, table_of, q_ref, k_ref, v_ref, bias_ref, o_ref):
    scale = HEAD_DIM ** -0.5
    nq = NA_GROUP * GRID_W
    krows = dr.shape[2]
    nk = krows * GRID_W
    first = lax.broadcasted_iota(jnp.int32, (GRID_W, 2 * GRID_W), 1) < GRID_W

    def bias_of(table, h):
        rows = []
        for rl in range(NA_GROUP):
            tiles = [jnp.where(first, bias_ref[h, int(dr[table, rl, j])], bias_ref[h, int(dr[table, rl, j + 1])])
                     for j in range(0, krows, 2)]
            rows.append(jnp.concatenate(tiles, axis=1))
        return jnp.concatenate(rows, axis=0)

    outs_heads = []
    for h in range(2):
        sl = slice(h * HEAD_DIM, (h + 1) * HEAD_DIM)
        q = (q_ref[:, sl] * scale).astype(BF16)
        k = k_ref[:, sl].astype(BF16)
        v = v_ref[:, sl].astype(BF16)
        kc, vc = k[:ctx_len], v[:ctx_len]
        dn = (((1,), (1,)), ((), ()))
        s = lax.dot_general(q[:ctx_len], kc, dn, preferred_element_type=F32)
        s = s - jnp.max(s, axis=-1, keepdims=True)
        e = jnp.exp(s)
        p = (e / jnp.sum(e, axis=-1, keepdims=True)).astype(BF16)
        pieces = [jnp.dot(p, vc, preferred_element_type=F32)]
        biases = [bias_of(t, h) for t in range(dr.shape[0])]
        for g, base in enumerate(bases):
            q0 = ctx_len + g * nq
            k0 = ctx_len + base * GRID_W
            qg = q[q0:q0 + nq]
            s_win = (lax.dot_general(qg, k[k0:k0 + nk], dn, preferred_element_type=F32)
                     + biases[table_of[g]])
            s_ctx = lax.dot_general(qg, kc, dn, preferred_element_type=F32)
            m = jnp.maximum(jnp.max(s_win, axis=-1, keepdims=True), jnp.max(s_ctx, axis=-1, keepdims=True))
            e_win = jnp.exp(s_win - m)
            e_ctx = jnp.exp(s_ctx - m)
            inv = 1.0 / (jnp.sum(e_win, axis=-1, keepdims=True) + jnp.sum(e_ctx, axis=-1, keepdims=True))
            acc = jnp.dot((e_win * inv).astype(BF16), v[k0:k0 + nk], preferred_element_type=F32)
            acc = acc + jnp.dot((e_ctx * inv).astype(BF16), vc, preferred_element_type=F32)
            pieces.append(acc)
        outs_heads.append(jnp.concatenate(pieces, axis=0))
    o_ref[...] = jnp.concatenate(outs_heads, axis=1).astype(o_ref.dtype)


def _na_bias(rpb):
    qc = np.arange(GRID_W)[:, None]
    kc = np.arange(GRID_W)[None, :]
    wstart = np.clip(qc - NA_WIN_COLS // 2, 0, GRID_W - NA_WIN_COLS)
    col_ok = (kc >= wstart) & (kc < wstart + NA_WIN_COLS)
    dc = np.clip(kc - qc + NA_WIN_COLS - 1, 0, 2 * NA_WIN_COLS - 2)
    blocks = jnp.where(col_ok, rpb[:, :, dc], NEG_INF)
    blocks = jnp.concatenate([blocks, jnp.full_like(blocks[:, :1], NEG_INF)], axis=1)
    return jnp.concatenate([blocks, blocks], axis=-1)


def _na_attention(proj, bias, tables, col0, batch, t_len, ctx_len):
    n = proj.shape[0]
    cb = col0 // LANE
    hp = NA_HEADS // 2
    seq = lambda j: pl.BlockSpec((t_len, LANE), lambda h, b: (b, cb + j * hp + h))
    return pl.pallas_call(
        functools.partial(_na_kernel, ctx_len, *tables),
        grid=(hp, batch),
        in_specs=[seq(0), seq(1), seq(2),
                  pl.BlockSpec((2,) + bias.shape[1:], lambda h, b: (h, 0, 0, 0))],
        out_specs=pl.BlockSpec((t_len, LANE), lambda h, b: (b, h)),
        out_shape=jax.ShapeDtypeStruct((n, NA_WIDTH), BF16),
        compiler_params=_cparams(("arbitrary", "arbitrary")),
        name="na_attention",
    )(proj, proj, proj, bias)


def _merge1_kernel(yr_ref, yn_ref, ys_ref, wr_ref, wn_ref, ws_ref, g0_ref, g1_ref, g2_ref,
                   b0_ref, b1_ref, b2_ref, o_ref):
    m = _sigmoid(g0_ref[...] + b0_ref[...]) * jnp.dot(yr_ref[...], wr_ref[...], preferred_element_type=F32)
    m = m + _sigmoid(g1_ref[...] + b1_ref[...]) * jnp.dot(yn_ref[...], wn_ref[...], preferred_element_type=F32)
    m = m + _sigmoid(g2_ref[...] + b2_ref[...]) * jnp.dot(ys_ref[...], ws_ref[...], preferred_element_type=F32)
    o_ref[...] = m.astype(o_ref.dtype)


def _merge1(y_rw, y_na, y_sc, lw, proj, gate_col0, d):
    n, c = y_rw.shape
    tm, tn = 512, MM_TN
    gb = gate_col0 // tn
    nd = d // tn
    ysp = pl.BlockSpec((tm, c), lambda i, j: (i, 0))
    wsp = pl.BlockSpec((c, tn), lambda i, j: (0, j))
    gsp = lambda br: pl.BlockSpec((tm, tn), lambda i, j: (i, gb + br * nd + j))
    bsp = lambda br: pl.BlockSpec((1, tn), lambda i, j: (0, br * nd + j))
    return pl.pallas_call(
        _merge1_kernel,
        grid=(n // tm, nd),
        in_specs=[ysp, ysp, ysp, wsp, wsp, wsp, gsp(0), gsp(1), gsp(2), bsp(0), bsp(1), bsp(2)],
        out_specs=pl.BlockSpec((tm, tn), lambda i, j: (i, j)),
        out_shape=jax.ShapeDtypeStruct((n, d), BF16),
        compiler_params=_cparams(("parallel", "arbitrary")),
        name="merge_branches",
    )(y_rw, y_na, y_sc, lw["w_out_rw"], lw["w_out_na"], lw["w_out_sc"], proj, proj, proj,
      lw["b_gate"], lw["b_gate"], lw["b_gate"])


def _merge2_kernel(alpha, m_ref, w_ref, x_ref, g1_ref, lg_ref, lb_ref, sh_ref, sc_ref, rw_ref, rb_ref,
                   x1_ref, h2_ref, lo_ref):
    mix = jnp.dot(m_ref[...], w_ref[...], preferred_element_type=F32)
    x1 = _ln(alpha * x_ref[...] + g1_ref[0] * mix) * lg_ref[...] + lb_ref[...]
    x1_ref[...] = x1
    h2 = _ln(x1) * (1.0 + sc_ref[0]) + sh_ref[0]
    h2_ref[...] = h2.astype(h2_ref.dtype)
    lo_ref[...] = _dot_split(h2, rw_ref) + rb_ref[...]


def _merge2(m, x, mods3, lw, nbb, alpha):
    n, d = x.shape
    full = lambda shape: pl.BlockSpec(shape, lambda i: (0,) * len(shape))
    row = pl.BlockSpec((ROW_BLK, d), lambda i: (i, 0))
    return pl.pallas_call(
        functools.partial(_merge2_kernel, alpha),
        grid=(n // ROW_BLK,),
        in_specs=[row, full((d, d)), row, _mod_spec(2, d, nbb), full((1, d)), full((1, d)),
                  _mod_spec(3, d, nbb), _mod_spec(4, d, nbb), full((2, d, LANE)), full((1, LANE))],
        out_specs=[row, row, pl.BlockSpec((ROW_BLK, LANE), lambda i: (i, 0))],
        out_shape=[jax.ShapeDtypeStruct((n, d), F32), jax.ShapeDtypeStruct((n, d), BF16),
                   jax.ShapeDtypeStruct((n, LANE), F32)],
        compiler_params=_cparams(("parallel",)),
        name="merge_out_ln",
    )(m, lw["w_merge"], x, mods3, lw["ln1_g"], lw["ln1_b"], mods3, mods3, lw["router_w"], lw["router_b"])


def _route_kernel(lo_ref, gate_ref, idx_ref, rank_ref, counts_ref, carry_ref):
    x = lo_ref[...]
    lanes = lax.broadcasted_iota(jnp.int32, x.shape, 1).astype(F32)
    x = jnp.where(lanes < N_EXPERTS, x, -jnp.inf)
    vals, idxs = [], []
    for _ in range(TOP_K):
        m = jnp.max(x, axis=-1, keepdims=True)
        sel = jnp.min(jnp.where(x == m, lanes, float(LANE)), axis=-1, keepdims=True)
        vals.append(m)
        idxs.append(sel)
        x = jnp.where(lanes == sel, -jnp.inf, x)
    es = [jnp.exp(v - vals[0]) for v in vals]
    tot = es[0] + es[1] + es[2] + es[3]
    @pl.when(pl.program_id(0) == 0)
    def _():
        carry_ref[...] = jnp.zeros_like(carry_ref)

    onehot = jnp.zeros(lo_ref.shape, F32)
    for j in range(TOP_K):
        onehot = onehot + jnp.where(lanes == idxs[j], 1.0, 0.0)
    nr = lo_ref.shape[0]
    below = (lax.broadcasted_iota(jnp.int32, (nr, nr), 1) < lax.broadcasted_iota(jnp.int32, (nr, nr), 0))
    prefix = jnp.dot(below.astype(BF16), onehot.astype(BF16), preferred_element_type=F32) + carry_ref[...]
    gate = jnp.zeros(lo_ref.shape, F32)
    idx = jnp.zeros(lo_ref.shape, F32)
    rank = jnp.zeros(lo_ref.shape, F32)
    for j in range(TOP_K):
        gate = jnp.where(lanes == j, es[j] / tot, gate)
        idx = jnp.where(lanes == j, idxs[j], idx)
        rank_j = jnp.sum(jnp.where(lanes == idxs[j], prefix, 0.0), axis=-1, keepdims=True)
        rank = jnp.where(lanes == j, rank_j, rank)
    gate_ref[...] = gate
    idx_ref[...] = idx.astype(jnp.int32)
    rank_ref[...] = rank.astype(jnp.int32)
    carry_ref[...] = carry_ref[...] + jnp.sum(onehot, axis=0, keepdims=True)
    counts_ref[...] = carry_ref[...].astype(jnp.int32)


def _route(logits):
    n = logits.shape[0]
    row = pl.BlockSpec((ROW_BLK, LANE), lambda i: (i, 0))
    return pl.pallas_call(
        _route_kernel,
        grid=(n // ROW_BLK,),
        in_specs=[row],
        out_specs=[row, row, row, pl.BlockSpec((1, LANE), lambda i: (0, 0))],
        out_shape=[jax.ShapeDtypeStruct((n, LANE), F32), jax.ShapeDtypeStruct((n, LANE), jnp.int32),
                   jax.ShapeDtypeStruct((n, LANE), jnp.int32), jax.ShapeDtypeStruct((1, LANE), jnp.int32)],
        scratch_shapes=[pltpu.VMEM((1, LANE), F32)],
        compiler_params=_cparams(("arbitrary",)),
        name="moe_route",
    )(logits)


def _expert_kernel(be_ref, x_ref, w1_ref, b1_ref, w2_ref, b2_ref, sel_ref, o_ref, w1b_ref, w2b_ref):
    i = pl.program_id(0)

    @pl.when(jnp.logical_or(i == 0, be_ref[i] != be_ref[jnp.maximum(i - 1, 0)]))
    def _():
        w1b_ref[...] = w1_ref[0].astype(BF16)
        w2b_ref[...] = w2_ref[0].astype(BF16)

    z = jnp.dot(x_ref[...], w1b_ref[...], preferred_element_type=F32) + b1_ref[0]
    z_glu = jnp.minimum(z, SWIGLU_LIMIT)
    z_lin = jnp.clip(pltpu.roll(z, z.shape[1] - 1, 1), -SWIGLU_LIMIT, SWIGLU_LIMIT)
    act = z_glu * _sigmoid(SWIGLU_ALPHA * z_glu) * (z_lin + 1.0)
    lanes = lax.broadcasted_iota(jnp.int32, act.shape, 1)
    act = jnp.where(lanes % 2 == 0, act, 0.0).astype(BF16)
    act = jnp.dot(act, sel_ref[...], preferred_element_type=F32).astype(BF16)
    o_ref[...] = jnp.dot(act, w2b_ref[...], preferred_element_type=F32) + b2_ref[0]


def _experts(xs, block_e, lw):
    n_slots, d = xs.shape
    n_blocks = n_slots // MOE_BLOCK
    f2 = 2 * D_EXPERT
    sel = jnp.asarray(np.arange(f2)[:, None] == 2 * np.arange(D_EXPERT)[None, :], BF16)
    grid_spec = pltpu.PrefetchScalarGridSpec(
        num_scalar_prefetch=1,
        grid=(n_blocks,),
        in_specs=[pl.BlockSpec((MOE_BLOCK, d), lambda i, be: (i, 0)),
                  pl.BlockSpec((1, d, f2), lambda i, be: (be[i], 0, 0)),
                  pl.BlockSpec((1, 1, f2), lambda i, be: (be[i], 0, 0)),
                  pl.BlockSpec((1, D_EXPERT, d), lambda i, be: (be[i], 0, 0)),
                  pl.BlockSpec((1, 1, d), lambda i, be: (be[i], 0, 0)),
                  pl.BlockSpec((f2, D_EXPERT), lambda i, be: (0, 0))],
        out_specs=pl.BlockSpec((MOE_BLOCK, d), lambda i, be: (i, 0)),
        scratch_shapes=[pltpu.VMEM((d, f2), BF16), pltpu.VMEM((D_EXPERT, d), BF16)],
    )
    return pl.pallas_call(
        _expert_kernel,
        grid_spec=grid_spec,
        out_shape=jax.ShapeDtypeStruct((n_slots, d), F32),
        compiler_params=_cparams(("arbitrary",)),
        name="moe_experts",
    )(block_e, xs, lw["moe_w1"], lw["moe_b1"], lw["moe_w2"], lw["moe_b2"], sel)


def _moe(h2, idx, rank, counts, lw):
    n, d = h2.shape
    a = n * TOP_K
    counts = counts[0, :N_EXPERTS]
    padded = (counts + MOE_BLOCK - 1) // MOE_BLOCK * MOE_BLOCK
    pad_end = jnp.cumsum(padded)
    pad_start = pad_end - padded
    dest = pad_start[idx[:, :TOP_K]] + rank[:, :TOP_K]
    n_blocks = -(-a // MOE_BLOCK) + N_EXPERTS
    n_slots = n_blocks * MOE_BLOCK
    slot_tok = jnp.zeros(n_slots, jnp.int32).at[dest.reshape(a)].set(jnp.arange(a, dtype=jnp.int32) // TOP_K)
    block_start = jnp.arange(n_blocks, dtype=jnp.int32) * MOE_BLOCK
    block_e = jnp.minimum(jnp.sum(block_start[:, None] >= pad_end[None, :], axis=1), N_EXPERTS - 1).astype(jnp.int32)
    xs = h2[slot_tok]
    ys = _experts(xs, block_e + lw["moe_row0"], lw)
    return [ys[dest[:, j]] for j in range(TOP_K)]


def _final_kernel(alpha, x_ref, y0_ref, y1_ref, y2_ref, y3_ref, gate_ref, g2_ref, lg_ref, lb_ref, sh_ref, sc_ref,
                  x2_ref, h_ref):
    gate = gate_ref[...]
    y = y0_ref[...] * gate[:, 0:1]
    for j, y_ref in enumerate((y1_ref, y2_ref, y3_ref), start=1):
        y = y + y_ref[...] * gate[:, j:j + 1]
    x2 = _ln(alpha * x_ref[...] + g2_ref[0] * y) * lg_ref[...] + lb_ref[...]
    x2_ref[...] = x2
    h_ref[...] = (_ln(x2) * (1.0 + sc_ref[0]) + sh_ref[0]).astype(h_ref.dtype)


def _final(x1, y_rows, gate, mods3, mods3_next, lw, nbb, alpha):
    n, d = x1.shape
    assert len(y_rows) == TOP_K == 4
    full = lambda shape: pl.BlockSpec(shape, lambda i: (0,) * len(shape))
    row = pl.BlockSpec((ROW_BLK, d), lambda i: (i, 0))
    return pl.pallas_call(
        functools.partial(_final_kernel, alpha),
        grid=(n // ROW_BLK,),
        in_specs=[row, row, row, row, row, pl.BlockSpec((ROW_BLK, LANE), lambda i: (i, 0)),
                  _mod_spec(5, d, nbb), full((1, d)), full((1, d)), _mod_spec(0, d, nbb), _mod_spec(1, d, nbb)],
        out_specs=[row, row],
        out_shape=[jax.ShapeDtypeStruct((n, d), F32), jax.ShapeDtypeStruct((n, d), BF16)],
        compiler_params=_cparams(("parallel",)),
        name="moe_combine_residual_ln",
    )(x1, *y_rows, gate, mods3, lw["ln2_g"], lw["ln2_b"], mods3_next, mods3_next)


def _rope_tables(ctx_len, seq):
    n_freq = HEAD_DIM // 4
    t = np.arange(seq)
    inv = np.power(ROPE_BASE, -np.arange(n_freq, dtype=np.float32) / n_freq).astype(np.float32)
    row = (t // GRID_W).astype(np.float32)
    col = (t % GRID_W).astype(np.float32)
    ang = jnp.asarray(np.concatenate([row[:, None] * inv, col[:, None] * inv], -1))
    cos = jnp.cos(ang)
    sin = jnp.sin(ang)
    cos_h = jnp.concatenate([cos, cos], -1)
    sin_h = jnp.concatenate([-sin, sin], -1)
    cos_t = jnp.concatenate([jnp.ones((ctx_len, HEAD_DIM), F32), cos_h], 0)
    sin_t = jnp.concatenate([jnp.zeros((ctx_len, HEAD_DIM), F32), sin_h], 0)
    return jnp.tile(cos_t, (1, LANE // HEAD_DIM)), jnp.tile(sin_t, (1, LANE // HEAD_DIM))


def _to_scan_layout(x, batch, t_len):
    return x.reshape(x.shape[:-2] + (batch, t_len, x.shape[-1]))


def _pad_dir_lora(w):
    z = jnp.zeros_like(w[0])
    return jnp.stack([_split_weight(jnp.concatenate([w[0], z], 0)), _split_weight(jnp.concatenate([z, w[1]], 0))])


def _from_scan_layout(o, batch, t_len):
    return o.reshape(2, batch * t_len, RW_WIDTH)


def kernel(x, c, ctx, c_ctx, w_mod, b_mod, w_in, rw_mu, rw_w0, rw_w_lora, rw_a0, rw_a_lora, rw_g_lora, rw_k_k, rw_k_a, rw_r_k, rw_gn_w, rw_gn_b, w_out_rw, na_rpb, w_out_na, sc_conv, w_out_sc, b_gate, w_merge, ln1_g, ln1_b, router_w, router_b, moe_w1, moe_b1, moe_w2, moe_b2, ln2_g, ln2_b):
    batch, seq, d = x.shape
    ctx_len = ctx.shape[1]
    depth = w_in.shape[0]
    t_len = ctx_len + seq
    n = batch * t_len
    nbb = t_len // ROW_BLK
    assert batch <= 8 and ctx_len == ROW_BLK and seq % (NA_GROUP * GRID_W) == 0
    alpha = (2 * depth) ** 0.25

    cc = jnp.zeros((16, d), F32).at[:batch].set(c).at[8].set(c_ctx)
    mods = _modulation(cc, w_mod, b_mod)
    mods3 = [mods[l].reshape(16, 1, 6 * d) for l in range(depth)]

    cos_t, sin_t = _rope_tables(ctx_len, seq)
    seg = (np.arange(RW_WIDTH)[:, None] // HEAD_DIM == np.arange(LANE)[None, :]).astype(np.float32)
    consts = dict(cos=cos_t, sin=sin_t, seg=jnp.asarray(seg, BF16), seg_t=jnp.asarray(seg.T, BF16))
    tables = _na_tables(seq // GRID_W)

    o_na = RW_COLS_PAD
    o_sc = o_na + 3 * NA_WIDTH
    o_gate = o_sc + 3 * SC_WIDTH
    pad_cols = RW_COLS_PAD - RW_COLS

    xs = jnp.concatenate([ctx, x], axis=1).reshape(n, d)
    h = _lnmod(xs, mods3[0], nbb)
    for l in range(depth):
        w_in_p = jnp.concatenate([w_in[l, :, :RW_COLS], jnp.zeros((d, pad_cols), F32), w_in[l, :, RW_COLS:]],
                                 axis=1).astype(BF16)
        lw = dict(
            mu=jnp.pad(rw_mu[l], ((0, 0), (0, pad_cols))), w0=rw_w0[l], w_lora=_pad_dir_lora(rw_w_lora[l]),
            a0=rw_a0[l], a_lora=_pad_dir_lora(rw_a_lora[l]),
            g_lora=_split_weight(jnp.pad(rw_g_lora[l], ((0, GATE_LORA_PAD - GATE_LORA), (0, 0)))),
            k_k=rw_k_k[l].reshape(1, -1), k_a=rw_k_a[l].reshape(1, -1), r_k=rw_r_k[l].reshape(1, -1),
            gn_w=rw_gn_w[l].reshape(1, -1), gn_b=rw_gn_b[l].reshape(1, -1),
            w_out_rw=w_out_rw[l].astype(BF16), w_out_na=w_out_na[l].astype(BF16),
            w_out_sc=w_out_sc[l].astype(BF16), b_gate=b_gate[l].reshape(1, -1),
            w_merge=w_merge[l].astype(BF16), ln1_g=ln1_g[l].reshape(1, -1), ln1_b=ln1_b[l].reshape(1, -1),
            router_w=_split_weight(jnp.pad(router_w[l], ((0, 0), (0, LANE - N_EXPERTS)))),
            router_b=jnp.pad(router_b[l], (0, LANE - N_EXPERTS)).reshape(1, -1),
            moe_row0=l * N_EXPERTS,
            moe_w1=moe_w1.reshape((depth * N_EXPERTS,) + moe_w1.shape[2:]),
            moe_b1=moe_b1.reshape(depth * N_EXPERTS, 1, -1),
            moe_w2=moe_w2.reshape((depth * N_EXPERTS,) + moe_w2.shape[2:]),
            moe_b2=moe_b2.reshape(depth * N_EXPERTS, 1, -1),
            ln2_g=ln2_g[l].reshape(1, -1), ln2_b=ln2_b[l].reshape(1, -1),
        )
        proj = _matmul(h, w_in_p)
        r_s, kap, v, w, kd, b, bonus, g = _rwkv_prepare(proj, lw, consts, nbb)
        tl = functools.partial(_to_scan_layout, batch=batch, t_len=t_len)
        wkv = _wkv_scan(tl(r_s), tl(kap), tl(v), tl(w), tl(kd), tl(b), ctx_len)
        y_rw = _rwkv_out(_from_scan_layout(wkv, batch, t_len), bonus, g, lw, consts)
        y_na = _na_attention(proj, _na_bias(na_rpb[l]), tables, o_na, batch, t_len, ctx_len)
        y_sc = _short_conv(proj, jnp.pad(sc_conv[l], ((0, 5), (0, 0))), o_sc, nbb)
        m = _merge1(y_rw, y_na, y_sc, lw, proj, o_gate, d)
        x1, h2, logits = _merge2(m, xs, mods3[l], lw, nbb, alpha)
        gate, idx, rank, counts = _route(logits)
        y_rows = _moe(h2, idx, rank, counts, lw)
        xs, h = _final(x1, y_rows, gate, mods3[l], mods3[min(l + 1, depth - 1)], lw, nbb, alpha)
    return xs.reshape(batch, t_len, d)[:, ctx_len:]
```

```python
import functools
import math

import numpy as np
import jax
import jax.numpy as jnp
from jax import lax
from jax.experimental import pallas as pl
from jax.experimental.pallas import tpu as pltpu

GRID_W = 64
RW_HEADS = 16
HEAD_DIM = 64
RW_WIDTH = RW_HEADS * HEAD_DIM
DECAY_LORA = 64
ICLR_LORA = 64
GATE_LORA = 160
GATE_LORA_PAD = 256
GN_EPS = 64e-5
NA_HEADS = 16
NA_WIDTH = NA_HEADS * HEAD_DIM
NA_WIN_ROWS = 8
NA_WIN_COLS = 16
SC_WIDTH = 1024
RW_COLS = 3 * RW_WIDTH + 2 * DECAY_LORA + 2 * ICLR_LORA + GATE_LORA
RW_COLS_PAD = 3 * RW_WIDTH + 2 * DECAY_LORA + 2 * ICLR_LORA + GATE_LORA_PAD
N_EXPERTS = 32
TOP_K = 4
D_EXPERT = 512
SWIGLU_ALPHA = 1.702
SWIGLU_LIMIT = 7.0
MOE_BLOCK = 256
ROPE_BASE = 10000.0
LN_EPS = 1e-6
NEG_INF = -1e30

LANE = 128
ROW_BLK = 256
MM_TM = 1024
MM_TN = 512
SCAN_TT = 16
SCAN_VSPLIT = 2
SCAN_NACC = 1
SCAN_RELAYOUT_UNROLL = 4
NA_GROUP = 4
NA_KROWS = 12
VMEM_LIMIT = 56 * 1024 * 1024

F32 = jnp.float32
BF16 = jnp.bfloat16


def _cparams(sem):
    return pltpu.CompilerParams(dimension_semantics=sem, vmem_limit_bytes=VMEM_LIMIT)


def _ln(x):
    mu = jnp.mean(x, axis=-1, keepdims=True)
    xc = x - mu
    var = jnp.mean(xc * xc, axis=-1, keepdims=True)
    return xc * lax.rsqrt(var + LN_EPS)


def _sigmoid(x):
    return 1.0 / (1.0 + jnp.exp(-x))


def _mod_kernel(c_ref, w_ref, b_ref, o_ref):
    c = c_ref[...]
    s = (c * _sigmoid(c)).astype(BF16)
    o_ref[0] = jnp.dot(s, w_ref[0].astype(BF16), preferred_element_type=F32) + b_ref[0]


def _modulation(cc, w_mod, b_mod):
    depth, d, n = w_mod.shape
    tn = 1024
    return pl.pallas_call(
        _mod_kernel,
        grid=(depth, n // tn),
        in_specs=[pl.BlockSpec((16, d), lambda l, j: (0, 0)),
                  pl.BlockSpec((1, d, tn), lambda l, j: (l, 0, j)),
                  pl.BlockSpec((1, 1, tn), lambda l, j: (l, 0, j))],
        out_specs=pl.BlockSpec((1, 16, tn), lambda l, j: (l, 0, j)),
        out_shape=jax.ShapeDtypeStruct((depth, 16, n), F32),
        compiler_params=_cparams(("arbitrary", "arbitrary")),
        name="modulation",
    )(cc, w_mod, b_mod.reshape(depth, 1, n))


def _mod_spec(chunk, d, nblk_per_batch):
    def imap(i):
        row = jnp.where(i % nblk_per_batch == 0, 8, i // nblk_per_batch)
        return (row, 0, chunk)
    return pl.BlockSpec((1, 1, d), imap)


def _lnmod_kernel(x_ref, sh_ref, sc_ref, o_ref):
    o_ref[...] = (_ln(x_ref[...]) * (1.0 + sc_ref[0]) + sh_ref[0]).astype(o_ref.dtype)


def _lnmod(x, mods3, nbb):
    n, d = x.shape
    return pl.pallas_call(
        _lnmod_kernel,
        grid=(n // ROW_BLK,),
        in_specs=[pl.BlockSpec((ROW_BLK, d), lambda i: (i, 0)),
                  _mod_spec(0, d, nbb), _mod_spec(1, d, nbb)],
        out_specs=pl.BlockSpec((ROW_BLK, d), lambda i: (i, 0)),
        out_shape=jax.ShapeDtypeStruct((n, d), BF16),
        compiler_params=_cparams(("parallel",)),
        name="ln_mod",
    )(x, mods3, mods3)


def _mm_kernel(a_ref, b_ref, o_ref):
    o_ref[...] = jnp.dot(a_ref[...], b_ref[...], preferred_element_type=F32).astype(o_ref.dtype)


def _matmul(a, b, out_dtype=F32):
    m, k = a.shape
    _, n = b.shape
    tm = MM_TM if m % MM_TM == 0 else ROW_BLK
    return pl.pallas_call(
        _mm_kernel,
        grid=(m // tm, n // MM_TN),
        in_specs=[pl.BlockSpec((tm, k), lambda i, j: (i, 0)),
                  pl.BlockSpec((k, MM_TN), lambda i, j: (0, j))],
        out_specs=pl.BlockSpec((tm, MM_TN), lambda i, j: (i, j)),
        out_shape=jax.ShapeDtypeStruct((m, n), out_dtype),
        compiler_params=_cparams(("parallel", "arbitrary")),
        name="in_proj",
    )(a, b)


def _halo_specs(width, col_blk, nblk):
    per = ROW_BLK // 8

    def prev_map(i):
        return (jnp.maximum(i * per - 1, 0), col_blk)

    def next_map(i):
        return (jnp.minimum((i + 1) * per, nblk * per - 1), col_blk)

    return (pl.BlockSpec((8, width), prev_map), pl.BlockSpec((8, width), next_map))


def _shifted(p, prev8, next8, i, nbb):
    pos = i % nbb
    has_prev = pos >= 2
    has_next = jnp.logical_and(pos >= 1, pos <= nbb - 2)
    rows = lax.broadcasted_iota(jnp.int32, p.shape, 0)
    prow = jnp.where(has_prev, prev8[7:8, :], 0.0)
    nrow = jnp.where(has_next, next8[0:1, :], 0.0)
    p_prev = jnp.where(rows == 0, prow, pltpu.roll(p, 1, 0))
    p_next = jnp.where(rows == p.shape[0] - 1, nrow, pltpu.roll(p, p.shape[0] - 1, 0))
    return p_prev, p_next


def _split3(x):
    hi = x.astype(BF16)
    r = x - hi.astype(F32)
    mid = r.astype(BF16)
    lo = (r - mid.astype(F32)).astype(BF16)
    return hi, mid, lo


def _dot_indicator(x, w):
    hi, mid, lo = _split3(x)
    return (jnp.dot(hi, w, preferred_element_type=F32) + jnp.dot(mid, w, preferred_element_type=F32)
            + jnp.dot(lo, w, preferred_element_type=F32))


def _dot_split(x, w_ref):
    hi, mid, _ = _split3(x)
    w_hi = w_ref[0]
    return (jnp.dot(hi, w_hi, preferred_element_type=F32) + jnp.dot(mid, w_hi, preferred_element_type=F32)
            + jnp.dot(hi, w_ref[1], preferred_element_type=F32))


def _split_weight(w):
    hi = w.astype(BF16)
    return jnp.stack([hi, (w - hi.astype(F32)).astype(BF16)])


def _seg_sum(x, e_ref, et_ref):
    return _dot_indicator(_dot_indicator(x, e_ref[...]), et_ref[...])


def _swap_halves(x):
    lanes = lax.broadcasted_iota(jnp.int32, x.shape, 1)
    first = (lanes % HEAD_DIM) < (HEAD_DIM // 2)
    n = x.shape[1]
    return jnp.where(first, pltpu.roll(x, n - HEAD_DIM // 2, 1), pltpu.roll(x, HEAD_DIM // 2, 1))


def _prep_kernel(nbb, p_ref, pp_ref, pn_ref, mu_ref, w0_ref, wl_ref, a0_ref, al_ref, gl_ref,
                 kk_ref, ka_ref, rk_ref, cos_ref, sin_ref, e_ref, et_ref,
                 r_o, kap_o, v_o, w_o, kd_o, b_o, bonus_o, g_o):
    i = pl.program_id(0)
    c = RW_WIDTH
    p = p_ref[...]
    p_prev, p_next = _shifted(p, pp_ref[...], pn_ref[...], i, nbb)
    pm = p + mu_ref[0:1, :] * (p_prev - p) + mu_ref[1:2, :] * (p_next - p)
    r, k, v = pm[:, :c], pm[:, c:2 * c], pm[:, 2 * c:3 * c]
    o = 3 * c
    wl = jnp.tanh(pm[:, o:o + 2 * DECAY_LORA])
    o += 2 * DECAY_LORA
    al = pm[:, o:o + 2 * ICLR_LORA]
    o += 2 * ICLR_LORA
    gl = _sigmoid(pm[:, o:o + GATE_LORA_PAD])
    g_o[...] = _dot_split(gl, gl_ref)
    kk = k * kk_ref[...]
    ss = _seg_sum(kk * kk, e_ref, et_ref)
    kk = kk / jnp.maximum(jnp.sqrt(ss), 1e-12)
    cos = jnp.concatenate([cos_ref[...]] * (c // LANE), axis=1)
    sin = jnp.concatenate([sin_ref[...]] * (c // LANE), axis=1)

    def rope(x):
        return x * cos + _swap_halves(x) * sin

    kap = rope(kk)
    r_o[...] = rope(r)
    kap_o[...] = kap
    v_o[...] = v
    k_sum = jnp.zeros_like(k)
    for d in range(2):
        w_raw = w0_ref[d:d + 1, :] + _dot_split(wl, wl_ref.at[d])
        w_o[d] = jnp.exp(-math.exp(-0.5) * _sigmoid(w_raw))
        a = _sigmoid(a0_ref[d:d + 1, :] + _dot_split(al, al_ref.at[d]))
        kd = k * (1.0 + (a - 1.0) * ka_ref[...])
        k_sum = k_sum + kd
        kd_o[d] = rope(kd)
        b_o[d] = kap * a
    bonus_o[...] = _seg_sum(r * k_sum * rk_ref[...], e_ref, et_ref) * v


def _rwkv_prepare(proj, lw, consts, nbb):
    n = proj.shape[0]
    c = RW_WIDTH
    nblk = n // ROW_BLK
    prev_spec, next_spec = _halo_specs(RW_COLS_PAD, 0, nblk)
    full = lambda shape: pl.BlockSpec(shape, lambda i: (0,) * len(shape))
    row_c = pl.BlockSpec((ROW_BLK, c), lambda i: (i, 0))
    row_2c = pl.BlockSpec((2, ROW_BLK, c), lambda i: (0, i, 0))
    tab = pl.BlockSpec((ROW_BLK, LANE), lambda i: (i % nbb, 0))
    outs = pl.pallas_call(
        functools.partial(_prep_kernel, nbb),
        grid=(nblk,),
        in_specs=[pl.BlockSpec((ROW_BLK, RW_COLS_PAD), lambda i: (i, 0)), prev_spec, next_spec,
                  full((2, RW_COLS_PAD)), full((2, c)), full((2, 2, 2 * DECAY_LORA, c)), full((2, c)),
                  full((2, 2, 2 * ICLR_LORA, c)), full((2, GATE_LORA_PAD, c)), full((1, c)), full((1, c)),
                  full((1, c)), tab, tab, full((c, LANE)), full((LANE, c))],
        out_specs=[row_c, row_c, row_c, row_2c, row_2c, row_2c, row_c, row_c],
        out_shape=[jax.ShapeDtypeStruct((n, c), F32)] * 3 + [jax.ShapeDtypeStruct((2, n, c), F32)] * 3
        + [jax.ShapeDtypeStruct((n, c), F32)] * 2,
        compiler_params=_cparams(("parallel",)),
        name="rwkv_prepare",
    )(proj, proj, proj, lw["mu"], lw["w0"], lw["w_lora"], lw["a0"], lw["a_lora"], lw["g_lora"],
      lw["k_k"], lw["k_a"], lw["r_k"], consts["cos"], consts["sin"], consts["seg"], consts["seg_t"])
    return outs


def _scan_kernel(rf_ref, kapf_ref, vf_ref, rb_ref, kapb_ref, vb_ref, wf_ref, kdf_ref, bf_ref, wb_ref, kdb_ref, bb_ref,
                 of_ref, ob_ref, s_ref, vec_ref, out_ref):
    nb, tt = rf_ref.shape[0], rf_ref.shape[1]
    n_pair = rf_ref.shape[2] // (2 * HEAD_DIM)
    nl = nb * n_pair
    vh = HEAD_DIM // SCAN_VSPLIT

    @pl.when(pl.program_id(0) == 0)
    def _():
        s_ref[...] = jnp.zeros_like(s_ref)

    def to_lanes(i, carry):
        slab = lambda x: jnp.concatenate([x[:, p * 2 * HEAD_DIM:(p + 1) * 2 * HEAD_DIM] for p in range(n_pair)], axis=0)
        tb = tt - 1 - i
        pairs = ((kapf_ref[:, i, :], kapb_ref[:, tb, :]), (rf_ref[:, i, :], rb_ref[:, tb, :]),
                 (wf_ref[0, :, i, :], wb_ref[0, :, tb, :]), (kdf_ref[0, :, i, :], kdb_ref[0, :, tb, :]),
                 (bf_ref[0, :, i, :], bb_ref[0, :, tb, :]), (vf_ref[:, i, :], vb_ref[:, tb, :]))
        for j, (xf, xb) in enumerate(pairs):
            res = jnp.concatenate([slab(xf), slab(xb)], axis=0).T
            vec_ref[0, i, j] = res[:HEAD_DIM]
            vec_ref[1, i, j] = res[HEAD_DIM:]
        return carry

    lax.fori_loop(0, tt, to_lanes, 0, unroll=SCAN_RELAYOUT_UNROLL)

    def add_to(acc, j, term):
        acc[j] = term if acc[j] is None else acc[j] + term

    def total(acc):
        return functools.reduce(lambda a, b: a + b, acc)

    skk0 = []
    for g in range(2):
        acc = [None] * SCAN_NACC
        for k in range(HEAD_DIM):
            add_to(acc, k % SCAN_NACC, s_ref[g, k] * vec_ref[g, 0, 0, k:k + 1, :])
        skk0.append(total(acc))

    def step(i, skks):
        i_next = jnp.minimum(i + 1, tt - 1)
        skks_next = []
        for g in range(2):
            parts = []
            for h in range(SCAN_VSPLIT):
                rows = slice(h * vh, (h + 1) * vh)
                skk_h = skks[g][rows]
                v = vec_ref[g, i, 5, rows]
                acc_out = [None] * SCAN_NACC
                acc_next = [None] * SCAN_NACC
                for k in range(HEAD_DIM):
                    s_new = (s_ref[g, k, rows] * vec_ref[g, i, 2, k:k + 1, :] - skk_h * vec_ref[g, i, 4, k:k + 1, :]
                             + v * vec_ref[g, i, 3, k:k + 1, :])
                    s_ref[g, k, rows] = s_new
                    add_to(acc_out, k % SCAN_NACC, s_new * vec_ref[g, i, 1, k:k + 1, :])
                    add_to(acc_next, k % SCAN_NACC, s_new * vec_ref[g, i_next, 0, k:k + 1, :])
                out_ref[g, i, rows] = total(acc_out)
                parts.append(total(acc_next))
            skks_next.append(jnp.concatenate(parts, axis=0))
        return tuple(skks_next)

    lax.fori_loop(0, tt, step, tuple(skk0))

    def to_tokens(i, carry):
        back = jnp.concatenate([out_ref[0, i], out_ref[1, i]], axis=0).T
        unslab = lambda x: jnp.concatenate([x[p * nb:(p + 1) * nb] for p in range(n_pair)], axis=1)
        of_ref[:, i, :] = unslab(back[:nl])
        ob_ref[:, tt - 1 - i, :] = unslab(back[nl:])
        return carry

    lax.fori_loop(0, tt, to_tokens, 0, unroll=SCAN_RELAYOUT_UNROLL)


def _wkv_scan(r, kap, v, w, kd, b, ctx_len):
    batch, t, width = r.shape
    lanes = batch * width // HEAD_DIM
    nb = t // SCAN_TT
    n_ctx_blk = ctx_len // SCAN_TT

    def mirrored(j):
        return jnp.where(j < n_ctx_blk, n_ctx_blk - 1 - j, nb - 1 - (j - n_ctx_blk))

    fwd = pl.BlockSpec((batch, SCAN_TT, width), lambda j: (0, j, 0))
    bwd = pl.BlockSpec((batch, SCAN_TT, width), lambda j: (0, mirrored(j), 0))
    fwd_dir = pl.BlockSpec((1, batch, SCAN_TT, width), lambda j: (0, 0, j, 0))
    bwd_dir = pl.BlockSpec((1, batch, SCAN_TT, width), lambda j: (1, 0, mirrored(j), 0))
    out = jax.ShapeDtypeStruct((batch, t, width), F32)
    return pl.pallas_call(
        _scan_kernel,
        grid=(nb,),
        in_specs=[fwd, fwd, fwd, bwd, bwd, bwd, fwd_dir, fwd_dir, fwd_dir, bwd_dir, bwd_dir, bwd_dir],
        out_specs=[fwd, bwd],
        out_shape=[out, out],
        scratch_shapes=[pltpu.VMEM((2, HEAD_DIM, HEAD_DIM, lanes), F32),
                        pltpu.VMEM((2, SCAN_TT, 6, HEAD_DIM, lanes), F32),
                        pltpu.VMEM((2, SCAN_TT, HEAD_DIM, lanes), F32)],
        compiler_params=_cparams(("arbitrary",)),
        name="wkv_scan",
    )(r, kap, v, r, kap, v, w, kd, b, w, kd, b)


def _rwkv_out_kernel(wkv_f_ref, wkv_b_ref, bonus_ref, g_ref, gw_ref, gb_ref, e_ref, et_ref, o_ref):
    x = wkv_f_ref[...] + wkv_b_ref[...]
    inv_n = 1.0 / HEAD_DIM
    mu = _seg_sum(x, e_ref, et_ref) * inv_n
    xc = x - mu
    var = _seg_sum(xc * xc, e_ref, et_ref) * inv_n
    y = xc * lax.rsqrt(var + GN_EPS) * gw_ref[...] + gb_ref[...]
    o_ref[...] = ((y + bonus_ref[...]) * g_ref[...]).astype(o_ref.dtype)


def _rwkv_out(wkv_f, wkv_b, bonus, g, lw, consts):
    n, c = bonus.shape
    full = lambda shape: pl.BlockSpec(shape, lambda i: (0,) * len(shape))
    row_c = pl.BlockSpec((ROW_BLK, c), lambda i: (i, 0))
    return pl.pallas_call(
        _rwkv_out_kernel,
        grid=(n // ROW_BLK,),
        in_specs=[row_c, row_c, row_c, row_c,
                  full((1, c)), full((1, c)), full((c, LANE)), full((LANE, c))],
        out_specs=row_c,
        out_shape=jax.ShapeDtypeStruct((n, c), BF16),
        compiler_params=_cparams(("parallel",)),
        name="rwkv_out",
    )(wkv_f, wkv_b, bonus, g, lw["gn_w"], lw["gn_b"], consts["seg"], consts["seg_t"])


def _conv_kernel(nbb, bg_ref, cg_ref, x_ref, cgp_ref, xp_ref, cgn_ref, xn_ref, w_ref, o_ref):
    i = pl.program_id(0)
    u = cg_ref[...] * x_ref[...]
    u_prev, u_next = _shifted(u, cgp_ref[...] * xp_ref[...], cgn_ref[...] * xn_ref[...], i, nbb)
    y = w_ref[0:1, :] * u_prev + w_ref[1:2, :] * u + w_ref[2:3, :] * u_next
    o_ref[...] = (bg_ref[...] * y).astype(o_ref.dtype)


def _short_conv(proj, conv_w, col0, nbb):
    n = proj.shape[0]
    cw = MM_TN
    nc = SC_WIDTH // cw
    assert col0 % cw == 0
    cb = col0 // cw
    nblk = n // ROW_BLK
    blk = lambda s: pl.BlockSpec((ROW_BLK, cw), lambda i, j: (i, cb + s * nc + j))

    def halos(s):
        prev_spec, next_spec = _halo_specs(cw, 0, nblk)
        pm, nm = prev_spec.index_map, next_spec.index_map
        return (pl.BlockSpec((8, cw), lambda i, j: (pm(i)[0], cb + s * nc + j)),
                pl.BlockSpec((8, cw), lambda i, j: (nm(i)[0], cb + s * nc + j)))

    cg_prev, cg_next = halos(1)
    x_prev, x_next = halos(2)
    return pl.pallas_call(
        functools.partial(_conv_kernel, nbb),
        grid=(nblk, nc),
        in_specs=[blk(0), blk(1), blk(2), cg_prev, x_prev, cg_next, x_next,
                  pl.BlockSpec((8, cw), lambda i, j: (0, j))],
        out_specs=pl.BlockSpec((ROW_BLK, cw), lambda i, j: (i, j)),
        out_shape=jax.ShapeDtypeStruct((n, SC_WIDTH), BF16),
        compiler_params=_cparams(("parallel", "arbitrary")),
        name="short_conv",
    )(proj, proj, proj, proj, proj, proj, proj, conv_w)


def _na_tables(rows):
    kr = min(NA_WIN_ROWS, rows)
    n_groups = rows // NA_GROUP
    krows = min(NA_KROWS, rows)
    n_dr = 2 * NA_WIN_ROWS - 1
    dr = np.full((n_groups, NA_GROUP, krows), n_dr, np.int32)
    bases = []
    for g in range(n_groups):
        r0 = g * NA_GROUP
        base = int(np.clip(np.clip(r0 - kr // 2, 0, rows - kr), 0, rows - krows))
        bases.append(base)
        for rl in range(NA_GROUP):
            r = r0 + rl
            rs = int(np.clip(r - kr // 2, 0, rows - kr))
            for j in range(krows):
                if rs <= base + j < rs + kr:
                    dr[g, rl, j] = base + j - r + NA_WIN_ROWS - 1
            assert (dr[g, rl] < n_dr).sum() == kr
    uniq, table_of = [], []
    for g in range(n_groups):
        for u, gu in enumerate(uniq):
            if np.array_equal(dr[g], dr[gu]):
                table_of.append(u)
                break
        else:
            table_of.append(len(uniq))
            uniq.append(g)
    return dr[uniq], tuple(bases), tuple(table_of)


def _na_kernel(ctx_len, dr, bases, table_of, q_ref, k_ref, v_ref, bias_ref, o_ref):
    scale = HEAD_DIM ** -0.5
    nq = NA_GROUP * GRID_W
    krows = dr.shape[2]
    nk = krows * GRID_W
    first = lax.broadcasted_iota(jnp.int32, (GRID_W, 2 * GRID_W), 1) < GRID_W

    def bias_of(table, h):
        rows = []
        for rl in range(NA_GROUP):
            tiles = [jnp.where(first, bias_ref[h, int(dr[table, rl, j])], bias_ref[h, int(dr[table, rl, j + 1])])
                     for j in range(0, krows, 2)]
            rows.append(jnp.concatenate(tiles, axis=1))
        return jnp.concatenate(rows, axis=0)

    outs_heads = []
    for h in range(2):
        sl = slice(h * HEAD_DIM, (h + 1) * HEAD_DIM)
        q = (q_ref[:, sl] * scale).astype(BF16)
        k = k_ref[:, sl].astype(BF16)
        v = v_ref[:, sl].astype(BF16)
        kc, vc = k[:ctx_len], v[:ctx_len]
        dn = (((1,), (1,)), ((), ()))
        s = lax.dot_general(q[:ctx_len], kc, dn, preferred_element_type=F32)
        s = s - jnp.max(s, axis=-1, keepdims=True)
        e = jnp.exp(s)
        p = (e / jnp.sum(e, axis=-1, keepdims=True)).astype(BF16)
        pieces = [jnp.dot(p, vc, preferred_element_type=F32)]
        biases = [bias_of(t, h) for t in range(dr.shape[0])]
        for g, base in enumerate(bases):
            q0 = ctx_len + g * nq
            k0 = ctx_len + base * GRID_W
            qg = q[q0:q0 + nq]
            s_win = (lax.dot_general(qg, k[k0:k0 + nk], dn, preferred_element_type=F32)
                     + biases[table_of[g]])
            s_ctx = lax.dot_general(qg, kc, dn, preferred_element_type=F32)
            m = jnp.maximum(jnp.max(s_win, axis=-1, keepdims=True), jnp.max(s_ctx, axis=-1, keepdims=True))
            e_win = jnp.exp(s_win - m)
            e_ctx = jnp.exp(s_ctx - m)
            inv = 1.0 / (jnp.sum(e_win, axis=-1, keepdims=True) + jnp.sum(e_ctx, axis=-1, keepdims=True))
            acc = jnp.dot((e_win * inv).astype(BF16), v[k0:k0 + nk], preferred_element_type=F32)
            acc = acc + jnp.dot((e_ctx * inv).astype(BF16), vc, preferred_element_type=F32)
            pieces.append(acc)
        outs_heads.append(jnp.concatenate(pieces, axis=0))
    o_ref[...] = jnp.concatenate(outs_heads, axis=1).astype(o_ref.dtype)


def _na_bias(rpb):
    qc = np.arange(GRID_W)[:, None]
    kc = np.arange(GRID_W)[None, :]
    wstart = np.clip(qc - NA_WIN_COLS // 2, 0, GRID_W - NA_WIN_COLS)
    col_ok = (kc >= wstart) & (kc < wstart + NA_WIN_COLS)
    dc = np.clip(kc - qc + NA_WIN_COLS - 1, 0, 2 * NA_WIN_COLS - 2)
    blocks = jnp.where(col_ok, rpb[:, :, dc], NEG_INF)
    blocks = jnp.concatenate([blocks, jnp.full_like(blocks[:, :1], NEG_INF)], axis=1)
    return jnp.concatenate([blocks, blocks], axis=-1)


def _na_attention(proj, bias, tables, col0, batch, t_len, ctx_len):
    n = proj.shape[0]
    cb = col0 // LANE
    hp = NA_HEADS // 2
    seq = lambda j: pl.BlockSpec((t_len, LANE), lambda h, b: (b, cb + j * hp + h))
    return pl.pallas_call(
        functools.partial(_na_kernel, ctx_len, *tables),
        grid=(hp, batch),
        in_specs=[seq(0), seq(1), seq(2),
                  pl.BlockSpec((2,) + bias.shape[1:], lambda h, b: (h, 0, 0, 0))],
        out_specs=pl.BlockSpec((t_len, LANE), lambda h, b: (b, h)),
        out_shape=jax.ShapeDtypeStruct((n, NA_WIDTH), BF16),
        compiler_params=_cparams(("arbitrary", "arbitrary")),
        name="na_attention",
    )(proj, proj, proj, bias)


def _merge1_kernel(yr_ref, yn_ref, ys_ref, wr_ref, wn_ref, ws_ref, g0_ref, g1_ref, g2_ref,
                   b0_ref, b1_ref, b2_ref, o_ref):
    m = _sigmoid(g0_ref[...] + b0_ref[...]) * jnp.dot(yr_ref[...], wr_ref[...], preferred_element_type=F32)
    m = m + _sigmoid(g1_ref[...] + b1_ref[...]) * jnp.dot(yn_ref[...], wn_ref[...], preferred_element_type=F32)
    m = m + _sigmoid(g2_ref[...] + b2_ref[...]) * jnp.dot(ys_ref[...], ws_ref[...], preferred_element_type=F32)
    o_ref[...] = m.astype(o_ref.dtype)


def _merge1(y_rw, y_na, y_sc, lw, proj, gate_col0, d):
    n, c = y_rw.shape
    tm, tn = 512, MM_TN
    gb = gate_col0 // tn
    nd = d // tn
    ysp = pl.BlockSpec((tm, c), lambda i, j: (i, 0))
    wsp = pl.BlockSpec((c, tn), lambda i, j: (0, j))
    gsp = lambda br: pl.BlockSpec((tm, tn), lambda i, j: (i, gb + br * nd + j))
    bsp = lambda br: pl.BlockSpec((1, tn), lambda i, j: (0, br * nd + j))
    return pl.pallas_call(
        _merge1_kernel,
        grid=(n // tm, nd),
        in_specs=[ysp, ysp, ysp, wsp, wsp, wsp, gsp(0), gsp(1), gsp(2), bsp(0), bsp(1), bsp(2)],
        out_specs=pl.BlockSpec((tm, tn), lambda i, j: (i, j)),
        out_shape=jax.ShapeDtypeStruct((n, d), BF16),
        compiler_params=_cparams(("parallel", "arbitrary")),
        name="merge_branches",
    )(y_rw, y_na, y_sc, lw["w_out_rw"], lw["w_out_na"], lw["w_out_sc"], proj, proj, proj,
      lw["b_gate"], lw["b_gate"], lw["b_gate"])


def _merge2_kernel(alpha, m_ref, w_ref, x_ref, g1_ref, lg_ref, lb_ref, sh_ref, sc_ref, rw_ref, rb_ref,
                   x1_ref, h2_ref, lo_ref):
    mix = jnp.dot(m_ref[...], w_ref[...], preferred_element_type=F32)
    x1 = _ln(alpha * x_ref[...] + g1_ref[0] * mix) * lg_ref[...] + lb_ref[...]
    x1_ref[...] = x1
    h2 = _ln(x1) * (1.0 + sc_ref[0]) + sh_ref[0]
    h2_ref[...] = h2.astype(h2_ref.dtype)
    lo_ref[...] = _dot_split(h2, rw_ref) + rb_ref[...]


def _merge2(m, x, mods3, lw, nbb, alpha):
    n, d = x.shape
    full = lambda shape: pl.BlockSpec(shape, lambda i: (0,) * len(shape))
    row = pl.BlockSpec((ROW_BLK, d), lambda i: (i, 0))
    return pl.pallas_call(
        functools.partial(_merge2_kernel, alpha),
        grid=(n // ROW_BLK,),
        in_specs=[row, full((d, d)), row, _mod_spec(2, d, nbb), full((1, d)), full((1, d)),
                  _mod_spec(3, d, nbb), _mod_spec(4, d, nbb), full((2, d, LANE)), full((1, LANE))],
        out_specs=[row, row, pl.BlockSpec((ROW_BLK, LANE), lambda i: (i, 0))],
        out_shape=[jax.ShapeDtypeStruct((n, d), F32), jax.ShapeDtypeStruct((n, d), BF16),
                   jax.ShapeDtypeStruct((n, LANE), F32)],
        compiler_params=_cparams(("parallel",)),
        name="merge_out_ln",
    )(m, lw["w_merge"], x, mods3, lw["ln1_g"], lw["ln1_b"], mods3, mods3, lw["router_w"], lw["router_b"])


def _route_kernel(lo_ref, gate_ref, idx_ref, rank_ref, counts_ref, carry_ref):
    x = lo_ref[...]
    lanes = lax.broadcasted_iota(jnp.int32, x.shape, 1).astype(F32)
    x = jnp.where(lanes < N_EXPERTS, x, -jnp.inf)
    vals, idxs = [], []
    for _ in range(TOP_K):
        m = jnp.max(x, axis=-1, keepdims=True)
        sel = jnp.min(jnp.where(x == m, lanes, float(LANE)), axis=-1, keepdims=True)
        vals.append(m)
        idxs.append(sel)
        x = jnp.where(lanes == sel, -jnp.inf, x)
    es = [jnp.exp(v - vals[0]) for v in vals]
    tot = es[0] + es[1] + es[2] + es[3]
    @pl.when(pl.program_id(0) == 0)
    def _():
        carry_ref[...] = jnp.zeros_like(carry_ref)

    onehot = jnp.zeros(lo_ref.shape, F32)
    for j in range(TOP_K):
        onehot = onehot + jnp.where(lanes == idxs[j], 1.0, 0.0)
    nr = lo_ref.shape[0]
    below = (lax.broadcasted_iota(jnp.int32, (nr, nr), 1) < lax.broadcasted_iota(jnp.int32, (nr, nr), 0))
    prefix = jnp.dot(below.astype(BF16), onehot.astype(BF16), preferred_element_type=F32) + carry_ref[...]
    gate = jnp.zeros(lo_ref.shape, F32)
    idx = jnp.zeros(lo_ref.shape, F32)
    rank = jnp.zeros(lo_ref.shape, F32)
    for j in range(TOP_K):
        gate = jnp.where(lanes == j, es[j] / tot, gate)
        idx = jnp.where(lanes == j, idxs[j], idx)
        rank_j = jnp.sum(jnp.where(lanes == idxs[j], prefix, 0.0), axis=-1, keepdims=True)
        rank = jnp.where(lanes == j, rank_j, rank)
    gate_ref[...] = gate
    idx_ref[...] = idx.astype(jnp.int32)
    rank_ref[...] = rank.astype(jnp.int32)
    carry_ref[...] = carry_ref[...] + jnp.sum(onehot, axis=0, keepdims=True)
    counts_ref[...] = carry_ref[...].astype(jnp.int32)


def _route(logits):
    n = logits.shape[0]
    row = pl.BlockSpec((ROW_BLK, LANE), lambda i: (i, 0))
    return pl.pallas_call(
        _route_kernel,
        grid=(n // ROW_BLK,),
        in_specs=[row],
        out_specs=[row, row, row, pl.BlockSpec((1, LANE), lambda i: (0, 0))],
        out_shape=[jax.ShapeDtypeStruct((n, LANE), F32), jax.ShapeDtypeStruct((n, LANE), jnp.int32),
                   jax.ShapeDtypeStruct((n, LANE), jnp.int32), jax.ShapeDtypeStruct((1, LANE), jnp.int32)],
        scratch_shapes=[pltpu.VMEM((1, LANE), F32)],
        compiler_params=_cparams(("arbitrary",)),
        name="moe_route",
    )(logits)


def _expert_kernel(be_ref, x_ref, w1_ref, b1_ref, w2_ref, b2_ref, sel_ref, o_ref, w1b_ref, w2b_ref):
    i = pl.program_id(0)

    @pl.when(jnp.logical_or(i == 0, be_ref[i] != be_ref[jnp.maximum(i - 1, 0)]))
    def _():
        w1b_ref[...] = w1_ref[0].astype(BF16)
        w2b_ref[...] = w2_ref[0].astype(BF16)

    z = jnp.dot(x_ref[...], w1b_ref[...], preferred_element_type=F32) + b1_ref[0]
    z_glu = jnp.minimum(z, SWIGLU_LIMIT)
    z_lin = jnp.clip(pltpu.roll(z, z.shape[1] - 1, 1), -SWIGLU_LIMIT, SWIGLU_LIMIT)
    act = z_glu * _sigmoid(SWIGLU_ALPHA * z_glu) * (z_lin + 1.0)
    lanes = lax.broadcasted_iota(jnp.int32, act.shape, 1)
    act = jnp.where(lanes % 2 == 0, act, 0.0).astype(BF16)
    act = jnp.dot(act, sel_ref[...], preferred_element_type=F32).astype(BF16)
    o_ref[...] = jnp.dot(act, w2b_ref[...], preferred_element_type=F32) + b2_ref[0]


def _experts(xs, block_e, lw):
    n_slots, d = xs.shape
    n_blocks = n_slots // MOE_BLOCK
    f2 = 2 * D_EXPERT
    sel = jnp.asarray(np.arange(f2)[:, None] == 2 * np.arange(D_EXPERT)[None, :], BF16)
    grid_spec = pltpu.PrefetchScalarGridSpec(
        num_scalar_prefetch=1,
        grid=(n_blocks,),
        in_specs=[pl.BlockSpec((MOE_BLOCK, d), lambda i, be: (i, 0)),
                  pl.BlockSpec((1, d, f2), lambda i, be: (be[i], 0, 0)),
                  pl.BlockSpec((1, 1, f2), lambda i, be: (be[i], 0, 0)),
                  pl.BlockSpec((1, D_EXPERT, d), lambda i, be: (be[i], 0, 0)),
                  pl.BlockSpec((1, 1, d), lambda i, be: (be[i], 0, 0)),
                  pl.BlockSpec((f2, D_EXPERT), lambda i, be: (0, 0))],
        out_specs=pl.BlockSpec((MOE_BLOCK, d), lambda i, be: (i, 0)),
        scratch_shapes=[pltpu.VMEM((d, f2), BF16), pltpu.VMEM((D_EXPERT, d), BF16)],
    )
    return pl.pallas_call(
        _expert_kernel,
        grid_spec=grid_spec,
        out_shape=jax.ShapeDtypeStruct((n_slots, d), F32),
        compiler_params=_cparams(("arbitrary",)),
        name="moe_experts",
    )(block_e, xs, lw["moe_w1"], lw["moe_b1"], lw["moe_w2"], lw["moe_b2"], sel)


def _moe(h2, idx, rank, counts, lw):
    n, d = h2.shape
    a = n * TOP_K
    counts = counts[0, :N_EXPERTS]
    padded = (counts + MOE_BLOCK - 1) // MOE_BLOCK * MOE_BLOCK
    pad_end = jnp.cumsum(padded)
    pad_start = pad_end - padded
    dest = pad_start[idx[:, :TOP_K]] + rank[:, :TOP_K]
    n_blocks = -(-a // MOE_BLOCK) + N_EXPERTS
    n_slots = n_blocks * MOE_BLOCK
    slot_tok = jnp.zeros(n_slots, jnp.int32).at[dest.reshape(a)].set(jnp.arange(a, dtype=jnp.int32) // TOP_K)
    block_start = jnp.arange(n_blocks, dtype=jnp.int32) * MOE_BLOCK
    block_e = jnp.minimum(jnp.sum(block_start[:, None] >= pad_end[None, :], axis=1), N_EXPERTS - 1).astype(jnp.int32)
    xs = h2[slot_tok]
    ys = _experts(xs, block_e + lw["moe_row0"], lw)
    return [ys[dest[:, j]] for j in range(TOP_K)]


def _final_kernel(alpha, x_ref, y0_ref, y1_ref, y2_ref, y3_ref, gate_ref, g2_ref, lg_ref, lb_ref, sh_ref, sc_ref,
                  x2_ref, h_ref):
    gate = gate_ref[...]
    y = y0_ref[...] * gate[:, 0:1]
    for j, y_ref in enumerate((y1_ref, y2_ref, y3_ref), start=1):
        y = y + y_ref[...] * gate[:, j:j + 1]
    x2 = _ln(alpha * x_ref[...] + g2_ref[0] * y) * lg_ref[...] + lb_ref[...]
    x2_ref[...] = x2
    h_ref[...] = (_ln(x2) * (1.0 + sc_ref[0]) + sh_ref[0]).astype(h_ref.dtype)


def _final(x1, y_rows, gate, mods3, mods3_next, lw, nbb, alpha):
    n, d = x1.shape
    assert len(y_rows) == TOP_K == 4
    full = lambda shape: pl.BlockSpec(shape, lambda i: (0,) * len(shape))
    row = pl.BlockSpec((ROW_BLK, d), lambda i: (i, 0))
    return pl.pallas_call(
        functools.partial(_final_kernel, alpha),
        grid=(n // ROW_BLK,),
        in_specs=[row, row, row, row, row, pl.BlockSpec((ROW_BLK, LANE), lambda i: (i, 0)),
                  _mod_spec(5, d, nbb), full((1, d)), full((1, d)), _mod_spec(0, d, nbb), _mod_spec(1, d, nbb)],
        out_specs=[row, row],
        out_shape=[jax.ShapeDtypeStruct((n, d), F32), jax.ShapeDtypeStruct((n, d), BF16)],
        compiler_params=_cparams(("parallel",)),
        name="moe_combine_residual_ln",
    )(x1, *y_rows, gate, mods3, lw["ln2_g"], lw["ln2_b"], mods3_next, mods3_next)


def _rope_tables(ctx_len, seq):
    n_freq = HEAD_DIM // 4
    t = np.arange(seq)
    inv = np.power(ROPE_BASE, -np.arange(n_freq, dtype=np.float32) / n_freq).astype(np.float32)
    row = (t // GRID_W).astype(np.float32)
    col = (t % GRID_W).astype(np.float32)
    ang = jnp.asarray(np.concatenate([row[:, None] * inv, col[:, None] * inv], -1))
    cos = jnp.cos(ang)
    sin = jnp.sin(ang)
    cos_h = jnp.concatenate([cos, cos], -1)
    sin_h = jnp.concatenate([-sin, sin], -1)
    cos_t = jnp.concatenate([jnp.ones((ctx_len, HEAD_DIM), F32), cos_h], 0)
    sin_t = jnp.concatenate([jnp.zeros((ctx_len, HEAD_DIM), F32), sin_h], 0)
    return jnp.tile(cos_t, (1, LANE // HEAD_DIM)), jnp.tile(sin_t, (1, LANE // HEAD_DIM))


def _to_scan_layout(x, batch, t_len):
    return x.reshape(x.shape[:-2] + (batch, t_len, x.shape[-1]))


def _pad_dir_lora(w):
    z = jnp.zeros_like(w[0])
    return jnp.stack([_split_weight(jnp.concatenate([w[0], z], 0)), _split_weight(jnp.concatenate([z, w[1]], 0))])


def kernel(x, c, ctx, c_ctx, w_mod, b_mod, w_in, rw_mu, rw_w0, rw_w_lora, rw_a0, rw_a_lora, rw_g_lora, rw_k_k, rw_k_a, rw_r_k, rw_gn_w, rw_gn_b, w_out_rw, na_rpb, w_out_na, sc_conv, w_out_sc, b_gate, w_merge, ln1_g, ln1_b, router_w, router_b, moe_w1, moe_b1, moe_w2, moe_b2, ln2_g, ln2_b):
    batch, seq, d = x.shape
    ctx_len = ctx.shape[1]
    depth = w_in.shape[0]
    t_len = ctx_len + seq
    n = batch * t_len
    nbb = t_len // ROW_BLK
    assert batch <= 8 and ctx_len == ROW_BLK and seq % (NA_GROUP * GRID_W) == 0
    alpha = (2 * depth) ** 0.25

    cc = jnp.zeros((16, d), F32).at[:batch].set(c).at[8].set(c_ctx)
    mods = _modulation(cc, w_mod, b_mod)
    mods3 = [mods[l].reshape(16, 1, 6 * d) for l in range(depth)]

    cos_t, sin_t = _rope_tables(ctx_len, seq)
    seg = (np.arange(RW_WIDTH)[:, None] // HEAD_DIM == np.arange(LANE)[None, :]).astype(np.float32)
    consts = dict(cos=cos_t, sin=sin_t, seg=jnp.asarray(seg, BF16), seg_t=jnp.asarray(seg.T, BF16))
    tables = _na_tables(seq // GRID_W)

    o_na = RW_COLS_PAD
    o_sc = o_na + 3 * NA_WIDTH
    o_gate = o_sc + 3 * SC_WIDTH
    pad_cols = RW_COLS_PAD - RW_COLS

    xs = jnp.concatenate([ctx, x], axis=1).reshape(n, d)
    h = _lnmod(xs, mods3[0], nbb)
    for l in range(depth):
        w_in_p = jnp.concatenate([w_in[l, :, :RW_COLS], jnp.zeros((d, pad_cols), F32), w_in[l, :, RW_COLS:]],
                                 axis=1).astype(BF16)
        lw = dict(
            mu=jnp.pad(rw_mu[l], ((0, 0), (0, pad_cols))), w0=rw_w0[l], w_lora=_pad_dir_lora(rw_w_lora[l]),
            a0=rw_a0[l], a_lora=_pad_dir_lora(rw_a_lora[l]),
            g_lora=_split_weight(jnp.pad(rw_g_lora[l], ((0, GATE_LORA_PAD - GATE_LORA), (0, 0)))),
            k_k=rw_k_k[l].reshape(1, -1), k_a=rw_k_a[l].reshape(1, -1), r_k=rw_r_k[l].reshape(1, -1),
            gn_w=rw_gn_w[l].reshape(1, -1), gn_b=rw_gn_b[l].reshape(1, -1),
            w_out_rw=w_out_rw[l].astype(BF16), w_out_na=w_out_na[l].astype(BF16),
            w_out_sc=w_out_sc[l].astype(BF16), b_gate=b_gate[l].reshape(1, -1),
            w_merge=w_merge[l].astype(BF16), ln1_g=ln1_g[l].reshape(1, -1), ln1_b=ln1_b[l].reshape(1, -1),
            router_w=_split_weight(jnp.pad(router_w[l], ((0, 0), (0, LANE - N_EXPERTS)))),
            router_b=jnp.pad(router_b[l], (0, LANE - N_EXPERTS)).reshape(1, -1),
            moe_row0=l * N_EXPERTS,
            moe_w1=moe_w1.reshape((depth * N_EXPERTS,) + moe_w1.shape[2:]),
            moe_b1=moe_b1.reshape(depth * N_EXPERTS, 1, -1),
            moe_w2=moe_w2.reshape((depth * N_EXPERTS,) + moe_w2.shape[2:]),
            moe_b2=moe_b2.reshape(depth * N_EXPERTS, 1, -1),
            ln2_g=ln2_g[l].reshape(1, -1), ln2_b=ln2_b[l].reshape(1, -1),
        )
        proj = _matmul(h, w_in_p)
        r_s, kap, v, w, kd, b, bonus, g = _rwkv_prepare(proj, lw, consts, nbb)
        tl = functools.partial(_to_scan_layout, batch=batch, t_len=t_len)
        wkv_f, wkv_b = _wkv_scan(tl(r_s), tl(kap), tl(v), tl(w), tl(kd), tl(b), ctx_len)
        y_rw = _rwkv_out(wkv_f.reshape(n, RW_WIDTH), wkv_b.reshape(n, RW_WIDTH), bonus, g, lw, consts)
        y_na = _na_attention(proj, _na_bias(na_rpb[l]), tables, o_na, batch, t_len, ctx_len)
        y_sc = _short_conv(proj, jnp.pad(sc_conv[l], ((0, 5), (0, 0))), o_sc, nbb)
        m = _merge1(y_rw, y_na, y_sc, lw, proj, o_gate, d)
        x1, h2, logits = _merge2(m, xs, mods3[l], lw, nbb, alpha)
        gate, idx, rank, counts = _route(logits)
        y_rows = _moe(h2, idx, rank, counts, lw)
        xs, h = _final(x1, y_rows, gate, mods3[l], mods3[min(l + 1, depth - 1)], lw, nbb, alpha)
    return xs.reshape(batch, t_len, d)[:, ctx_len:]
```

```python
import functools
import math

import numpy as np
import jax
import jax.numpy as jnp
from jax import lax
from jax.experimental import pallas as pl
from jax.experimental.pallas import tpu as pltpu

GRID_W = 64
RW_HEADS = 16
HEAD_DIM = 64
RW_WIDTH = RW_HEADS * HEAD_DIM
DECAY_LORA = 64
ICLR_LORA = 64
GATE_LORA = 160
GATE_LORA_PAD = 256
GN_EPS = 64e-5
NA_HEADS = 16
NA_WIDTH = NA_HEADS * HEAD_DIM
NA_WIN_ROWS = 8
NA_WIN_COLS = 16
SC_WIDTH = 1024
RW_COLS = 3 * RW_WIDTH + 2 * DECAY_LORA + 2 * ICLR_LORA + GATE_LORA
RW_COLS_PAD = 3 * RW_WIDTH + 2 * DECAY_LORA + 2 * ICLR_LORA + GATE_LORA_PAD
N_EXPERTS = 32
TOP_K = 4
D_EXPERT = 512
SWIGLU_ALPHA = 1.702
SWIGLU_LIMIT = 7.0
MOE_BLOCK = 256
ROPE_BASE = 10000.0
LN_EPS = 1e-6
NEG_INF = -1e30

LANE = 128
ROW_BLK = 256
MM_TM = 2048
MM_TN = 512
SCAN_TT = 16
SCAN_VSPLIT = 2
SCAN_NACC = 1
SCAN_RELAYOUT_UNROLL = 4
NA_GROUP = 4
NA_KROWS = 12
VMEM_LIMIT = 56 * 1024 * 1024

F32 = jnp.float32
BF16 = jnp.bfloat16


def _cparams(sem):
    return pltpu.CompilerParams(dimension_semantics=sem, vmem_limit_bytes=VMEM_LIMIT)


def _ln(x):
    mu = jnp.mean(x, axis=-1, keepdims=True)
    xc = x - mu
    var = jnp.mean(xc * xc, axis=-1, keepdims=True)
    return xc * lax.rsqrt(var + LN_EPS)


def _sigmoid(x):
    return 1.0 / (1.0 + jnp.exp(-x))


def _mod_kernel(c_ref, w_ref, b_ref, o_ref):
    c = c_ref[...]
    s = (c * _sigmoid(c)).astype(BF16)
    o_ref[0] = jnp.dot(s, w_ref[0].astype(BF16), preferred_element_type=F32) + b_ref[0]


def _modulation(cc, w_mod, b_mod):
    depth, d, n = w_mod.shape
    tn = 1024
    return pl.pallas_call(
        _mod_kernel,
        grid=(depth, n // tn),
        in_specs=[pl.BlockSpec((16, d), lambda l, j: (0, 0)),
                  pl.BlockSpec((1, d, tn), lambda l, j: (l, 0, j)),
                  pl.BlockSpec((1, 1, tn), lambda l, j: (l, 0, j))],
        out_specs=pl.BlockSpec((1, 16, tn), lambda l, j: (l, 0, j)),
        out_shape=jax.ShapeDtypeStruct((depth, 16, n), F32),
        compiler_params=_cparams(("arbitrary", "arbitrary")),
        name="modulation",
    )(cc, w_mod, b_mod.reshape(depth, 1, n))


def _mod_spec(chunk, d, nblk_per_batch):
    def imap(i):
        row = jnp.where(i % nblk_per_batch == 0, 8, i // nblk_per_batch)
        return (row, 0, chunk)
    return pl.BlockSpec((1, 1, d), imap)


def _lnmod_kernel(x_ref, sh_ref, sc_ref, o_ref):
    o_ref[...] = (_ln(x_ref[...]) * (1.0 + sc_ref[0]) + sh_ref[0]).astype(o_ref.dtype)


def _lnmod(x, mods3, nbb):
    n, d = x.shape
    return pl.pallas_call(
        _lnmod_kernel,
        grid=(n // ROW_BLK,),
        in_specs=[pl.BlockSpec((ROW_BLK, d), lambda i: (i, 0)),
                  _mod_spec(0, d, nbb), _mod_spec(1, d, nbb)],
        out_specs=pl.BlockSpec((ROW_BLK, d), lambda i: (i, 0)),
        out_shape=jax.ShapeDtypeStruct((n, d), BF16),
        compiler_params=_cparams(("parallel",)),
        name="ln_mod",
    )(x, mods3, mods3)


def _mm_kernel(a_ref, b_ref, o_ref):
    o_ref[...] = jnp.dot(a_ref[...], b_ref[...], preferred_element_type=F32).astype(o_ref.dtype)


def _matmul(a, b, out_dtype=F32):
    m, k = a.shape
    _, n = b.shape
    tm = MM_TM if m % MM_TM == 0 else ROW_BLK
    return pl.pallas_call(
        _mm_kernel,
        grid=(m // tm, n // MM_TN),
        in_specs=[pl.BlockSpec((tm, k), lambda i, j: (i, 0)),
                  pl.BlockSpec((k, MM_TN), lambda i, j: (0, j))],
        out_specs=pl.BlockSpec((tm, MM_TN), lambda i, j: (i, j)),
        out_shape=jax.ShapeDtypeStruct((m, n), out_dtype),
        compiler_params=_cparams(("parallel", "arbitrary")),
        name="in_proj",
    )(a, b)


def _halo_specs(width, col_blk, nblk):
    per = ROW_BLK // 8

    def prev_map(i):
        return (jnp.maximum(i * per - 1, 0), col_blk)

    def next_map(i):
        return (jnp.minimum((i + 1) * per, nblk * per - 1), col_blk)

    return (pl.BlockSpec((8, width), prev_map), pl.BlockSpec((8, width), next_map))


def _shifted(p, prev8, next8, i, nbb):
    pos = i % nbb
    has_prev = pos >= 2
    has_next = jnp.logical_and(pos >= 1, pos <= nbb - 2)
    rows = lax.broadcasted_iota(jnp.int32, p.shape, 0)
    prow = jnp.where(has_prev, prev8[7:8, :], 0.0)
    nrow = jnp.where(has_next, next8[0:1, :], 0.0)
    p_prev = jnp.where(rows == 0, prow, pltpu.roll(p, 1, 0))
    p_next = jnp.where(rows == p.shape[0] - 1, nrow, pltpu.roll(p, p.shape[0] - 1, 0))
    return p_prev, p_next


def _split3(x):
    hi = x.astype(BF16)
    r = x - hi.astype(F32)
    mid = r.astype(BF16)
    lo = (r - mid.astype(F32)).astype(BF16)
    return hi, mid, lo


def _dot_indicator(x, w):
    hi, mid, lo = _split3(x)
    return (jnp.dot(hi, w, preferred_element_type=F32) + jnp.dot(mid, w, preferred_element_type=F32)
            + jnp.dot(lo, w, preferred_element_type=F32))


def _dot_split(x, w_ref):
    hi, mid, _ = _split3(x)
    w_hi = w_ref[0]
    return (jnp.dot(hi, w_hi, preferred_element_type=F32) + jnp.dot(mid, w_hi, preferred_element_type=F32)
            + jnp.dot(hi, w_ref[1], preferred_element_type=F32))


def _split_weight(w):
    hi = w.astype(BF16)
    return jnp.stack([hi, (w - hi.astype(F32)).astype(BF16)])


def _seg_sum(x, e_ref, et_ref):
    return _dot_indicator(_dot_indicator(x, e_ref[...]), et_ref[...])


def _swap_halves(x):
    lanes = lax.broadcasted_iota(jnp.int32, x.shape, 1)
    first = (lanes % HEAD_DIM) < (HEAD_DIM // 2)
    n = x.shape[1]
    return jnp.where(first, pltpu.roll(x, n - HEAD_DIM // 2, 1), pltpu.roll(x, HEAD_DIM // 2, 1))


def _prep_kernel(nbb, p_ref, pp_ref, pn_ref, mu_ref, w0_ref, wl_ref, a0_ref, al_ref, gl_ref,
                 kk_ref, ka_ref, rk_ref, cos_ref, sin_ref, e_ref, et_ref,
                 r_o, kap_o, v_o, w_o, kd_o, b_o, bonus_o, g_o):
    i = pl.program_id(0)
    c = RW_WIDTH
    p = p_ref[...]
    p_prev, p_next = _shifted(p, pp_ref[...], pn_ref[...], i, nbb)
    pm = p + mu_ref[0:1, :] * (p_prev - p) + mu_ref[1:2, :] * (p_next - p)
    r, k, v = pm[:, :c], pm[:, c:2 * c], pm[:, 2 * c:3 * c]
    o = 3 * c
    wl = jnp.tanh(pm[:, o:o + 2 * DECAY_LORA])
    o += 2 * DECAY_LORA
    al = pm[:, o:o + 2 * ICLR_LORA]
    o += 2 * ICLR_LORA
    gl = _sigmoid(pm[:, o:o + GATE_LORA_PAD])
    g_o[...] = _dot_split(gl, gl_ref)
    kk = k * kk_ref[...]
    ss = _seg_sum(kk * kk, e_ref, et_ref)
    kk = kk / jnp.maximum(jnp.sqrt(ss), 1e-12)
    cos = jnp.concatenate([cos_ref[...]] * (c // LANE), axis=1)
    sin = jnp.concatenate([sin_ref[...]] * (c // LANE), axis=1)

    def rope(x):
        return x * cos + _swap_halves(x) * sin

    kap = rope(kk)
    r_o[...] = rope(r)
    kap_o[...] = kap
    v_o[...] = v
    k_sum = jnp.zeros_like(k)
    for d in range(2):
        w_raw = w0_ref[d:d + 1, :] + _dot_split(wl, wl_ref.at[d])
        w_o[d] = jnp.exp(-math.exp(-0.5) * _sigmoid(w_raw))
        a = _sigmoid(a0_ref[d:d + 1, :] + _dot_split(al, al_ref.at[d]))
        kd = k * (1.0 + (a - 1.0) * ka_ref[...])
        k_sum = k_sum + kd
        kd_o[d] = rope(kd)
        b_o[d] = kap * a
    bonus_o[...] = _seg_sum(r * k_sum * rk_ref[...], e_ref, et_ref) * v


def _rwkv_prepare(proj, lw, consts, nbb):
    n = proj.shape[0]
    c = RW_WIDTH
    nblk = n // ROW_BLK
    prev_spec, next_spec = _halo_specs(RW_COLS_PAD, 0, nblk)
    full = lambda shape: pl.BlockSpec(shape, lambda i: (0,) * len(shape))
    row_c = pl.BlockSpec((ROW_BLK, c), lambda i: (i, 0))
    row_2c = pl.BlockSpec((2, ROW_BLK, c), lambda i: (0, i, 0))
    tab = pl.BlockSpec((ROW_BLK, LANE), lambda i: (i % nbb, 0))
    outs = pl.pallas_call(
        functools.partial(_prep_kernel, nbb),
        grid=(nblk,),
        in_specs=[pl.BlockSpec((ROW_BLK, RW_COLS_PAD), lambda i: (i, 0)), prev_spec, next_spec,
                  full((2, RW_COLS_PAD)), full((2, c)), full((2, 2, 2 * DECAY_LORA, c)), full((2, c)),
                  full((2, 2, 2 * ICLR_LORA, c)), full((2, GATE_LORA_PAD, c)), full((1, c)), full((1, c)),
                  full((1, c)), tab, tab, full((c, LANE)), full((LANE, c))],
        out_specs=[row_c, row_c, row_c, row_2c, row_2c, row_2c, row_c, row_c],
        out_shape=[jax.ShapeDtypeStruct((n, c), F32)] * 3 + [jax.ShapeDtypeStruct((2, n, c), F32)] * 3
        + [jax.ShapeDtypeStruct((n, c), F32)] * 2,
        compiler_params=_cparams(("parallel",)),
        name="rwkv_prepare",
    )(proj, proj, proj, lw["mu"], lw["w0"], lw["w_lora"], lw["a0"], lw["a_lora"], lw["g_lora"],
      lw["k_k"], lw["k_a"], lw["r_k"], consts["cos"], consts["sin"], consts["seg"], consts["seg_t"])
    return outs


def _scan_kernel(rf_ref, kapf_ref, vf_ref, rb_ref, kapb_ref, vb_ref, wf_ref, kdf_ref, bf_ref, wb_ref, kdb_ref, bb_ref,
                 of_ref, ob_ref, s_ref, vec_ref, out_ref):
    nb, tt = rf_ref.shape[0], rf_ref.shape[1]
    n_pair = rf_ref.shape[2] // (2 * HEAD_DIM)
    nl = nb * n_pair
    vh = HEAD_DIM // SCAN_VSPLIT

    @pl.when(pl.program_id(0) == 0)
    def _():
        s_ref[...] = jnp.zeros_like(s_ref)

    def to_lanes(i, carry):
        slab = lambda x: jnp.concatenate([x[:, p * 2 * HEAD_DIM:(p + 1) * 2 * HEAD_DIM] for p in range(n_pair)], axis=0)
        tb = tt - 1 - i
        pairs = ((kapf_ref[:, i, :], kapb_ref[:, tb, :]), (rf_ref[:, i, :], rb_ref[:, tb, :]),
                 (wf_ref[0, :, i, :], wb_ref[0, :, tb, :]), (kdf_ref[0, :, i, :], kdb_ref[0, :, tb, :]),
                 (bf_ref[0, :, i, :], bb_ref[0, :, tb, :]), (vf_ref[:, i, :], vb_ref[:, tb, :]))
        for j, (xf, xb) in enumerate(pairs):
            res = jnp.concatenate([slab(xf), slab(xb)], axis=0).T
            vec_ref[0, i, j] = res[:HEAD_DIM]
            vec_ref[1, i, j] = res[HEAD_DIM:]
        return carry

    lax.fori_loop(0, tt, to_lanes, 0, unroll=SCAN_RELAYOUT_UNROLL)

    def add_to(acc, j, term):
        acc[j] = term if acc[j] is None else acc[j] + term

    def total(acc):
        return functools.reduce(lambda a, b: a + b, acc)

    skk0 = []
    for g in range(2):
        acc = [None] * SCAN_NACC
        for k in range(HEAD_DIM):
            add_to(acc, k % SCAN_NACC, s_ref[g, k] * vec_ref[g, 0, 0, k:k + 1, :])
        skk0.append(total(acc))

    def step(i, skks):
        i_next = jnp.minimum(i + 1, tt - 1)
        skks_next = []
        for g in range(2):
            parts = []
            for h in range(SCAN_VSPLIT):
                rows = slice(h * vh, (h + 1) * vh)
                skk_h = skks[g][rows]
                v = vec_ref[g, i, 5, rows]
                acc_out = [None] * SCAN_NACC
                acc_next = [None] * SCAN_NACC
                for k in range(HEAD_DIM):
                    s_new = (s_ref[g, k, rows] * vec_ref[g, i, 2, k:k + 1, :] - skk_h * vec_ref[g, i, 4, k:k + 1, :]
                             + v * vec_ref[g, i, 3, k:k + 1, :])
                    s_ref[g, k, rows] = s_new
                    add_to(acc_out, k % SCAN_NACC, s_new * vec_ref[g, i, 1, k:k + 1, :])
                    add_to(acc_next, k % SCAN_NACC, s_new * vec_ref[g, i_next, 0, k:k + 1, :])
                out_ref[g, i, rows] = total(acc_out)
                parts.append(total(acc_next))
            skks_next.append(jnp.concatenate(parts, axis=0))
        return tuple(skks_next)

    lax.fori_loop(0, tt, step, tuple(skk0))

    def to_tokens(i, carry):
        back = jnp.concatenate([out_ref[0, i], out_ref[1, i]], axis=0).T
        unslab = lambda x: jnp.concatenate([x[p * nb:(p + 1) * nb] for p in range(n_pair)], axis=1)
        of_ref[:, i, :] = unslab(back[:nl])
        ob_ref[:, tt - 1 - i, :] = unslab(back[nl:])
        return carry

    lax.fori_loop(0, tt, to_tokens, 0, unroll=SCAN_RELAYOUT_UNROLL)


def _wkv_scan(r, kap, v, w, kd, b, ctx_len):
    batch, t, width = r.shape
    lanes = batch * width // HEAD_DIM
    nb = t // SCAN_TT
    n_ctx_blk = ctx_len // SCAN_TT

    def mirrored(j):
        return jnp.where(j < n_ctx_blk, n_ctx_blk - 1 - j, nb - 1 - (j - n_ctx_blk))

    fwd = pl.BlockSpec((batch, SCAN_TT, width), lambda j: (0, j, 0))
    bwd = pl.BlockSpec((batch, SCAN_TT, width), lambda j: (0, mirrored(j), 0))
    fwd_dir = pl.BlockSpec((1, batch, SCAN_TT, width), lambda j: (0, 0, j, 0))
    bwd_dir = pl.BlockSpec((1, batch, SCAN_TT, width), lambda j: (1, 0, mirrored(j), 0))
    out = jax.ShapeDtypeStruct((batch, t, width), F32)
    return pl.pallas_call(
        _scan_kernel,
        grid=(nb,),
        in_specs=[fwd, fwd, fwd, bwd, bwd, bwd, fwd_dir, fwd_dir, fwd_dir, bwd_dir, bwd_dir, bwd_dir],
        out_specs=[fwd, bwd],
        out_shape=[out, out],
        scratch_shapes=[pltpu.VMEM((2, HEAD_DIM, HEAD_DIM, lanes), F32),
                        pltpu.VMEM((2, SCAN_TT, 6, HEAD_DIM, lanes), F32),
                        pltpu.VMEM((2, SCAN_TT, HEAD_DIM, lanes), F32)],
        compiler_params=_cparams(("arbitrary",)),
        name="wkv_scan",
    )(r, kap, v, r, kap, v, w, kd, b, w, kd, b)


def _rwkv_out_kernel(wkv_f_ref, wkv_b_ref, bonus_ref, g_ref, gw_ref, gb_ref, e_ref, et_ref, o_ref):
    x = wkv_f_ref[...] + wkv_b_ref[...]
    inv_n = 1.0 / HEAD_DIM
    mu = _seg_sum(x, e_ref, et_ref) * inv_n
    xc = x - mu
    var = _seg_sum(xc * xc, e_ref, et_ref) * inv_n
    y = xc * lax.rsqrt(var + GN_EPS) * gw_ref[...] + gb_ref[...]
    o_ref[...] = ((y + bonus_ref[...]) * g_ref[...]).astype(o_ref.dtype)


def _rwkv_out(wkv_f, wkv_b, bonus, g, lw, consts):
    n, c = bonus.shape
    full = lambda shape: pl.BlockSpec(shape, lambda i: (0,) * len(shape))
    row_c = pl.BlockSpec((ROW_BLK, c), lambda i: (i, 0))
    return pl.pallas_call(
        _rwkv_out_kernel,
        grid=(n // ROW_BLK,),
        in_specs=[row_c, row_c, row_c, row_c,
                  full((1, c)), full((1, c)), full((c, LANE)), full((LANE, c))],
        out_specs=row_c,
        out_shape=jax.ShapeDtypeStruct((n, c), BF16),
        compiler_params=_cparams(("parallel",)),
        name="rwkv_out",
    )(wkv_f, wkv_b, bonus, g, lw["gn_w"], lw["gn_b"], consts["seg"], consts["seg_t"])


def _conv_kernel(nbb, bg_ref, cg_ref, x_ref, cgp_ref, xp_ref, cgn_ref, xn_ref, w_ref, o_ref):
    i = pl.program_id(0)
    u = cg_ref[...] * x_ref[...]
    u_prev, u_next = _shifted(u, cgp_ref[...] * xp_ref[...], cgn_ref[...] * xn_ref[...], i, nbb)
    y = w_ref[0:1, :] * u_prev + w_ref[1:2, :] * u + w_ref[2:3, :] * u_next
    o_ref[...] = (bg_ref[...] * y).astype(o_ref.dtype)


def _short_conv(proj, conv_w, col0, nbb):
    n = proj.shape[0]
    cw = MM_TN
    nc = SC_WIDTH // cw
    assert col0 % cw == 0
    cb = col0 // cw
    nblk = n // ROW_BLK
    blk = lambda s: pl.BlockSpec((ROW_BLK, cw), lambda i, j: (i, cb + s * nc + j))

    def halos(s):
        prev_spec, next_spec = _halo_specs(cw, 0, nblk)
        pm, nm = prev_spec.index_map, next_spec.index_map
        return (pl.BlockSpec((8, cw), lambda i, j: (pm(i)[0], cb + s * nc + j)),
                pl.BlockSpec((8, cw), lambda i, j: (nm(i)[0], cb + s * nc + j)))

    cg_prev, cg_next = halos(1)
    x_prev, x_next = halos(2)
    return pl.pallas_call(
        functools.partial(_conv_kernel, nbb),
        grid=(nblk, nc),
        in_specs=[blk(0), blk(1), blk(2), cg_prev, x_prev, cg_next, x_next,
                  pl.BlockSpec((8, cw), lambda i, j: (0, j))],
        out_specs=pl.BlockSpec((ROW_BLK, cw), lambda i, j: (i, j)),
        out_shape=jax.ShapeDtypeStruct((n, SC_WIDTH), BF16),
        compiler_params=_cparams(("parallel", "arbitrary")),
        name="short_conv",
    )(proj, proj, proj, proj, proj, proj, proj, conv_w)


def _na_tables(rows):
    kr = min(NA_WIN_ROWS, rows)
    n_groups = rows // NA_GROUP
    krows = min(NA_KROWS, rows)
    n_dr = 2 * NA_WIN_ROWS - 1
    dr = np.full((n_groups, NA_GROUP, krows), n_dr, np.int32)
    bases = []
    for g in range(n_groups):
        r0 = g * NA_GROUP
        base = int(np.clip(np.clip(r0 - kr // 2, 0, rows - kr), 0, rows - krows))
        bases.append(base)
        for rl in range(NA_GROUP):
            r = r0 + rl
            rs = int(np.clip(r - kr // 2, 0, rows - kr))
            for j in range(krows):
                if rs <= base + j < rs + kr:
                    dr[g, rl, j] = base + j - r + NA_WIN_ROWS - 1
            assert (dr[g, rl] < n_dr).sum() == kr
    uniq, table_of = [], []
    for g in range(n_groups):
        for u, gu in enumerate(uniq):
            if np.array_equal(dr[g], dr[gu]):
                table_of.append(u)
                break
        else:
            table_of.append(len(uniq))
            uniq.append(g)
    return dr[uniq], tuple(bases), tuple(table_of)


def _na_kernel(ctx_len, dr, bases, table_of, q_ref, k_ref, v_ref, bias_ref, o_ref):
    scale = HEAD_DIM ** -0.5
    nq = NA_GROUP * GRID_W
    krows = dr.shape[2]
    nk = krows * GRID_W
    first = lax.broadcasted_iota(jnp.int32, (GRID_W, 2 * GRID_W), 1) < GRID_W

    def bias_of(table, h):
        rows = []
        for rl in range(NA_GROUP):
            tiles = [jnp.where(first, bias_ref[h, int(dr[table, rl, j])], bias_ref[h, int(dr[table, rl, j + 1])])
                     for j in range(0, krows, 2)]
            rows.append(jnp.concatenate(tiles, axis=1))
        return jnp.concatenate(rows, axis=0)

    outs_heads = []
    for h in range(2):
        sl = slice(h * HEAD_DIM, (h + 1) * HEAD_DIM)
        q = (q_ref[:, sl] * scale).astype(BF16)
        k = k_ref[:, sl].astype(BF16)
        v = v_ref[:, sl].astype(BF16)
        kc, vc = k[:ctx_len], v[:ctx_len]
        dn = (((1,), (1,)), ((), ()))
        s = lax.dot_general(q[:ctx_len], kc, dn, preferred_element_type=F32)
        s = s - jnp.max(s, axis=-1, keepdims=True)
        e = jnp.exp(s)
        p = (e / jnp.sum(e, axis=-1, keepdims=True)).astype(BF16)
        pieces = [jnp.dot(p, vc, preferred_element_type=F32)]
        biases = [bias_of(t, h) for t in range(dr.shape[0])]
        for g, base in enumerate(bases):
            q0 = ctx_len + g * nq
            k0 = ctx_len + base * GRID_W
            qg = q[q0:q0 + nq]
            s_win = (lax.dot_general(qg, k[k0:k0 + nk], dn, preferred_element_type=F32)
                     + biases[table_of[g]])
            s_ctx = lax.dot_general(qg, kc, dn, preferred_element_type=F32)
            m = jnp.maximum(jnp.max(s_win, axis=-1, keepdims=True), jnp.max(s_ctx, axis=-1, keepdims=True))
            e_win = jnp.exp(s_win - m)
            e_ctx = jnp.exp(s_ctx - m)
            inv = 1.0 / (jnp.sum(e_win, axis=-1, keepdims=True) + jnp.sum(e_ctx, axis=-1, keepdims=True))
            acc = jnp.dot((e_win * inv).astype(BF16), v[k0:k0 + nk], preferred_element_type=F32)
            acc = acc + jnp.dot((e_ctx * inv).astype(BF16), vc, preferred_element_type=F32)
            pieces.append(acc)
        outs_heads.append(jnp.concatenate(pieces, axis=0))
    o_ref[...] = jnp.concatenate(outs_heads, axis=1).astype(o_ref.dtype)


def _na_bias(rpb):
    qc = np.arange(GRID_W)[:, None]
    kc = np.arange(GRID_W)[None, :]
    wstart = np.clip(qc - NA_WIN_COLS // 2, 0, GRID_W - NA_WIN_COLS)
    col_ok = (kc >= wstart) & (kc < wstart + NA_WIN_COLS)
    dc = np.clip(kc - qc + NA_WIN_COLS - 1, 0, 2 * NA_WIN_COLS - 2)
    blocks = jnp.where(col_ok, rpb[:, :, dc], NEG_INF)
    blocks = jnp.concatenate([blocks, jnp.full_like(blocks[:, :1], NEG_INF)], axis=1)
    return jnp.concatenate([blocks, blocks], axis=-1)


def _na_attention(proj, bias, tables, col0, batch, t_len, ctx_len):
    n = proj.shape[0]
    cb = col0 // LANE
    hp = NA_HEADS // 2
    seq = lambda j: pl.BlockSpec((t_len, LANE), lambda h, b: (b, cb + j * hp + h))
    return pl.pallas_call(
        functools.partial(_na_kernel, ctx_len, *tables),
        grid=(hp, batch),
        in_specs=[seq(0), seq(1), seq(2),
                  pl.BlockSpec((2,) + bias.shape[1:], lambda h, b: (h, 0, 0, 0))],
        out_specs=pl.BlockSpec((t_len, LANE), lambda h, b: (b, h)),
        out_shape=jax.ShapeDtypeStruct((n, NA_WIDTH), BF16),
        compiler_params=_cparams(("arbitrary", "arbitrary")),
        name="na_attention",
    )(proj, proj, proj, bias)


def _merge1_kernel(yr_ref, yn_ref, ys_ref, wr_ref, wn_ref, ws_ref, g0_ref, g1_ref, g2_ref,
                   b0_ref, b1_ref, b2_ref, o_ref):
    m = _sigmoid(g0_ref[...] + b0_ref[...]) * jnp.dot(yr_ref[...], wr_ref[...], preferred_element_type=F32)
    m = m + _sigmoid(g1_ref[...] + b1_ref[...]) * jnp.dot(yn_ref[...], wn_ref[...], preferred_element_type=F32)
    m = m + _sigmoid(g2_ref[...] + b2_ref[...]) * jnp.dot(ys_ref[...], ws_ref[...], preferred_element_type=F32)
    o_ref[...] = m.astype(o_ref.dtype)


def _merge1(y_rw, y_na, y_sc, lw, proj, gate_col0, d):
    n, c = y_rw.shape
    tm, tn = 512, MM_TN
    gb = gate_col0 // tn
    nd = d // tn
    ysp = pl.BlockSpec((tm, c), lambda i, j: (i, 0))
    wsp = pl.BlockSpec((c, tn), lambda i, j: (0, j))
    gsp = lambda br: pl.BlockSpec((tm, tn), lambda i, j: (i, gb + br * nd + j))
    bsp = lambda br: pl.BlockSpec((1, tn), lambda i, j: (0, br * nd + j))
    return pl.pallas_call(
        _merge1_kernel,
        grid=(n // tm, nd),
        in_specs=[ysp, ysp, ysp, wsp, wsp, wsp, gsp(0), gsp(1), gsp(2), bsp(0), bsp(1), bsp(2)],
        out_specs=pl.BlockSpec((tm, tn), lambda i, j: (i, j)),
        out_shape=jax.ShapeDtypeStruct((n, d), BF16),
        compiler_params=_cparams(("parallel", "arbitrary")),
        name="merge_branches",
    )(y_rw, y_na, y_sc, lw["w_out_rw"], lw["w_out_na"], lw["w_out_sc"], proj, proj, proj,
      lw["b_gate"], lw["b_gate"], lw["b_gate"])


def _merge2_kernel(alpha, m_ref, w_ref, x_ref, g1_ref, lg_ref, lb_ref, sh_ref, sc_ref, rw_ref, rb_ref,
                   x1_ref, h2_ref, lo_ref):
    mix = jnp.dot(m_ref[...], w_ref[...], preferred_element_type=F32)
    x1 = _ln(alpha * x_ref[...] + g1_ref[0] * mix) * lg_ref[...] + lb_ref[...]
    x1_ref[...] = x1
    h2 = _ln(x1) * (1.0 + sc_ref[0]) + sh_ref[0]
    h2_ref[...] = h2.astype(h2_ref.dtype)
    lo_ref[...] = _dot_split(h2, rw_ref) + rb_ref[...]


def _merge2(m, x, mods3, lw, nbb, alpha):
    n, d = x.shape
    full = lambda shape: pl.BlockSpec(shape, lambda i: (0,) * len(shape))
    row = pl.BlockSpec((ROW_BLK, d), lambda i: (i, 0))
    return pl.pallas_call(
        functools.partial(_merge2_kernel, alpha),
        grid=(n // ROW_BLK,),
        in_specs=[row, full((d, d)), row, _mod_spec(2, d, nbb), full((1, d)), full((1, d)),
                  _mod_spec(3, d, nbb), _mod_spec(4, d, nbb), full((2, d, LANE)), full((1, LANE))],
        out_specs=[row, row, pl.BlockSpec((ROW_BLK, LANE), lambda i: (i, 0))],
        out_shape=[jax.ShapeDtypeStruct((n, d), F32), jax.ShapeDtypeStruct((n, d), F32),
                   jax.ShapeDtypeStruct((n, LANE), F32)],
        compiler_params=_cparams(("parallel",)),
        name="merge_out_ln",
    )(m, lw["w_merge"], x, mods3, lw["ln1_g"], lw["ln1_b"], mods3, mods3, lw["router_w"], lw["router_b"])


def _route_kernel(lo_ref, gate_ref, idx_ref, rank_ref, counts_ref, carry_ref):
    x = lo_ref[...]
    lanes = lax.broadcasted_iota(jnp.int32, x.shape, 1).astype(F32)
    x = jnp.where(lanes < N_EXPERTS, x, -jnp.inf)
    vals, idxs = [], []
    for _ in range(TOP_K):
        m = jnp.max(x, axis=-1, keepdims=True)
        sel = jnp.min(jnp.where(x == m, lanes, float(LANE)), axis=-1, keepdims=True)
        vals.append(m)
        idxs.append(sel)
        x = jnp.where(lanes == sel, -jnp.inf, x)
    es = [jnp.exp(v - vals[0]) for v in vals]
    tot = es[0] + es[1] + es[2] + es[3]
    @pl.when(pl.program_id(0) == 0)
    def _():
        carry_ref[...] = jnp.zeros_like(carry_ref)

    onehot = jnp.zeros(lo_ref.shape, F32)
    for j in range(TOP_K):
        onehot = onehot + jnp.where(lanes == idxs[j], 1.0, 0.0)
    nr = lo_ref.shape[0]
    below = (lax.broadcasted_iota(jnp.int32, (nr, nr), 1) < lax.broadcasted_iota(jnp.int32, (nr, nr), 0))
    prefix = jnp.dot(below.astype(BF16), onehot.astype(BF16), preferred_element_type=F32) + carry_ref[...]
    gate = jnp.zeros(lo_ref.shape, F32)
    idx = jnp.zeros(lo_ref.shape, F32)
    rank = jnp.zeros(lo_ref.shape, F32)
    for j in range(TOP_K):
        gate = jnp.where(lanes == j, es[j] / tot, gate)
        idx = jnp.where(lanes == j, idxs[j], idx)
        rank_j = jnp.sum(jnp.where(lanes == idxs[j], prefix, 0.0), axis=-1, keepdims=True)
        rank = jnp.where(lanes == j, rank_j, rank)
    gate_ref[...] = gate
    idx_ref[...] = idx.astype(jnp.int32)
    rank_ref[...] = rank.astype(jnp.int32)
    carry_ref[...] = carry_ref[...] + jnp.sum(onehot, axis=0, keepdims=True)
    counts_ref[...] = carry_ref[...].astype(jnp.int32)


def _route(logits):
    n = logits.shape[0]
    row = pl.BlockSpec((ROW_BLK, LANE), lambda i: (i, 0))
    return pl.pallas_call(
        _route_kernel,
        grid=(n // ROW_BLK,),
        in_specs=[row],
        out_specs=[row, row, row, pl.BlockSpec((1, LANE), lambda i: (0, 0))],
        out_shape=[jax.ShapeDtypeStruct((n, LANE), F32), jax.ShapeDtypeStruct((n, LANE), jnp.int32),
                   jax.ShapeDtypeStruct((n, LANE), jnp.int32), jax.ShapeDtypeStruct((1, LANE), jnp.int32)],
        scratch_shapes=[pltpu.VMEM((1, LANE), F32)],
        compiler_params=_cparams(("arbitrary",)),
        name="moe_route",
    )(logits)


def _expert_kernel(be_ref, x_ref, w1_ref, b1_ref, w2_ref, b2_ref, sel_ref, o_ref, w1b_ref, w2b_ref):
    i = pl.program_id(0)

    @pl.when(jnp.logical_or(i == 0, be_ref[i] != be_ref[jnp.maximum(i - 1, 0)]))
    def _():
        w1b_ref[...] = w1_ref[0].astype(BF16)
        w2b_ref[...] = w2_ref[0].astype(BF16)

    z = jnp.dot(x_ref[...].astype(BF16), w1b_ref[...], preferred_element_type=F32) + b1_ref[0]
    z_glu = jnp.minimum(z, SWIGLU_LIMIT)
    z_lin = jnp.clip(pltpu.roll(z, z.shape[1] - 1, 1), -SWIGLU_LIMIT, SWIGLU_LIMIT)
    act = z_glu * _sigmoid(SWIGLU_ALPHA * z_glu) * (z_lin + 1.0)
    lanes = lax.broadcasted_iota(jnp.int32, act.shape, 1)
    act = jnp.where(lanes % 2 == 0, act, 0.0).astype(BF16)
    act = jnp.dot(act, sel_ref[...], preferred_element_type=F32).astype(BF16)
    o_ref[...] = jnp.dot(act, w2b_ref[...], preferred_element_type=F32) + b2_ref[0]


def _experts(xs, block_e, lw):
    n_slots, d = xs.shape
    n_blocks = n_slots // MOE_BLOCK
    f2 = 2 * D_EXPERT
    sel = jnp.asarray(np.arange(f2)[:, None] == 2 * np.arange(D_EXPERT)[None, :], BF16)
    grid_spec = pltpu.PrefetchScalarGridSpec(
        num_scalar_prefetch=1,
        grid=(n_blocks,),
        in_specs=[pl.BlockSpec((MOE_BLOCK, d), lambda i, be: (i, 0)),
                  pl.BlockSpec((1, d, f2), lambda i, be: (be[i], 0, 0)),
                  pl.BlockSpec((1, 1, f2), lambda i, be: (be[i], 0, 0)),
                  pl.BlockSpec((1, D_EXPERT, d), lambda i, be: (be[i], 0, 0)),
                  pl.BlockSpec((1, 1, d), lambda i, be: (be[i], 0, 0)),
                  pl.BlockSpec((f2, D_EXPERT), lambda i, be: (0, 0))],
        out_specs=pl.BlockSpec((MOE_BLOCK, d), lambda i, be: (i, 0)),
        scratch_shapes=[pltpu.VMEM((d, f2), BF16), pltpu.VMEM((D_EXPERT, d), BF16)],
    )
    return pl.pallas_call(
        _expert_kernel,
        grid_spec=grid_spec,
        out_shape=jax.ShapeDtypeStruct((n_slots, d), F32),
        compiler_params=_cparams(("arbitrary",)),
        name="moe_experts",
    )(block_e, xs, lw["moe_w1"], lw["moe_b1"], lw["moe_w2"], lw["moe_b2"], sel)


def _moe(h2, idx, rank, counts, lw):
    n, d = h2.shape
    a = n * TOP_K
    counts = counts[0, :N_EXPERTS]
    padded = (counts + MOE_BLOCK - 1) // MOE_BLOCK * MOE_BLOCK
    pad_end = jnp.cumsum(padded)
    pad_start = pad_end - padded
    dest = pad_start[idx[:, :TOP_K]] + rank[:, :TOP_K]
    n_blocks = -(-a // MOE_BLOCK) + N_EXPERTS
    n_slots = n_blocks * MOE_BLOCK
    slot_tok = jnp.zeros(n_slots, jnp.int32).at[dest.reshape(a)].set(jnp.arange(a, dtype=jnp.int32) // TOP_K)
    block_start = jnp.arange(n_blocks, dtype=jnp.int32) * MOE_BLOCK
    block_e = jnp.minimum(jnp.sum(block_start[:, None] >= pad_end[None, :], axis=1), N_EXPERTS - 1).astype(jnp.int32)
    xs = h2[slot_tok]
    ys = _experts(xs, block_e + lw["moe_row0"], lw)
    return [ys[dest[:, j]] for j in range(TOP_K)]


def _final_kernel(alpha, x_ref, y0_ref, y1_ref, y2_ref, y3_ref, gate_ref, g2_ref, lg_ref, lb_ref, sh_ref, sc_ref,
                  x2_ref, h_ref):
    gate = gate_ref[...]
    y = y0_ref[...] * gate[:, 0:1]
    for j, y_ref in enumerate((y1_ref, y2_ref, y3_ref), start=1):
        y = y + y_ref[...] * gate[:, j:j + 1]
    x2 = _ln(alpha * x_ref[...] + g2_ref[0] * y) * lg_ref[...] + lb_ref[...]
    x2_ref[...] = x2
    h_ref[...] = (_ln(x2) * (1.0 + sc_ref[0]) + sh_ref[0]).astype(h_ref.dtype)


def _final(x1, y_rows, gate, mods3, mods3_next, lw, nbb, alpha):
    n, d = x1.shape
    assert len(y_rows) == TOP_K == 4
    full = lambda shape: pl.BlockSpec(shape, lambda i: (0,) * len(shape))
    row = pl.BlockSpec((ROW_BLK, d), lambda i: (i, 0))
    return pl.pallas_call(
        functools.partial(_final_kernel, alpha),
        grid=(n // ROW_BLK,),
        in_specs=[row, row, row, row, row, pl.BlockSpec((ROW_BLK, LANE), lambda i: (i, 0)),
                  _mod_spec(5, d, nbb), full((1, d)), full((1, d)), _mod_spec(0, d, nbb), _mod_spec(1, d, nbb)],
        out_specs=[row, row],
        out_shape=[jax.ShapeDtypeStruct((n, d), F32), jax.ShapeDtypeStruct((n, d), BF16)],
        compiler_params=_cparams(("parallel",)),
        name="moe_combine_residual_ln",
    )(x1, *y_rows, gate, mods3, lw["ln2_g"], lw["ln2_b"], mods3_next, mods3_next)


def _rope_tables(ctx_len, seq):
    n_freq = HEAD_DIM // 4
    t = np.arange(seq)
    inv = np.power(ROPE_BASE, -np.arange(n_freq, dtype=np.float32) / n_freq).astype(np.float32)
    row = (t // GRID_W).astype(np.float32)
    col = (t % GRID_W).astype(np.float32)
    ang = jnp.asarray(np.concatenate([row[:, None] * inv, col[:, None] * inv], -1))
    cos = jnp.cos(ang)
    sin = jnp.sin(ang)
    cos_h = jnp.concatenate([cos, cos], -1)
    sin_h = jnp.concatenate([-sin, sin], -1)
    cos_t = jnp.concatenate([jnp.ones((ctx_len, HEAD_DIM), F32), cos_h], 0)
    sin_t = jnp.concatenate([jnp.zeros((ctx_len, HEAD_DIM), F32), sin_h], 0)
    return jnp.tile(cos_t, (1, LANE // HEAD_DIM)), jnp.tile(sin_t, (1, LANE // HEAD_DIM))


def _to_scan_layout(x, batch, t_len):
    return x.reshape(x.shape[:-2] + (batch, t_len, x.shape[-1]))


def _pad_dir_lora(w):
    z = jnp.zeros_like(w[0])
    return jnp.stack([_split_weight(jnp.concatenate([w[0], z], 0)), _split_weight(jnp.concatenate([z, w[1]], 0))])


def kernel(x, c, ctx, c_ctx, w_mod, b_mod, w_in, rw_mu, rw_w0, rw_w_lora, rw_a0, rw_a_lora, rw_g_lora, rw_k_k, rw_k_a, rw_r_k, rw_gn_w, rw_gn_b, w_out_rw, na_rpb, w_out_na, sc_conv, w_out_sc, b_gate, w_merge, ln1_g, ln1_b, router_w, router_b, moe_w1, moe_b1, moe_w2, moe_b2, ln2_g, ln2_b):
    batch, seq, d = x.shape
    ctx_len = ctx.shape[1]
    depth = w_in.shape[0]
    t_len = ctx_len + seq
    n = batch * t_len
    nbb = t_len // ROW_BLK
    assert batch <= 8 and ctx_len == ROW_BLK and seq % (NA_GROUP * GRID_W) == 0
    alpha = (2 * depth) ** 0.25

    cc = jnp.zeros((16, d), F32).at[:batch].set(c).at[8].set(c_ctx)
    mods = _modulation(cc, w_mod, b_mod)
    mods3 = [mods[l].reshape(16, 1, 6 * d) for l in range(depth)]

    cos_t, sin_t = _rope_tables(ctx_len, seq)
    seg = (np.arange(RW_WIDTH)[:, None] // HEAD_DIM == np.arange(LANE)[None, :]).astype(np.float32)
    consts = dict(cos=cos_t, sin=sin_t, seg=jnp.asarray(seg, BF16), seg_t=jnp.asarray(seg.T, BF16))
    tables = _na_tables(seq // GRID_W)

    o_na = RW_COLS_PAD
    o_sc = o_na + 3 * NA_WIDTH
    o_gate = o_sc + 3 * SC_WIDTH
    pad_cols = RW_COLS_PAD - RW_COLS

    xs = jnp.concatenate([ctx, x], axis=1).reshape(n, d)
    h = _lnmod(xs, mods3[0], nbb)
    for l in range(depth):
        w_in_p = jnp.concatenate([w_in[l, :, :RW_COLS], jnp.zeros((d, pad_cols), F32), w_in[l, :, RW_COLS:]],
                                 axis=1).astype(BF16)
        lw = dict(
            mu=jnp.pad(rw_mu[l], ((0, 0), (0, pad_cols))), w0=rw_w0[l], w_lora=_pad_dir_lora(rw_w_lora[l]),
            a0=rw_a0[l], a_lora=_pad_dir_lora(rw_a_lora[l]),
            g_lora=_split_weight(jnp.pad(rw_g_lora[l], ((0, GATE_LORA_PAD - GATE_LORA), (0, 0)))),
            k_k=rw_k_k[l].reshape(1, -1), k_a=rw_k_a[l].reshape(1, -1), r_k=rw_r_k[l].reshape(1, -1),
            gn_w=rw_gn_w[l].reshape(1, -1), gn_b=rw_gn_b[l].reshape(1, -1),
            w_out_rw=w_out_rw[l].astype(BF16), w_out_na=w_out_na[l].astype(BF16),
            w_out_sc=w_out_sc[l].astype(BF16), b_gate=b_gate[l].reshape(1, -1),
            w_merge=w_merge[l].astype(BF16), ln1_g=ln1_g[l].reshape(1, -1), ln1_b=ln1_b[l].reshape(1, -1),
            router_w=_split_weight(jnp.pad(router_w[l], ((0, 0), (0, LANE - N_EXPERTS)))),
            router_b=jnp.pad(router_b[l], (0, LANE - N_EXPERTS)).reshape(1, -1),
            moe_row0=l * N_EXPERTS,
            moe_w1=moe_w1.reshape((depth * N_EXPERTS,) + moe_w1.shape[2:]),
            moe_b1=moe_b1.reshape(depth * N_EXPERTS, 1, -1),
            moe_w2=moe_w2.reshape((depth * N_EXPERTS,) + moe_w2.shape[2:]),
            moe_b2=moe_b2.reshape(depth * N_EXPERTS, 1, -1),
            ln2_g=ln2_g[l].reshape(1, -1), ln2_b=ln2_b[l].reshape(1, -1),
        )
        proj = _matmul(h, w_in_p)
        r_s, kap, v, w, kd, b, bonus, g = _rwkv_prepare(proj, lw, consts, nbb)
        tl = functools.partial(_to_scan_layout, batch=batch, t_len=t_len)
        wkv_f, wkv_b = _wkv_scan(tl(r_s), tl(kap), tl(v), tl(w), tl(kd), tl(b), ctx_len)
        y_rw = _rwkv_out(wkv_f.reshape(n, RW_WIDTH), wkv_b.reshape(n, RW_WIDTH), bonus, g, lw, consts)
        y_na = _na_attention(proj, _na_bias(na_rpb[l]), tables, o_na, batch, t_len, ctx_len)
        y_sc = _short_conv(proj, jnp.pad(sc_conv[l], ((0, 5), (0, 0))), o_sc, nbb)
        m = _merge1(y_rw, y_na, y_sc, lw, proj, o_gate, d)
        x1, h2, logits = _merge2(m, xs, mods3[l], lw, nbb, alpha)
        gate, idx, rank, counts = _route(logits)
        y_rows = _moe(h2, idx, rank, counts, lw)
        xs, h = _final(x1, y_rows, gate, mods3[l], mods3[min(l + 1, depth - 1)], lw, nbb, alpha)
    return xs.reshape(batch, t_len, d)[:, ctx_len:]
```

```python
import functools
import math

import numpy as np
import jax
import jax.numpy as jnp
from jax import lax
from jax.experimental import pallas as pl
from jax.experimental.pallas import tpu as pltpu

GRID_W = 64
RW_HEADS = 16
HEAD_DIM = 64
RW_WIDTH = RW_HEADS * HEAD_DIM
DECAY_LORA = 64
ICLR_LORA = 64
GATE_LORA = 160
GATE_LORA_PAD = 256
GN_EPS = 64e-5
NA_HEADS = 16
NA_WIDTH = NA_HEADS * HEAD_DIM
NA_WIN_ROWS = 8
NA_WIN_COLS = 16
SC_WIDTH = 1024
RW_COLS = 3 * RW_WIDTH + 2 * DECAY_LORA + 2 * ICLR_LORA + GATE_LORA
RW_COLS_PAD = 3 * RW_WIDTH + 2 * DECAY_LORA + 2 * ICLR_LORA + GATE_LORA_PAD
N_EXPERTS = 32
TOP_K = 4
D_EXPERT = 512
SWIGLU_ALPHA = 1.702
SWIGLU_LIMIT = 7.0
MOE_BLOCK = 256
ROPE_BASE = 10000.0
LN_EPS = 1e-6
NEG_INF = -1e30

LANE = 128
ROW_BLK = 256
MM_TM = 2048
MM_TN = 512
SCAN_TT = 32
SCAN_VSPLIT = 2
SCAN_NACC = 1
SCAN_RELAYOUT_UNROLL = 4
NA_GROUP = 4
NA_KROWS = 12
VMEM_LIMIT = 56 * 1024 * 1024

F32 = jnp.float32
BF16 = jnp.bfloat16


def _cparams(sem):
    return pltpu.CompilerParams(dimension_semantics=sem, vmem_limit_bytes=VMEM_LIMIT)


def _ln(x):
    mu = jnp.mean(x, axis=-1, keepdims=True)
    xc = x - mu
    var = jnp.mean(xc * xc, axis=-1, keepdims=True)
    return xc * lax.rsqrt(var + LN_EPS)


def _sigmoid(x):
    return 1.0 / (1.0 + jnp.exp(-x))


def _mod_kernel(c_ref, w_ref, b_ref, o_ref):
    c = c_ref[...]
    s = (c * _sigmoid(c)).astype(BF16)
    o_ref[0] = jnp.dot(s, w_ref[0].astype(BF16), preferred_element_type=F32) + b_ref[0]


def _modulation(cc, w_mod, b_mod):
    depth, d, n = w_mod.shape
    tn = 1024
    return pl.pallas_call(
        _mod_kernel,
        grid=(depth, n // tn),
        in_specs=[pl.BlockSpec((16, d), lambda l, j: (0, 0)),
                  pl.BlockSpec((1, d, tn), lambda l, j: (l, 0, j)),
                  pl.BlockSpec((1, 1, tn), lambda l, j: (l, 0, j))],
        out_specs=pl.BlockSpec((1, 16, tn), lambda l, j: (l, 0, j)),
        out_shape=jax.ShapeDtypeStruct((depth, 16, n), F32),
        compiler_params=_cparams(("arbitrary", "arbitrary")),
        name="modulation",
    )(cc, w_mod, b_mod.reshape(depth, 1, n))


def _mod_spec(chunk, d, nblk_per_batch):
    def imap(i):
        row = jnp.where(i % nblk_per_batch == 0, 8, i // nblk_per_batch)
        return (row, 0, chunk)
    return pl.BlockSpec((1, 1, d), imap)


def _lnmod_kernel(x_ref, sh_ref, sc_ref, o_ref):
    o_ref[...] = (_ln(x_ref[...]) * (1.0 + sc_ref[0]) + sh_ref[0]).astype(o_ref.dtype)


def _lnmod(x, mods3, nbb):
    n, d = x.shape
    return pl.pallas_call(
        _lnmod_kernel,
        grid=(n // ROW_BLK,),
        in_specs=[pl.BlockSpec((ROW_BLK, d), lambda i: (i, 0)),
                  _mod_spec(0, d, nbb), _mod_spec(1, d, nbb)],
        out_specs=pl.BlockSpec((ROW_BLK, d), lambda i: (i, 0)),
        out_shape=jax.ShapeDtypeStruct((n, d), BF16),
        compiler_params=_cparams(("parallel",)),
        name="ln_mod",
    )(x, mods3, mods3)


def _mm_kernel(a_ref, b_ref, o_ref):
    o_ref[...] = jnp.dot(a_ref[...], b_ref[...], preferred_element_type=F32).astype(o_ref.dtype)


def _matmul(a, b, out_dtype=F32):
    m, k = a.shape
    _, n = b.shape
    tm = MM_TM if m % MM_TM == 0 else ROW_BLK
    return pl.pallas_call(
        _mm_kernel,
        grid=(m // tm, n // MM_TN),
        in_specs=[pl.BlockSpec((tm, k), lambda i, j: (i, 0)),
                  pl.BlockSpec((k, MM_TN), lambda i, j: (0, j))],
        out_specs=pl.BlockSpec((tm, MM_TN), lambda i, j: (i, j)),
        out_shape=jax.ShapeDtypeStruct((m, n), out_dtype),
        compiler_params=_cparams(("parallel", "arbitrary")),
        name="in_proj",
    )(a, b)


def _halo_specs(width, col_blk, nblk):
    per = ROW_BLK // 8

    def prev_map(i):
        return (jnp.maximum(i * per - 1, 0), col_blk)

    def next_map(i):
        return (jnp.minimum((i + 1) * per, nblk * per - 1), col_blk)

    return (pl.BlockSpec((8, width), prev_map), pl.BlockSpec((8, width), next_map))


def _shifted(p, prev8, next8, i, nbb):
    pos = i % nbb
    has_prev = pos >= 2
    has_next = jnp.logical_and(pos >= 1, pos <= nbb - 2)
    rows = lax.broadcasted_iota(jnp.int32, p.shape, 0)
    prow = jnp.where(has_prev, prev8[7:8, :], 0.0)
    nrow = jnp.where(has_next, next8[0:1, :], 0.0)
    p_prev = jnp.where(rows == 0, prow, pltpu.roll(p, 1, 0))
    p_next = jnp.where(rows == p.shape[0] - 1, nrow, pltpu.roll(p, p.shape[0] - 1, 0))
    return p_prev, p_next


def _split3(x):
    hi = x.astype(BF16)
    r = x - hi.astype(F32)
    mid = r.astype(BF16)
    lo = (r - mid.astype(F32)).astype(BF16)
    return hi, mid, lo


def _dot_indicator(x, w):
    hi, mid, lo = _split3(x)
    return (jnp.dot(hi, w, preferred_element_type=F32) + jnp.dot(mid, w, preferred_element_type=F32)
            + jnp.dot(lo, w, preferred_element_type=F32))


def _dot_split(x, w_ref):
    hi, mid, _ = _split3(x)
    w_hi = w_ref[0]
    return (jnp.dot(hi, w_hi, preferred_element_type=F32) + jnp.dot(mid, w_hi, preferred_element_type=F32)
            + jnp.dot(hi, w_ref[1], preferred_element_type=F32))


def _split_weight(w):
    hi = w.astype(BF16)
    return jnp.stack([hi, (w - hi.astype(F32)).astype(BF16)])


def _seg_sum(x, e_ref, et_ref):
    return _dot_indicator(_dot_indicator(x, e_ref[...]), et_ref[...])


def _swap_halves(x):
    lanes = lax.broadcasted_iota(jnp.int32, x.shape, 1)
    first = (lanes % HEAD_DIM) < (HEAD_DIM // 2)
    n = x.shape[1]
    return jnp.where(first, pltpu.roll(x, n - HEAD_DIM // 2, 1), pltpu.roll(x, HEAD_DIM // 2, 1))


def _prep_kernel(nbb, p_ref, pp_ref, pn_ref, mu_ref, w0_ref, wl_ref, a0_ref, al_ref, gl_ref,
                 kk_ref, ka_ref, rk_ref, cos_ref, sin_ref, e_ref, et_ref,
                 r_o, kap_o, v_o, w_o, kd_o, b_o, bonus_o, g_o):
    i = pl.program_id(0)
    c = RW_WIDTH
    p = p_ref[...]
    p_prev, p_next = _shifted(p, pp_ref[...], pn_ref[...], i, nbb)
    pm = p + mu_ref[0:1, :] * (p_prev - p) + mu_ref[1:2, :] * (p_next - p)
    r, k, v = pm[:, :c], pm[:, c:2 * c], pm[:, 2 * c:3 * c]
    o = 3 * c
    wl = jnp.tanh(pm[:, o:o + 2 * DECAY_LORA])
    o += 2 * DECAY_LORA
    al = pm[:, o:o + 2 * ICLR_LORA]
    o += 2 * ICLR_LORA
    gl = _sigmoid(pm[:, o:o + GATE_LORA_PAD])
    g_o[...] = _dot_split(gl, gl_ref)
    kk = k * kk_ref[...]
    ss = _seg_sum(kk * kk, e_ref, et_ref)
    kk = kk / jnp.maximum(jnp.sqrt(ss), 1e-12)
    cos = jnp.concatenate([cos_ref[...]] * (c // LANE), axis=1)
    sin = jnp.concatenate([sin_ref[...]] * (c // LANE), axis=1)

    def rope(x):
        return x * cos + _swap_halves(x) * sin

    kap = rope(kk)
    r_o[...] = rope(r)
    kap_o[...] = kap
    v_o[...] = v
    k_sum = jnp.zeros_like(k)
    for d in range(2):
        w_raw = w0_ref[d:d + 1, :] + _dot_split(wl, wl_ref.at[d])
        w_o[d] = jnp.exp(-math.exp(-0.5) * _sigmoid(w_raw))
        a = _sigmoid(a0_ref[d:d + 1, :] + _dot_split(al, al_ref.at[d]))
        kd = k * (1.0 + (a - 1.0) * ka_ref[...])
        k_sum = k_sum + kd
        kd_o[d] = rope(kd)
        b_o[d] = kap * a
    bonus_o[...] = _seg_sum(r * k_sum * rk_ref[...], e_ref, et_ref) * v


def _rwkv_prepare(proj, lw, consts, nbb):
    n = proj.shape[0]
    c = RW_WIDTH
    nblk = n // ROW_BLK
    prev_spec, next_spec = _halo_specs(RW_COLS_PAD, 0, nblk)
    full = lambda shape: pl.BlockSpec(shape, lambda i: (0,) * len(shape))
    row_c = pl.BlockSpec((ROW_BLK, c), lambda i: (i, 0))
    row_2c = pl.BlockSpec((2, ROW_BLK, c), lambda i: (0, i, 0))
    tab = pl.BlockSpec((ROW_BLK, LANE), lambda i: (i % nbb, 0))
    outs = pl.pallas_call(
        functools.partial(_prep_kernel, nbb),
        grid=(nblk,),
        in_specs=[pl.BlockSpec((ROW_BLK, RW_COLS_PAD), lambda i: (i, 0)), prev_spec, next_spec,
                  full((2, RW_COLS_PAD)), full((2, c)), full((2, 2, 2 * DECAY_LORA, c)), full((2, c)),
                  full((2, 2, 2 * ICLR_LORA, c)), full((2, GATE_LORA_PAD, c)), full((1, c)), full((1, c)),
                  full((1, c)), tab, tab, full((c, LANE)), full((LANE, c))],
        out_specs=[row_c, row_c, row_c, row_2c, row_2c, row_2c, row_c, row_c],
        out_shape=[jax.ShapeDtypeStruct((n, c), F32)] * 3 + [jax.ShapeDtypeStruct((2, n, c), F32)] * 3
        + [jax.ShapeDtypeStruct((n, c), F32)] * 2,
        compiler_params=_cparams(("parallel",)),
        name="rwkv_prepare",
    )(proj, proj, proj, lw["mu"], lw["w0"], lw["w_lora"], lw["a0"], lw["a_lora"], lw["g_lora"],
      lw["k_k"], lw["k_a"], lw["r_k"], consts["cos"], consts["sin"], consts["seg"], consts["seg_t"])
    return outs


def _scan_kernel(rf_ref, kapf_ref, vf_ref, rb_ref, kapb_ref, vb_ref, wf_ref, kdf_ref, bf_ref, wb_ref, kdb_ref, bb_ref,
                 of_ref, ob_ref, s_ref, vec_ref, out_ref):
    nb, tt = rf_ref.shape[0], rf_ref.shape[1]
    n_pair = rf_ref.shape[2] // (2 * HEAD_DIM)
    nl = nb * n_pair
    vh = HEAD_DIM // SCAN_VSPLIT

    @pl.when(pl.program_id(0) == 0)
    def _():
        s_ref[...] = jnp.zeros_like(s_ref)

    def to_lanes(i, carry):
        slab = lambda x: jnp.concatenate([x[:, p * 2 * HEAD_DIM:(p + 1) * 2 * HEAD_DIM] for p in range(n_pair)], axis=0)
        tb = tt - 1 - i
        pairs = ((kapf_ref[:, i, :], kapb_ref[:, tb, :]), (rf_ref[:, i, :], rb_ref[:, tb, :]),
                 (wf_ref[0, :, i, :], wb_ref[0, :, tb, :]), (kdf_ref[0, :, i, :], kdb_ref[0, :, tb, :]),
                 (bf_ref[0, :, i, :], bb_ref[0, :, tb, :]), (vf_ref[:, i, :], vb_ref[:, tb, :]))
        for j, (xf, xb) in enumerate(pairs):
            res = jnp.concatenate([slab(xf), slab(xb)], axis=0).T
            vec_ref[0, i, j] = res[:HEAD_DIM]
            vec_ref[1, i, j] = res[HEAD_DIM:]
        return carry

    lax.fori_loop(0, tt, to_lanes, 0, unroll=SCAN_RELAYOUT_UNROLL)

    def add_to(acc, j, term):
        acc[j] = term if acc[j] is None else acc[j] + term

    def total(acc):
        return functools.reduce(lambda a, b: a + b, acc)

    skk0 = []
    for g in range(2):
        acc = [None] * SCAN_NACC
        for k in range(HEAD_DIM):
            add_to(acc, k % SCAN_NACC, s_ref[g, k] * vec_ref[g, 0, 0, k:k + 1, :])
        skk0.append(total(acc))

    def step(i, skks):
        i_next = jnp.minimum(i + 1, tt - 1)
        skks_next = []
        for g in range(2):
            parts = []
            for h in range(SCAN_VSPLIT):
                rows = slice(h * vh, (h + 1) * vh)
                skk_h = skks[g][rows]
                v = vec_ref[g, i, 5, rows]
                acc_out = [None] * SCAN_NACC
                acc_next = [None] * SCAN_NACC
                for k in range(HEAD_DIM):
                    s_new = (s_ref[g, k, rows] * vec_ref[g, i, 2, k:k + 1, :] - skk_h * vec_ref[g, i, 4, k:k + 1, :]
                             + v * vec_ref[g, i, 3, k:k + 1, :])
                    s_ref[g, k, rows] = s_new
                    add_to(acc_out, k % SCAN_NACC, s_new * vec_ref[g, i, 1, k:k + 1, :])
                    add_to(acc_next, k % SCAN_NACC, s_new * vec_ref[g, i_next, 0, k:k + 1, :])
                out_ref[g, i, rows] = total(acc_out)
                parts.append(total(acc_next))
            skks_next.append(jnp.concatenate(parts, axis=0))
        return tuple(skks_next)

    lax.fori_loop(0, tt, step, tuple(skk0))

    def to_tokens(i, carry):
        back = jnp.concatenate([out_ref[0, i], out_ref[1, i]], axis=0).T
        unslab = lambda x: jnp.concatenate([x[p * nb:(p + 1) * nb] for p in range(n_pair)], axis=1)
        of_ref[:, i, :] = unslab(back[:nl])
        ob_ref[:, tt - 1 - i, :] = unslab(back[nl:])
        return carry

    lax.fori_loop(0, tt, to_tokens, 0, unroll=SCAN_RELAYOUT_UNROLL)


def _wkv_scan(r, kap, v, w, kd, b, ctx_len):
    batch, t, width = r.shape
    lanes = batch * width // HEAD_DIM
    nb = t // SCAN_TT
    n_ctx_blk = ctx_len // SCAN_TT

    def mirrored(j):
        return jnp.where(j < n_ctx_blk, n_ctx_blk - 1 - j, nb - 1 - (j - n_ctx_blk))

    fwd = pl.BlockSpec((batch, SCAN_TT, width), lambda j: (0, j, 0))
    bwd = pl.BlockSpec((batch, SCAN_TT, width), lambda j: (0, mirrored(j), 0))
    fwd_dir = pl.BlockSpec((1, batch, SCAN_TT, width), lambda j: (0, 0, j, 0))
    bwd_dir = pl.BlockSpec((1, batch, SCAN_TT, width), lambda j: (1, 0, mirrored(j), 0))
    out = jax.ShapeDtypeStruct((batch, t, width), F32)
    return pl.pallas_call(
        _scan_kernel,
        grid=(nb,),
        in_specs=[fwd, fwd, fwd, bwd, bwd, bwd, fwd_dir, fwd_dir, fwd_dir, bwd_dir, bwd_dir, bwd_dir],
        out_specs=[fwd, bwd],
        out_shape=[out, out],
        scratch_shapes=[pltpu.VMEM((2, HEAD_DIM, HEAD_DIM, lanes), F32),
                        pltpu.VMEM((2, SCAN_TT, 6, HEAD_DIM, lanes), F32),
                        pltpu.VMEM((2, SCAN_TT, HEAD_DIM, lanes), F32)],
        compiler_params=_cparams(("arbitrary",)),
        name="wkv_scan",
    )(r, kap, v, r, kap, v, w, kd, b, w, kd, b)


def _rwkv_out_kernel(wkv_f_ref, wkv_b_ref, bonus_ref, g_ref, gw_ref, gb_ref, e_ref, et_ref, o_ref):
    x = wkv_f_ref[...] + wkv_b_ref[...]
    inv_n = 1.0 / HEAD_DIM
    mu = _seg_sum(x, e_ref, et_ref) * inv_n
    xc = x - mu
    var = _seg_sum(xc * xc, e_ref, et_ref) * inv_n
    y = xc * lax.rsqrt(var + GN_EPS) * gw_ref[...] + gb_ref[...]
    o_ref[...] = ((y + bonus_ref[...]) * g_ref[...]).astype(o_ref.dtype)


def _rwkv_out(wkv_f, wkv_b, bonus, g, lw, consts):
    n, c = bonus.shape
    full = lambda shape: pl.BlockSpec(shape, lambda i: (0,) * len(shape))
    row_c = pl.BlockSpec((ROW_BLK, c), lambda i: (i, 0))
    return pl.pallas_call(
        _rwkv_out_kernel,
        grid=(n // ROW_BLK,),
        in_specs=[row_c, row_c, row_c, row_c,
                  full((1, c)), full((1, c)), full((c, LANE)), full((LANE, c))],
        out_specs=row_c,
        out_shape=jax.ShapeDtypeStruct((n, c), BF16),
        compiler_params=_cparams(("parallel",)),
        name="rwkv_out",
    )(wkv_f, wkv_b, bonus, g, lw["gn_w"], lw["gn_b"], consts["seg"], consts["seg_t"])


def _conv_kernel(nbb, bg_ref, cg_ref, x_ref, cgp_ref, xp_ref, cgn_ref, xn_ref, w_ref, o_ref):
    i = pl.program_id(0)
    u = cg_ref[...] * x_ref[...]
    u_prev, u_next = _shifted(u, cgp_ref[...] * xp_ref[...], cgn_ref[...] * xn_ref[...], i, nbb)
    y = w_ref[0:1, :] * u_prev + w_ref[1:2, :] * u + w_ref[2:3, :] * u_next
    o_ref[...] = (bg_ref[...] * y).astype(o_ref.dtype)


def _short_conv(proj, conv_w, col0, nbb):
    n = proj.shape[0]
    cw = MM_TN
    nc = SC_WIDTH // cw
    assert col0 % cw == 0
    cb = col0 // cw
    nblk = n // ROW_BLK
    blk = lambda s: pl.BlockSpec((ROW_BLK, cw), lambda i, j: (i, cb + s * nc + j))

    def halos(s):
        prev_spec, next_spec = _halo_specs(cw, 0, nblk)
        pm, nm = prev_spec.index_map, next_spec.index_map
        return (pl.BlockSpec((8, cw), lambda i, j: (pm(i)[0], cb + s * nc + j)),
                pl.BlockSpec((8, cw), lambda i, j: (nm(i)[0], cb + s * nc + j)))

    cg_prev, cg_next = halos(1)
    x_prev, x_next = halos(2)
    return pl.pallas_call(
        functools.partial(_conv_kernel, nbb),
        grid=(nblk, nc),
        in_specs=[blk(0), blk(1), blk(2), cg_prev, x_prev, cg_next, x_next,
                  pl.BlockSpec((8, cw), lambda i, j: (0, j))],
        out_specs=pl.BlockSpec((ROW_BLK, cw), lambda i, j: (i, j)),
        out_shape=jax.ShapeDtypeStruct((n, SC_WIDTH), BF16),
        compiler_params=_cparams(("parallel", "arbitrary")),
        name="short_conv",
    )(proj, proj, proj, proj, proj, proj, proj, conv_w)


def _na_tables(rows):
    kr = min(NA_WIN_ROWS, rows)
    n_groups = rows // NA_GROUP
    krows = min(NA_KROWS, rows)
    n_dr = 2 * NA_WIN_ROWS - 1
    dr = np.full((n_groups, NA_GROUP, krows), n_dr, np.int32)
    bases = []
    for g in range(n_groups):
        r0 = g * NA_GROUP
        base = int(np.clip(np.clip(r0 - kr // 2, 0, rows - kr), 0, rows - krows))
        bases.append(base)
        for rl in range(NA_GROUP):
            r = r0 + rl
            rs = int(np.clip(r - kr // 2, 0, rows - kr))
            for j in range(krows):
                if rs <= base + j < rs + kr:
                    dr[g, rl, j] = base + j - r + NA_WIN_ROWS - 1
            assert (dr[g, rl] < n_dr).sum() == kr
    uniq, table_of = [], []
    for g in range(n_groups):
        for u, gu in enumerate(uniq):
            if np.array_equal(dr[g], dr[gu]):
                table_of.append(u)
                break
        else:
            table_of.append(len(uniq))
            uniq.append(g)
    return dr[uniq], tuple(bases), tuple(table_of)


def _na_kernel(ctx_len, dr, bases, table_of, q_ref, k_ref, v_ref, bias_ref, o_ref):
    scale = HEAD_DIM ** -0.5
    nq = NA_GROUP * GRID_W
    krows = dr.shape[2]
    nk = krows * GRID_W
    first = lax.broadcasted_iota(jnp.int32, (GRID_W, 2 * GRID_W), 1) < GRID_W

    def bias_of(table, h):
        rows = []
        for rl in range(NA_GROUP):
            tiles = [jnp.where(first, bias_ref[h, int(dr[table, rl, j])], bias_ref[h, int(dr[table, rl, j + 1])])
                     for j in range(0, krows, 2)]
            rows.append(jnp.concatenate(tiles, axis=1))
        return jnp.concatenate(rows, axis=0)

    outs_heads = []
    for h in range(2):
        sl = slice(h * HEAD_DIM, (h + 1) * HEAD_DIM)
        q = (q_ref[:, sl] * scale).astype(BF16)
        k = k_ref[:, sl].astype(BF16)
        v = v_ref[:, sl].astype(BF16)
        kc, vc = k[:ctx_len], v[:ctx_len]
        dn = (((1,), (1,)), ((), ()))
        s = lax.dot_general(q[:ctx_len], kc, dn, preferred_element_type=F32)
        s = s - jnp.max(s, axis=-1, keepdims=True)
        e = jnp.exp(s)
        p = (e / jnp.sum(e, axis=-1, keepdims=True)).astype(BF16)
        pieces = [jnp.dot(p, vc, preferred_element_type=F32)]
        biases = [bias_of(t, h) for t in range(dr.shape[0])]
        for g, base in enumerate(bases):
            q0 = ctx_len + g * nq
            k0 = ctx_len + base * GRID_W
            qg = q[q0:q0 + nq]
            s_win = (lax.dot_general(qg, k[k0:k0 + nk], dn, preferred_element_type=F32)
                     + biases[table_of[g]])
            s_ctx = lax.dot_general(qg, kc, dn, preferred_element_type=F32)
            m = jnp.maximum(jnp.max(s_win, axis=-1, keepdims=True), jnp.max(s_ctx, axis=-1, keepdims=True))
            e_win = jnp.exp(s_win - m)
            e_ctx = jnp.exp(s_ctx - m)
            inv = 1.0 / (jnp.sum(e_win, axis=-1, keepdims=True) + jnp.sum(e_ctx, axis=-1, keepdims=True))
            acc = jnp.dot((e_win * inv).astype(BF16), v[k0:k0 + nk], preferred_element_type=F32)
            acc = acc + jnp.dot((e_ctx * inv).astype(BF16), vc, preferred_element_type=F32)
            pieces.append(acc)
        outs_heads.append(jnp.concatenate(pieces, axis=0))
    o_ref[...] = jnp.concatenate(outs_heads, axis=1).astype(o_ref.dtype)


def _na_bias(rpb):
    qc = np.arange(GRID_W)[:, None]
    kc = np.arange(GRID_W)[None, :]
    wstart = np.clip(qc - NA_WIN_COLS // 2, 0, GRID_W - NA_WIN_COLS)
    col_ok = (kc >= wstart) & (kc < wstart + NA_WIN_COLS)
    dc = np.clip(kc - qc + NA_WIN_COLS - 1, 0, 2 * NA_WIN_COLS - 2)
    blocks = jnp.where(col_ok, rpb[:, :, dc], NEG_INF)
    blocks = jnp.concatenate([blocks, jnp.full_like(blocks[:, :1], NEG_INF)], axis=1)
    return jnp.concatenate([blocks, blocks], axis=-1)


def _na_attention(proj, bias, tables, col0, batch, t_len, ctx_len):
    n = proj.shape[0]
    cb = col0 // LANE
    hp = NA_HEADS // 2
    seq = lambda j: pl.BlockSpec((t_len, LANE), lambda h, b: (b, cb + j * hp + h))
    return pl.pallas_call(
        functools.partial(_na_kernel, ctx_len, *tables),
        grid=(hp, batch),
        in_specs=[seq(0), seq(1), seq(2),
                  pl.BlockSpec((2,) + bias.shape[1:], lambda h, b: (h, 0, 0, 0))],
        out_specs=pl.BlockSpec((t_len, LANE), lambda h, b: (b, h)),
        out_shape=jax.ShapeDtypeStruct((n, NA_WIDTH), BF16),
        compiler_params=_cparams(("arbitrary", "arbitrary")),
        name="na_attention",
    )(proj, proj, proj, bias)


def _merge1_kernel(yr_ref, yn_ref, ys_ref, wr_ref, wn_ref, ws_ref, g0_ref, g1_ref, g2_ref,
                   b0_ref, b1_ref, b2_ref, o_ref):
    m = _sigmoid(g0_ref[...] + b0_ref[...]) * jnp.dot(yr_ref[...], wr_ref[...], preferred_element_type=F32)
    m = m + _sigmoid(g1_ref[...] + b1_ref[...]) * jnp.dot(yn_ref[...], wn_ref[...], preferred_element_type=F32)
    m = m + _sigmoid(g2_ref[...] + b2_ref[...]) * jnp.dot(ys_ref[...], ws_ref[...], preferred_element_type=F32)
    o_ref[...] = m.astype(o_ref.dtype)


def _merge1(y_rw, y_na, y_sc, lw, proj, gate_col0, d):
    n, c = y_rw.shape
    tm, tn = 512, MM_TN
    gb = gate_col0 // tn
    nd = d // tn
    ysp = pl.BlockSpec((tm, c), lambda i, j: (i, 0))
    wsp = pl.BlockSpec((c, tn), lambda i, j: (0, j))
    gsp = lambda br: pl.BlockSpec((tm, tn), lambda i, j: (i, gb + br * nd + j))
    bsp = lambda br: pl.BlockSpec((1, tn), lambda i, j: (0, br * nd + j))
    return pl.pallas_call(
        _merge1_kernel,
        grid=(n // tm, nd),
        in_specs=[ysp, ysp, ysp, wsp, wsp, wsp, gsp(0), gsp(1), gsp(2), bsp(0), bsp(1), bsp(2)],
        out_specs=pl.BlockSpec((tm, tn), lambda i, j: (i, j)),
        out_shape=jax.ShapeDtypeStruct((n, d), BF16),
        compiler_params=_cparams(("parallel", "arbitrary")),
        name="merge_branches",
    )(y_rw, y_na, y_sc, lw["w_out_rw"], lw["w_out_na"], lw["w_out_sc"], proj, proj, proj,
      lw["b_gate"], lw["b_gate"], lw["b_gate"])


def _merge2_kernel(alpha, m_ref, w_ref, x_ref, g1_ref, lg_ref, lb_ref, sh_ref, sc_ref, rw_ref, rb_ref,
                   x1_ref, h2_ref, lo_ref):
    mix = jnp.dot(m_ref[...], w_ref[...], preferred_element_type=F32)
    x1 = _ln(alpha * x_ref[...] + g1_ref[0] * mix) * lg_ref[...] + lb_ref[...]
    x1_ref[...] = x1
    h2 = _ln(x1) * (1.0 + sc_ref[0]) + sh_ref[0]
    h2_ref[...] = h2.astype(h2_ref.dtype)
    lo_ref[...] = _dot_split(h2, rw_ref) + rb_ref[...]


def _merge2(m, x, mods3, lw, nbb, alpha):
    n, d = x.shape
    full = lambda shape: pl.BlockSpec(shape, lambda i: (0,) * len(shape))
    row = pl.BlockSpec((ROW_BLK, d), lambda i: (i, 0))
    return pl.pallas_call(
        functools.partial(_merge2_kernel, alpha),
        grid=(n // ROW_BLK,),
        in_specs=[row, full((d, d)), row, _mod_spec(2, d, nbb), full((1, d)), full((1, d)),
                  _mod_spec(3, d, nbb), _mod_spec(4, d, nbb), full((2, d, LANE)), full((1, LANE))],
        out_specs=[row, row, pl.BlockSpec((ROW_BLK, LANE), lambda i: (i, 0))],
        out_shape=[jax.ShapeDtypeStruct((n, d), F32), jax.ShapeDtypeStruct((n, d), F32),
                   jax.ShapeDtypeStruct((n, LANE), F32)],
        compiler_params=_cparams(("parallel",)),
        name="merge_out_ln",
    )(m, lw["w_merge"], x, mods3, lw["ln1_g"], lw["ln1_b"], mods3, mods3, lw["router_w"], lw["router_b"])


def _route_kernel(lo_ref, gate_ref, idx_ref, rank_ref, counts_ref, carry_ref):
    x = lo_ref[...]
    lanes = lax.broadcasted_iota(jnp.int32, x.shape, 1).astype(F32)
    x = jnp.where(lanes < N_EXPERTS, x, -jnp.inf)
    vals, idxs = [], []
    for _ in range(TOP_K):
        m = jnp.max(x, axis=-1, keepdims=True)
        sel = jnp.min(jnp.where(x == m, lanes, float(LANE)), axis=-1, keepdims=True)
        vals.append(m)
        idxs.append(sel)
        x = jnp.where(lanes == sel, -jnp.inf, x)
    es = [jnp.exp(v - vals[0]) for v in vals]
    tot = es[0] + es[1] + es[2] + es[3]
    @pl.when(pl.program_id(0) == 0)
    def _():
        carry_ref[...] = jnp.zeros_like(carry_ref)

    onehot = jnp.zeros(lo_ref.shape, F32)
    for j in range(TOP_K):
        onehot = onehot + jnp.where(lanes == idxs[j], 1.0, 0.0)
    nr = lo_ref.shape[0]
    below = (lax.broadcasted_iota(jnp.int32, (nr, nr), 1) < lax.broadcasted_iota(jnp.int32, (nr, nr), 0))
    prefix = jnp.dot(below.astype(BF16), onehot.astype(BF16), preferred_element_type=F32) + carry_ref[...]
    gate = jnp.zeros(lo_ref.shape, F32)
    idx = jnp.zeros(lo_ref.shape, F32)
    rank = jnp.zeros(lo_ref.shape, F32)
    for j in range(TOP_K):
        gate = jnp.where(lanes == j, es[j] / tot, gate)
        idx = jnp.where(lanes == j, idxs[j], idx)
        rank_j = jnp.sum(jnp.where(lanes == idxs[j], prefix, 0.0), axis=-1, keepdims=True)
        rank = jnp.where(lanes == j, rank_j, rank)
    gate_ref[...] = gate
    idx_ref[...] = idx.astype(jnp.int32)
    rank_ref[...] = rank.astype(jnp.int32)
    carry_ref[...] = carry_ref[...] + jnp.sum(onehot, axis=0, keepdims=True)
    counts_ref[...] = carry_ref[...].astype(jnp.int32)


def _route(logits):
    n = logits.shape[0]
    row = pl.BlockSpec((ROW_BLK, LANE), lambda i: (i, 0))
    return pl.pallas_call(
        _route_kernel,
        grid=(n // ROW_BLK,),
        in_specs=[row],
        out_specs=[row, row, row, pl.BlockSpec((1, LANE), lambda i: (0, 0))],
        out_shape=[jax.ShapeDtypeStruct((n, LANE), F32), jax.ShapeDtypeStruct((n, LANE), jnp.int32),
                   jax.ShapeDtypeStruct((n, LANE), jnp.int32), jax.ShapeDtypeStruct((1, LANE), jnp.int32)],
        scratch_shapes=[pltpu.VMEM((1, LANE), F32)],
        compiler_params=_cparams(("arbitrary",)),
        name="moe_route",
    )(logits)


def _expert_kernel(be_ref, nu_ref, x_ref, w1_ref, b1_ref, w2_ref, b2_ref, sel_ref, o_ref, w1b_ref, w2b_ref):
    i = pl.program_id(0)
    used = i < nu_ref[0]

    @pl.when(jnp.logical_and(used, jnp.logical_or(i == 0, be_ref[i] != be_ref[jnp.maximum(i - 1, 0)])))
    def _():
        w1b_ref[...] = w1_ref[0].astype(BF16)
        w2b_ref[...] = w2_ref[0].astype(BF16)

    @pl.when(used)
    def _():
        z = jnp.dot(x_ref[...].astype(BF16), w1b_ref[...], preferred_element_type=F32) + b1_ref[0]
        z_glu = jnp.minimum(z, SWIGLU_LIMIT)
        z_lin = jnp.clip(pltpu.roll(z, z.shape[1] - 1, 1), -SWIGLU_LIMIT, SWIGLU_LIMIT)
        act = z_glu * _sigmoid(SWIGLU_ALPHA * z_glu) * (z_lin + 1.0)
        lanes = lax.broadcasted_iota(jnp.int32, act.shape, 1)
        act = jnp.where(lanes % 2 == 0, act, 0.0).astype(BF16)
        act = jnp.dot(act, sel_ref[...], preferred_element_type=F32).astype(BF16)
        o_ref[...] = jnp.dot(act, w2b_ref[...], preferred_element_type=F32) + b2_ref[0]

    @pl.when(jnp.logical_not(used))
    def _():
        o_ref[...] = jnp.zeros_like(o_ref)


def _experts(xs, block_e, n_used, lw):
    n_slots, d = xs.shape
    n_blocks = n_slots // MOE_BLOCK
    f2 = 2 * D_EXPERT
    sel = jnp.asarray(np.arange(f2)[:, None] == 2 * np.arange(D_EXPERT)[None, :], BF16)
    grid_spec = pltpu.PrefetchScalarGridSpec(
        num_scalar_prefetch=2,
        grid=(n_blocks,),
        in_specs=[pl.BlockSpec((MOE_BLOCK, d), lambda i, be, nu: (i, 0)),
                  pl.BlockSpec((1, d, f2), lambda i, be, nu: (be[i], 0, 0)),
                  pl.BlockSpec((1, 1, f2), lambda i, be, nu: (be[i], 0, 0)),
                  pl.BlockSpec((1, D_EXPERT, d), lambda i, be, nu: (be[i], 0, 0)),
                  pl.BlockSpec((1, 1, d), lambda i, be, nu: (be[i], 0, 0)),
                  pl.BlockSpec((f2, D_EXPERT), lambda i, be, nu: (0, 0))],
        out_specs=pl.BlockSpec((MOE_BLOCK, d), lambda i, be, nu: (i, 0)),
        scratch_shapes=[pltpu.VMEM((d, f2), BF16), pltpu.VMEM((D_EXPERT, d), BF16)],
    )
    return pl.pallas_call(
        _expert_kernel,
        grid_spec=grid_spec,
        out_shape=jax.ShapeDtypeStruct((n_slots, d), F32),
        compiler_params=_cparams(("arbitrary",)),
        name="moe_experts",
    )(block_e, n_used, xs, lw["moe_w1"], lw["moe_b1"], lw["moe_w2"], lw["moe_b2"], sel)


def _moe(h2, idx, rank, counts, lw):
    n, d = h2.shape
    a = n * TOP_K
    counts = counts[0, :N_EXPERTS]
    padded = (counts + MOE_BLOCK - 1) // MOE_BLOCK * MOE_BLOCK
    pad_end = jnp.cumsum(padded)
    pad_start = pad_end - padded
    dest = pad_start[idx[:, :TOP_K]] + rank[:, :TOP_K]
    n_blocks = -(-a // MOE_BLOCK) + N_EXPERTS
    n_slots = n_blocks * MOE_BLOCK
    slot_tok = jnp.zeros(n_slots, jnp.int32).at[dest.reshape(a)].set(jnp.arange(a, dtype=jnp.int32) // TOP_K)
    block_start = jnp.arange(n_blocks, dtype=jnp.int32) * MOE_BLOCK
    block_e = jnp.minimum(jnp.sum(block_start[:, None] >= pad_end[None, :], axis=1), N_EXPERTS - 1).astype(jnp.int32)
    xs = h2[slot_tok]
    n_used = (pad_end[N_EXPERTS - 1:] // MOE_BLOCK).astype(jnp.int32)
    ys = _experts(xs, block_e + lw["moe_row0"], n_used, lw)
    return [ys[dest[:, j]] for j in range(TOP_K)]


def _final_kernel(alpha, x_ref, y0_ref, y1_ref, y2_ref, y3_ref, gate_ref, g2_ref, lg_ref, lb_ref, sh_ref, sc_ref,
                  x2_ref, h_ref):
    gate = gate_ref[...]
    y = y0_ref[...] * gate[:, 0:1]
    for j, y_ref in enumerate((y1_ref, y2_ref, y3_ref), start=1):
        y = y + y_ref[...] * gate[:, j:j + 1]
    x2 = _ln(alpha * x_ref[...] + g2_ref[0] * y) * lg_ref[...] + lb_ref[...]
    x2_ref[...] = x2
    h_ref[...] = (_ln(x2) * (1.0 + sc_ref[0]) + sh_ref[0]).astype(h_ref.dtype)


def _final(x1, y_rows, gate, mods3, mods3_next, lw, nbb, alpha):
    n, d = x1.shape
    assert len(y_rows) == TOP_K == 4
    full = lambda shape: pl.BlockSpec(shape, lambda i: (0,) * len(shape))
    row = pl.BlockSpec((ROW_BLK, d), lambda i: (i, 0))
    return pl.pallas_call(
        functools.partial(_final_kernel, alpha),
        grid=(n // ROW_BLK,),
        in_specs=[row, row, row, row, row, pl.BlockSpec((ROW_BLK, LANE), lambda i: (i, 0)),
                  _mod_spec(5, d, nbb), full((1, d)), full((1, d)), _mod_spec(0, d, nbb), _mod_spec(1, d, nbb)],
        out_specs=[row, row],
        out_shape=[jax.ShapeDtypeStruct((n, d), F32), jax.ShapeDtypeStruct((n, d), BF16)],
        compiler_params=_cparams(("parallel",)),
        name="moe_combine_residual_ln",
    )(x1, *y_rows, gate, mods3, lw["ln2_g"], lw["ln2_b"], mods3_next, mods3_next)


def _rope_tables(ctx_len, seq):
    n_freq = HEAD_DIM // 4
    t = np.arange(seq)
    inv = np.power(ROPE_BASE, -np.arange(n_freq, dtype=np.float32) / n_freq).astype(np.float32)
    row = (t // GRID_W).astype(np.float32)
    col = (t % GRID_W).astype(np.float32)
    ang = jnp.asarray(np.concatenate([row[:, None] * inv, col[:, None] * inv], -1))
    cos = jnp.cos(ang)
    sin = jnp.sin(ang)
    cos_h = jnp.concatenate([cos, cos], -1)
    sin_h = jnp.concatenate([-sin, sin], -1)
    cos_t = jnp.concatenate([jnp.ones((ctx_len, HEAD_DIM), F32), cos_h], 0)
    sin_t = jnp.concatenate([jnp.zeros((ctx_len, HEAD_DIM), F32), sin_h], 0)
    return jnp.tile(cos_t, (1, LANE // HEAD_DIM)), jnp.tile(sin_t, (1, LANE // HEAD_DIM))


def _to_scan_layout(x, batch, t_len):
    return x.reshape(x.shape[:-2] + (batch, t_len, x.shape[-1]))


def _pad_dir_lora(w):
    z = jnp.zeros_like(w[0])
    return jnp.stack([_split_weight(jnp.concatenate([w[0], z], 0)), _split_weight(jnp.concatenate([z, w[1]], 0))])


def kernel(x, c, ctx, c_ctx, w_mod, b_mod, w_in, rw_mu, rw_w0, rw_w_lora, rw_a0, rw_a_lora, rw_g_lora, rw_k_k, rw_k_a, rw_r_k, rw_gn_w, rw_gn_b, w_out_rw, na_rpb, w_out_na, sc_conv, w_out_sc, b_gate, w_merge, ln1_g, ln1_b, router_w, router_b, moe_w1, moe_b1, moe_w2, moe_b2, ln2_g, ln2_b):
    batch, seq, d = x.shape
    ctx_len = ctx.shape[1]
    depth = w_in.shape[0]
    t_len = ctx_len + seq
    n = batch * t_len
    nbb = t_len // ROW_BLK
    assert batch <= 8 and ctx_len == ROW_BLK and seq % (NA_GROUP * GRID_W) == 0
    alpha = (2 * depth) ** 0.25

    cc = jnp.zeros((16, d), F32).at[:batch].set(c).at[8].set(c_ctx)
    mods = _modulation(cc, w_mod, b_mod)
    mods3 = [mods[l].reshape(16, 1, 6 * d) for l in range(depth)]

    cos_t, sin_t = _rope_tables(ctx_len, seq)
    seg = (np.arange(RW_WIDTH)[:, None] // HEAD_DIM == np.arange(LANE)[None, :]).astype(np.float32)
    consts = dict(cos=cos_t, sin=sin_t, seg=jnp.asarray(seg, BF16), seg_t=jnp.asarray(seg.T, BF16))
    tables = _na_tables(seq // GRID_W)

    o_na = RW_COLS_PAD
    o_sc = o_na + 3 * NA_WIDTH
    o_gate = o_sc + 3 * SC_WIDTH
    pad_cols = RW_COLS_PAD - RW_COLS

    xs = jnp.concatenate([ctx, x], axis=1).reshape(n, d)
    h = _lnmod(xs, mods3[0], nbb)
    for l in range(depth):
        w_in_p = jnp.concatenate([w_in[l, :, :RW_COLS], jnp.zeros((d, pad_cols), F32), w_in[l, :, RW_COLS:]],
                                 axis=1).astype(BF16)
        lw = dict(
            mu=jnp.pad(rw_mu[l], ((0, 0), (0, pad_cols))), w0=rw_w0[l], w_lora=_pad_dir_lora(rw_w_lora[l]),
            a0=rw_a0[l], a_lora=_pad_dir_lora(rw_a_lora[l]),
            g_lora=_split_weight(jnp.pad(rw_g_lora[l], ((0, GATE_LORA_PAD - GATE_LORA), (0, 0)))),
            k_k=rw_k_k[l].reshape(1, -1), k_a=rw_k_a[l].reshape(1, -1), r_k=rw_r_k[l].reshape(1, -1),
            gn_w=rw_gn_w[l].reshape(1, -1), gn_b=rw_gn_b[l].reshape(1, -1),
            w_out_rw=w_out_rw[l].astype(BF16), w_out_na=w_out_na[l].astype(BF16),
            w_out_sc=w_out_sc[l].astype(BF16), b_gate=b_gate[l].reshape(1, -1),
            w_merge=w_merge[l].astype(BF16), ln1_g=ln1_g[l].reshape(1, -1), ln1_b=ln1_b[l].reshape(1, -1),
            router_w=_split_weight(jnp.pad(router_w[l], ((0, 0), (0, LANE - N_EXPERTS)))),
            router_b=jnp.pad(router_b[l], (0, LANE - N_EXPERTS)).reshape(1, -1),
            moe_row0=l * N_EXPERTS,
            moe_w1=moe_w1.reshape((depth * N_EXPERTS,) + moe_w1.shape[2:]),
            moe_b1=moe_b1.reshape(depth * N_EXPERTS, 1, -1),
            moe_w2=moe_w2.reshape((depth * N_EXPERTS,) + moe_w2.shape[2:]),
            moe_b2=moe_b2.reshape(depth * N_EXPERTS, 1, -1),
            ln2_g=ln2_g[l].reshape(1, -1), ln2_b=ln2_b[l].reshape(1, -1),
        )
        proj = _matmul(h, w_in_p)
        r_s, kap, v, w, kd, b, bonus, g = _rwkv_prepare(proj, lw, consts, nbb)
        tl = functools.partial(_to_scan_layout, batch=batch, t_len=t_len)
        wkv_f, wkv_b = _wkv_scan(tl(r_s), tl(kap), tl(v), tl(w), tl(kd), tl(b), ctx_len)
        y_rw = _rwkv_out(wkv_f.reshape(n, RW_WIDTH), wkv_b.reshape(n, RW_WIDTH), bonus, g, lw, consts)
        y_na = _na_attention(proj, _na_bias(na_rpb[l]), tables, o_na, batch, t_len, ctx_len)
        y_sc = _short_conv(proj, jnp.pad(sc_conv[l], ((0, 5), (0, 0))), o_sc, nbb)
        m = _merge1(y_rw, y_na, y_sc, lw, proj, o_gate, d)
        x1, h2, logits = _merge2(m, xs, mods3[l], lw, nbb, alpha)
        gate, idx, rank, counts = _route(logits)
        y_rows = _moe(h2, idx, rank, counts, lw)
        xs, h = _final(x1, y_rows, gate, mods3[l], mods3[min(l + 1, depth - 1)], lw, nbb, alpha)
    return xs.reshape(batch, t_len, d)[:, ctx_len:]
```

```python
import functools
import math

import numpy as np
import jax
import jax.numpy as jnp
from jax import lax
from jax.experimental import pallas as pl
from jax.experimental.pallas import tpu as pltpu

GRID_W = 64
RW_HEADS = 16
HEAD_DIM = 64
RW_WIDTH = RW_HEADS * HEAD_DIM
DECAY_LORA = 64
ICLR_LORA = 64
GATE_LORA = 160
GATE_LORA_PAD = 256
GN_EPS = 64e-5
NA_HEADS = 16
NA_WIDTH = NA_HEADS * HEAD_DIM
NA_WIN_ROWS = 8
NA_WIN_COLS = 16
SC_WIDTH = 1024
RW_COLS = 3 * RW_WIDTH + 2 * DECAY_LORA + 2 * ICLR_LORA + GATE_LORA
RW_COLS_PAD = 3 * RW_WIDTH + 2 * DECAY_LORA + 2 * ICLR_LORA + GATE_LORA_PAD
N_EXPERTS = 32
TOP_K = 4
D_EXPERT = 512
SWIGLU_ALPHA = 1.702
SWIGLU_LIMIT = 7.0
MOE_BLOCK = 256
ROPE_BASE = 10000.0
LN_EPS = 1e-6
NEG_INF = -1e30

LANE = 128
ROW_BLK = 256
MM_TM = 2048
MM_TN = 512
SCAN_TT = 32
SCAN_VSPLIT = 2
SCAN_NACC = 1
SCAN_RELAYOUT_UNROLL = 4
NA_GROUP = 4
NA_KROWS = 12
VMEM_LIMIT = 56 * 1024 * 1024

F32 = jnp.float32
BF16 = jnp.bfloat16


def _cparams(sem):
    return pltpu.CompilerParams(dimension_semantics=sem, vmem_limit_bytes=VMEM_LIMIT)


def _ln(x):
    mu = jnp.mean(x, axis=-1, keepdims=True)
    xc = x - mu
    var = jnp.mean(xc * xc, axis=-1, keepdims=True)
    return xc * lax.rsqrt(var + LN_EPS)


def _sigmoid(x):
    return 1.0 / (1.0 + jnp.exp(-x))


def _mod_kernel(c_ref, w_ref, b_ref, o_ref):
    c = c_ref[...]
    s = (c * _sigmoid(c)).astype(BF16)
    o_ref[0] = jnp.dot(s, w_ref[0].astype(BF16), preferred_element_type=F32) + b_ref[0]


def _modulation(cc, w_mod, b_mod):
    depth, d, n = w_mod.shape
    tn = 1024
    return pl.pallas_call(
        _mod_kernel,
        grid=(depth, n // tn),
        in_specs=[pl.BlockSpec((16, d), lambda l, j: (0, 0)),
                  pl.BlockSpec((1, d, tn), lambda l, j: (l, 0, j)),
                  pl.BlockSpec((1, 1, tn), lambda l, j: (l, 0, j))],
        out_specs=pl.BlockSpec((1, 16, tn), lambda l, j: (l, 0, j)),
        out_shape=jax.ShapeDtypeStruct((depth, 16, n), F32),
        compiler_params=_cparams(("arbitrary", "arbitrary")),
        name="modulation",
    )(cc, w_mod, b_mod.reshape(depth, 1, n))


def _mod_spec(chunk, d, nblk_per_batch):
    def imap(i):
        row = jnp.where(i % nblk_per_batch == 0, 8, i // nblk_per_batch)
        return (row, 0, chunk)
    return pl.BlockSpec((1, 1, d), imap)


def _lnmod_kernel(x_ref, sh_ref, sc_ref, o_ref):
    o_ref[...] = (_ln(x_ref[...]) * (1.0 + sc_ref[0]) + sh_ref[0]).astype(o_ref.dtype)


def _lnmod(x, mods3, nbb):
    n, d = x.shape
    return pl.pallas_call(
        _lnmod_kernel,
        grid=(n // ROW_BLK,),
        in_specs=[pl.BlockSpec((ROW_BLK, d), lambda i: (i, 0)),
                  _mod_spec(0, d, nbb), _mod_spec(1, d, nbb)],
        out_specs=pl.BlockSpec((ROW_BLK, d), lambda i: (i, 0)),
        out_shape=jax.ShapeDtypeStruct((n, d), BF16),
        compiler_params=_cparams(("parallel",)),
        name="ln_mod",
    )(x, mods3, mods3)


def _mm_kernel(a_ref, b_ref, o_ref):
    o_ref[...] = jnp.dot(a_ref[...], b_ref[...], preferred_element_type=F32).astype(o_ref.dtype)


def _matmul(a, b, out_dtype=F32):
    m, k = a.shape
    _, n = b.shape
    tm = MM_TM if m % MM_TM == 0 else ROW_BLK
    return pl.pallas_call(
        _mm_kernel,
        grid=(m // tm, n // MM_TN),
        in_specs=[pl.BlockSpec((tm, k), lambda i, j: (i, 0)),
                  pl.BlockSpec((k, MM_TN), lambda i, j: (0, j))],
        out_specs=pl.BlockSpec((tm, MM_TN), lambda i, j: (i, j)),
        out_shape=jax.ShapeDtypeStruct((m, n), out_dtype),
        compiler_params=_cparams(("parallel", "arbitrary")),
        name="in_proj",
    )(a, b)


def _halo_specs(width, col_blk, nblk):
    per = ROW_BLK // 8

    def prev_map(i):
        return (jnp.maximum(i * per - 1, 0), col_blk)

    def next_map(i):
        return (jnp.minimum((i + 1) * per, nblk * per - 1), col_blk)

    return (pl.BlockSpec((8, width), prev_map), pl.BlockSpec((8, width), next_map))


def _shifted(p, prev8, next8, i, nbb):
    pos = i % nbb
    has_prev = pos >= 2
    has_next = jnp.logical_and(pos >= 1, pos <= nbb - 2)
    rows = lax.broadcasted_iota(jnp.int32, p.shape, 0)
    prow = jnp.where(has_prev, prev8[7:8, :], 0.0)
    nrow = jnp.where(has_next, next8[0:1, :], 0.0)
    p_prev = jnp.where(rows == 0, prow, pltpu.roll(p, 1, 0))
    p_next = jnp.where(rows == p.shape[0] - 1, nrow, pltpu.roll(p, p.shape[0] - 1, 0))
    return p_prev, p_next


def _split3(x):
    hi = x.astype(BF16)
    r = x - hi.astype(F32)
    mid = r.astype(BF16)
    lo = (r - mid.astype(F32)).astype(BF16)
    return hi, mid, lo


def _dot_indicator(x, w):
    hi, mid, lo = _split3(x)
    return (jnp.dot(hi, w, preferred_element_type=F32) + jnp.dot(mid, w, preferred_element_type=F32)
            + jnp.dot(lo, w, preferred_element_type=F32))


def _dot_split(x, w_ref):
    hi, mid, _ = _split3(x)
    w_hi = w_ref[0]
    return (jnp.dot(hi, w_hi, preferred_element_type=F32) + jnp.dot(mid, w_hi, preferred_element_type=F32)
            + jnp.dot(hi, w_ref[1], preferred_element_type=F32))


def _split_weight(w):
    hi = w.astype(BF16)
    return jnp.stack([hi, (w - hi.astype(F32)).astype(BF16)])


def _seg_sum(x, e_ref, et_ref):
    return _dot_indicator(_dot_indicator(x, e_ref[...]), et_ref[...])


def _swap_halves(x):
    lanes = lax.broadcasted_iota(jnp.int32, x.shape, 1)
    first = (lanes % HEAD_DIM) < (HEAD_DIM // 2)
    n = x.shape[1]
    return jnp.where(first, pltpu.roll(x, n - HEAD_DIM // 2, 1), pltpu.roll(x, HEAD_DIM // 2, 1))


def _prep_kernel(nbb, p_ref, pp_ref, pn_ref, mu_ref, w0_ref, wl_ref, a0_ref, al_ref, gl_ref,
                 kk_ref, ka_ref, rk_ref, cos_ref, sin_ref, e_ref, et_ref,
                 r_o, kap_o, v_o, w_o, kd_o, b_o, bonus_o, g_o):
    i = pl.program_id(0)
    c = RW_WIDTH
    p = p_ref[...]
    p_prev, p_next = _shifted(p, pp_ref[...], pn_ref[...], i, nbb)
    pm = p + mu_ref[0:1, :] * (p_prev - p) + mu_ref[1:2, :] * (p_next - p)
    r, k, v = pm[:, :c], pm[:, c:2 * c], pm[:, 2 * c:3 * c]
    o = 3 * c
    wl = jnp.tanh(pm[:, o:o + 2 * DECAY_LORA])
    o += 2 * DECAY_LORA
    al = pm[:, o:o + 2 * ICLR_LORA]
    o += 2 * ICLR_LORA
    gl = _sigmoid(pm[:, o:o + GATE_LORA_PAD])
    g_o[...] = _dot_split(gl, gl_ref)
    kk = k * kk_ref[...]
    ss = _seg_sum(kk * kk, e_ref, et_ref)
    kk = kk / jnp.maximum(jnp.sqrt(ss), 1e-12)
    cos = jnp.concatenate([cos_ref[...]] * (c // LANE), axis=1)
    sin = jnp.concatenate([sin_ref[...]] * (c // LANE), axis=1)

    def rope(x):
        return x * cos + _swap_halves(x) * sin

    kap = rope(kk)
    r_o[...] = rope(r)
    kap_o[...] = kap
    v_o[...] = v
    k_sum = jnp.zeros_like(k)
    for d in range(2):
        w_raw = w0_ref[d:d + 1, :] + _dot_split(wl, wl_ref.at[d])
        w_o[d] = jnp.exp(-math.exp(-0.5) * _sigmoid(w_raw))
        a = _sigmoid(a0_ref[d:d + 1, :] + _dot_split(al, al_ref.at[d]))
        kd = k * (1.0 + (a - 1.0) * ka_ref[...])
        k_sum = k_sum + kd
        kd_o[d] = rope(kd)
        b_o[d] = kap * a
    bonus_o[...] = _seg_sum(r * k_sum * rk_ref[...], e_ref, et_ref) * v


def _rwkv_prepare(proj, lw, consts, nbb):
    n = proj.shape[0]
    c = RW_WIDTH
    nblk = n // ROW_BLK
    prev_spec, next_spec = _halo_specs(RW_COLS_PAD, 0, nblk)
    full = lambda shape: pl.BlockSpec(shape, lambda i: (0,) * len(shape))
    row_c = pl.BlockSpec((ROW_BLK, c), lambda i: (i, 0))
    row_2c = pl.BlockSpec((2, ROW_BLK, c), lambda i: (0, i, 0))
    tab = pl.BlockSpec((ROW_BLK, LANE), lambda i: (i % nbb, 0))
    outs = pl.pallas_call(
        functools.partial(_prep_kernel, nbb),
        grid=(nblk,),
        in_specs=[pl.BlockSpec((ROW_BLK, RW_COLS_PAD), lambda i: (i, 0)), prev_spec, next_spec,
                  full((2, RW_COLS_PAD)), full((2, c)), full((2, 2, 2 * DECAY_LORA, c)), full((2, c)),
                  full((2, 2, 2 * ICLR_LORA, c)), full((2, GATE_LORA_PAD, c)), full((1, c)), full((1, c)),
                  full((1, c)), tab, tab, full((c, LANE)), full((LANE, c))],
        out_specs=[row_c, row_c, row_c, row_2c, row_2c, row_2c, row_c, row_c],
        out_shape=[jax.ShapeDtypeStruct((n, c), F32)] * 3 + [jax.ShapeDtypeStruct((2, n, c), F32)] * 3
        + [jax.ShapeDtypeStruct((n, c), F32)] * 2,
        compiler_params=_cparams(("parallel",)),
        name="rwkv_prepare",
    )(proj, proj, proj, lw["mu"], lw["w0"], lw["w_lora"], lw["a0"], lw["a_lora"], lw["g_lora"],
      lw["k_k"], lw["k_a"], lw["r_k"], consts["cos"], consts["sin"], consts["seg"], consts["seg_t"])
    return outs


def _scan_kernel(rf_ref, kapf_ref, vf_ref, rb_ref, kapb_ref, vb_ref, wf_ref, kdf_ref, bf_ref, wb_ref, kdb_ref, bb_ref,
                 of_ref, ob_ref, s_ref, vec_ref, out_ref):
    nb, tt = rf_ref.shape[0], rf_ref.shape[1]
    n_pair = rf_ref.shape[2] // (2 * HEAD_DIM)
    nl = nb * n_pair
    vh = HEAD_DIM // SCAN_VSPLIT

    @pl.when(pl.program_id(0) == 0)
    def _():
        s_ref[...] = jnp.zeros_like(s_ref)

    def to_lanes(i, carry):
        slab = lambda x: jnp.concatenate([x[:, p * 2 * HEAD_DIM:(p + 1) * 2 * HEAD_DIM] for p in range(n_pair)], axis=0)
        tb = tt - 1 - i
        pairs = ((kapf_ref[:, i, :], kapb_ref[:, tb, :]), (rf_ref[:, i, :], rb_ref[:, tb, :]),
                 (wf_ref[0, :, i, :], wb_ref[0, :, tb, :]), (kdf_ref[0, :, i, :], kdb_ref[0, :, tb, :]),
                 (bf_ref[0, :, i, :], bb_ref[0, :, tb, :]), (vf_ref[:, i, :], vb_ref[:, tb, :]))
        for j, (xf, xb) in enumerate(pairs):
            res = jnp.concatenate([slab(xf), slab(xb)], axis=0).T
            vec_ref[0, i, j] = res[:HEAD_DIM]
            vec_ref[1, i, j] = res[HEAD_DIM:]
        return carry

    lax.fori_loop(0, tt, to_lanes, 0, unroll=SCAN_RELAYOUT_UNROLL)

    def add_to(acc, j, term):
        acc[j] = term if acc[j] is None else acc[j] + term

    def total(acc):
        return functools.reduce(lambda a, b: a + b, acc)

    skk0 = []
    for g in range(2):
        acc = [None] * SCAN_NACC
        for k in range(HEAD_DIM):
            add_to(acc, k % SCAN_NACC, s_ref[g, k] * vec_ref[g, 0, 0, k:k + 1, :])
        skk0.append(total(acc))

    def step(i, skks):
        i_next = jnp.minimum(i + 1, tt - 1)
        skks_next = []
        for g in range(2):
            parts = []
            for h in range(SCAN_VSPLIT):
                rows = slice(h * vh, (h + 1) * vh)
                skk_h = skks[g][rows]
                v = vec_ref[g, i, 5, rows]
                acc_out = [None] * SCAN_NACC
                acc_next = [None] * SCAN_NACC
                for k in range(HEAD_DIM):
                    s_new = (s_ref[g, k, rows] * vec_ref[g, i, 2, k:k + 1, :] - skk_h * vec_ref[g, i, 4, k:k + 1, :]
                             + v * vec_ref[g, i, 3, k:k + 1, :])
                    s_ref[g, k, rows] = s_new
                    add_to(acc_out, k % SCAN_NACC, s_new * vec_ref[g, i, 1, k:k + 1, :])
                    add_to(acc_next, k % SCAN_NACC, s_new * vec_ref[g, i_next, 0, k:k + 1, :])
                out_ref[g, i, rows] = total(acc_out)
                parts.append(total(acc_next))
            skks_next.append(jnp.concatenate(parts, axis=0))
        return tuple(skks_next)

    lax.fori_loop(0, tt, step, tuple(skk0))

    def to_tokens(i, carry):
        back = jnp.concatenate([out_ref[0, i], out_ref[1, i]], axis=0).T
        unslab = lambda x: jnp.concatenate([x[p * nb:(p + 1) * nb] for p in range(n_pair)], axis=1)
        of_ref[:, i, :] = unslab(back[:nl])
        ob_ref[:, tt - 1 - i, :] = unslab(back[nl:])
        return carry

    lax.fori_loop(0, tt, to_tokens, 0, unroll=SCAN_RELAYOUT_UNROLL)


def _wkv_scan(r, kap, v, w, kd, b, ctx_len):
    batch, t, width = r.shape
    lanes = batch * width // HEAD_DIM
    nb = t // SCAN_TT
    n_ctx_blk = ctx_len // SCAN_TT

    def mirrored(j):
        return jnp.where(j < n_ctx_blk, n_ctx_blk - 1 - j, nb - 1 - (j - n_ctx_blk))

    fwd = pl.BlockSpec((batch, SCAN_TT, width), lambda j: (0, j, 0))
    bwd = pl.BlockSpec((batch, SCAN_TT, width), lambda j: (0, mirrored(j), 0))
    fwd_dir = pl.BlockSpec((1, batch, SCAN_TT, width), lambda j: (0, 0, j, 0))
    bwd_dir = pl.BlockSpec((1, batch, SCAN_TT, width), lambda j: (1, 0, mirrored(j), 0))
    out = jax.ShapeDtypeStruct((batch, t, width), F32)
    return pl.pallas_call(
        _scan_kernel,
        grid=(nb,),
        in_specs=[fwd, fwd, fwd, bwd, bwd, bwd, fwd_dir, fwd_dir, fwd_dir, bwd_dir, bwd_dir, bwd_dir],
        out_specs=[fwd, bwd],
        out_shape=[out, out],
        scratch_shapes=[pltpu.VMEM((2, HEAD_DIM, HEAD_DIM, lanes), F32),
                        pltpu.VMEM((2, SCAN_TT, 6, HEAD_DIM, lanes), F32),
                        pltpu.VMEM((2, SCAN_TT, HEAD_DIM, lanes), F32)],
        compiler_params=_cparams(("arbitrary",)),
        name="wkv_scan",
    )(r, kap, v, r, kap, v, w, kd, b, w, kd, b)


def _rwkv_out_kernel(wkv_f_ref, wkv_b_ref, bonus_ref, g_ref, gw_ref, gb_ref, e_ref, et_ref, o_ref):
    x = wkv_f_ref[...] + wkv_b_ref[...]
    inv_n = 1.0 / HEAD_DIM
    mu = _seg_sum(x, e_ref, et_ref) * inv_n
    xc = x - mu
    var = _seg_sum(xc * xc, e_ref, et_ref) * inv_n
    y = xc * lax.rsqrt(var + GN_EPS) * gw_ref[...] + gb_ref[...]
    o_ref[...] = ((y + bonus_ref[...]) * g_ref[...]).astype(o_ref.dtype)


def _rwkv_out(wkv_f, wkv_b, bonus, g, lw, consts):
    n, c = bonus.shape
    full = lambda shape: pl.BlockSpec(shape, lambda i: (0,) * len(shape))
    row_c = pl.BlockSpec((ROW_BLK, c), lambda i: (i, 0))
    return pl.pallas_call(
        _rwkv_out_kernel,
        grid=(n // ROW_BLK,),
        in_specs=[row_c, row_c, row_c, row_c,
                  full((1, c)), full((1, c)), full((c, LANE)), full((LANE, c))],
        out_specs=row_c,
        out_shape=jax.ShapeDtypeStruct((n, c), BF16),
        compiler_params=_cparams(("parallel",)),
        name="rwkv_out",
    )(wkv_f, wkv_b, bonus, g, lw["gn_w"], lw["gn_b"], consts["seg"], consts["seg_t"])


def _conv_kernel(nbb, bg_ref, cg_ref, x_ref, cgp_ref, xp_ref, cgn_ref, xn_ref, w_ref, o_ref):
    i = pl.program_id(0)
    u = cg_ref[...] * x_ref[...]
    u_prev, u_next = _shifted(u, cgp_ref[...] * xp_ref[...], cgn_ref[...] * xn_ref[...], i, nbb)
    y = w_ref[0:1, :] * u_prev + w_ref[1:2, :] * u + w_ref[2:3, :] * u_next
    o_ref[...] = (bg_ref[...] * y).astype(o_ref.dtype)


def _short_conv(proj, conv_w, col0, nbb):
    n = proj.shape[0]
    cw = MM_TN
    nc = SC_WIDTH // cw
    assert col0 % cw == 0
    cb = col0 // cw
    nblk = n // ROW_BLK
    blk = lambda s: pl.BlockSpec((ROW_BLK, cw), lambda i, j: (i, cb + s * nc + j))

    def halos(s):
        prev_spec, next_spec = _halo_specs(cw, 0, nblk)
        pm, nm = prev_spec.index_map, next_spec.index_map
        return (pl.BlockSpec((8, cw), lambda i, j: (pm(i)[0], cb + s * nc + j)),
                pl.BlockSpec((8, cw), lambda i, j: (nm(i)[0], cb + s * nc + j)))

    cg_prev, cg_next = halos(1)
    x_prev, x_next = halos(2)
    return pl.pallas_call(
        functools.partial(_conv_kernel, nbb),
        grid=(nblk, nc),
        in_specs=[blk(0), blk(1), blk(2), cg_prev, x_prev, cg_next, x_next,
                  pl.BlockSpec((8, cw), lambda i, j: (0, j))],
        out_specs=pl.BlockSpec((ROW_BLK, cw), lambda i, j: (i, j)),
        out_shape=jax.ShapeDtypeStruct((n, SC_WIDTH), BF16),
        compiler_params=_cparams(("parallel", "arbitrary")),
        name="short_conv",
    )(proj, proj, proj, proj, proj, proj, proj, conv_w)


def _na_tables(rows):
    kr = min(NA_WIN_ROWS, rows)
    n_groups = rows // NA_GROUP
    krows = min(NA_KROWS, rows)
    n_dr = 2 * NA_WIN_ROWS - 1
    dr = np.full((n_groups, NA_GROUP, krows), n_dr, np.int32)
    bases = []
    for g in range(n_groups):
        r0 = g * NA_GROUP
        base = int(np.clip(np.clip(r0 - kr // 2, 0, rows - kr), 0, rows - krows))
        bases.append(base)
        for rl in range(NA_GROUP):
            r = r0 + rl
            rs = int(np.clip(r - kr // 2, 0, rows - kr))
            for j in range(krows):
                if rs <= base + j < rs + kr:
                    dr[g, rl, j] = base + j - r + NA_WIN_ROWS - 1
            assert (dr[g, rl] < n_dr).sum() == kr
    uniq, table_of = [], []
    for g in range(n_groups):
        for u, gu in enumerate(uniq):
            if np.array_equal(dr[g], dr[gu]):
                table_of.append(u)
                break
        else:
            table_of.append(len(uniq))
            uniq.append(g)
    return dr[uniq], tuple(bases), tuple(table_of)


def _na_kernel(ctx_len, dr, bases, table_of, q_ref, k_ref, v_ref, bias_ref, o_ref):
    scale = HEAD_DIM ** -0.5
    nq = NA_GROUP * GRID_W
    krows = dr.shape[2]
    nk = krows * GRID_W
    first = lax.broadcasted_iota(jnp.int32, (GRID_W, 2 * GRID_W), 1) < GRID_W

    def bias_of(table, h):
        rows = []
        for rl in range(NA_GROUP):
            tiles = [jnp.where(first, bias_ref[h, int(dr[table, rl, j])], bias_ref[h, int(dr[table, rl, j + 1])])
                     for j in range(0, krows, 2)]
            rows.append(jnp.concatenate(tiles, axis=1))
        return jnp.concatenate(rows, axis=0)

    outs_heads = []
    for h in range(2):
        sl = slice(h * HEAD_DIM, (h + 1) * HEAD_DIM)
        q = (q_ref[:, sl] * scale).astype(BF16)
        k = k_ref[:, sl].astype(BF16)
        v = v_ref[:, sl].astype(BF16)
        kc, vc = k[:ctx_len], v[:ctx_len]
        dn = (((1,), (1,)), ((), ()))
        s = lax.dot_general(q[:ctx_len], kc, dn, preferred_element_type=F32)
        s = s - jnp.max(s, axis=-1, keepdims=True)
        e = jnp.exp(s)
        p = (e / jnp.sum(e, axis=-1, keepdims=True)).astype(BF16)
        pieces = [jnp.dot(p, vc, preferred_element_type=F32)]
        biases = [bias_of(t, h) for t in range(dr.shape[0])]
        for g, base in enumerate(bases):
            q0 = ctx_len + g * nq
            k0 = ctx_len + base * GRID_W
            qg = q[q0:q0 + nq]
            s_win = (lax.dot_general(qg, k[k0:k0 + nk], dn, preferred_element_type=F32)
                     + biases[table_of[g]])
            s_ctx = lax.dot_general(qg, kc, dn, preferred_element_type=F32)
            m = jnp.maximum(jnp.max(s_win, axis=-1, keepdims=True), jnp.max(s_ctx, axis=-1, keepdims=True))
            e_win = jnp.exp(s_win - m)
            e_ctx = jnp.exp(s_ctx - m)
            inv = 1.0 / (jnp.sum(e_win, axis=-1, keepdims=True) + jnp.sum(e_ctx, axis=-1, keepdims=True))
            acc = jnp.dot((e_win * inv).astype(BF16), v[k0:k0 + nk], preferred_element_type=F32)
            acc = acc + jnp.dot((e_ctx * inv).astype(BF16), vc, preferred_element_type=F32)
            pieces.append(acc)
        outs_heads.append(jnp.concatenate(pieces, axis=0))
    o_ref[...] = jnp.concatenate(outs_heads, axis=1).astype(o_ref.dtype)


def _na_bias(rpb):
    qc = np.arange(GRID_W)[:, None]
    kc = np.arange(GRID_W)[None, :]
    wstart = np.clip(qc - NA_WIN_COLS // 2, 0, GRID_W - NA_WIN_COLS)
    col_ok = (kc >= wstart) & (kc < wstart + NA_WIN_COLS)
    dc = np.clip(kc - qc + NA_WIN_COLS - 1, 0, 2 * NA_WIN_COLS - 2)
    blocks = jnp.where(col_ok, rpb[:, :, dc], NEG_INF)
    blocks = jnp.concatenate([blocks, jnp.full_like(blocks[:, :1], NEG_INF)], axis=1)
    return jnp.concatenate([blocks, blocks], axis=-1)


def _na_attention(proj, bias, tables, col0, batch, t_len, ctx_len):
    n = proj.shape[0]
    cb = col0 // LANE
    hp = NA_HEADS // 2
    seq = lambda j: pl.BlockSpec((t_len, LANE), lambda h, b: (b, cb + j * hp + h))
    return pl.pallas_call(
        functools.partial(_na_kernel, ctx_len, *tables),
        grid=(hp, batch),
        in_specs=[seq(0), seq(1), seq(2),
                  pl.BlockSpec((2,) + bias.shape[1:], lambda h, b: (h, 0, 0, 0))],
        out_specs=pl.BlockSpec((t_len, LANE), lambda h, b: (b, h)),
        out_shape=jax.ShapeDtypeStruct((n, NA_WIDTH), BF16),
        compiler_params=_cparams(("arbitrary", "arbitrary")),
        name="na_attention",
    )(proj, proj, proj, bias)


def _merge1_kernel(yr_ref, yn_ref, ys_ref, wr_ref, wn_ref, ws_ref, g0_ref, g1_ref, g2_ref,
                   b0_ref, b1_ref, b2_ref, o_ref):
    m = _sigmoid(g0_ref[...] + b0_ref[...]) * jnp.dot(yr_ref[...], wr_ref[...], preferred_element_type=F32)
    m = m + _sigmoid(g1_ref[...] + b1_ref[...]) * jnp.dot(yn_ref[...], wn_ref[...], preferred_element_type=F32)
    m = m + _sigmoid(g2_ref[...] + b2_ref[...]) * jnp.dot(ys_ref[...], ws_ref[...], preferred_element_type=F32)
    o_ref[...] = m.astype(o_ref.dtype)


def _merge1(y_rw, y_na, y_sc, lw, proj, gate_col0, d):
    n, c = y_rw.shape
    tm, tn = 512, MM_TN
    gb = gate_col0 // tn
    nd = d // tn
    ysp = pl.BlockSpec((tm, c), lambda i, j: (i, 0))
    wsp = pl.BlockSpec((c, tn), lambda i, j: (0, j))
    gsp = lambda br: pl.BlockSpec((tm, tn), lambda i, j: (i, gb + br * nd + j))
    bsp = lambda br: pl.BlockSpec((1, tn), lambda i, j: (0, br * nd + j))
    return pl.pallas_call(
        _merge1_kernel,
        grid=(n // tm, nd),
        in_specs=[ysp, ysp, ysp, wsp, wsp, wsp, gsp(0), gsp(1), gsp(2), bsp(0), bsp(1), bsp(2)],
        out_specs=pl.BlockSpec((tm, tn), lambda i, j: (i, j)),
        out_shape=jax.ShapeDtypeStruct((n, d), BF16),
        compiler_params=_cparams(("parallel", "arbitrary")),
        name="merge_branches",
    )(y_rw, y_na, y_sc, lw["w_out_rw"], lw["w_out_na"], lw["w_out_sc"], proj, proj, proj,
      lw["b_gate"], lw["b_gate"], lw["b_gate"])


def _merge2_kernel(alpha, m_ref, w_ref, x_ref, g1_ref, lg_ref, lb_ref, sh_ref, sc_ref, rw_ref, rb_ref,
                   x1_ref, h2_ref, lo_ref):
    mix = jnp.dot(m_ref[...], w_ref[...], preferred_element_type=F32)
    x1 = _ln(alpha * x_ref[...] + g1_ref[0] * mix) * lg_ref[...] + lb_ref[...]
    x1_ref[...] = x1
    h2 = _ln(x1) * (1.0 + sc_ref[0]) + sh_ref[0]
    h2_ref[...] = h2.astype(h2_ref.dtype)
    lo_ref[...] = _dot_split(h2, rw_ref) + rb_ref[...]


def _merge2(m, x, mods3, lw, nbb, alpha):
    n, d = x.shape
    full = lambda shape: pl.BlockSpec(shape, lambda i: (0,) * len(shape))
    row = pl.BlockSpec((ROW_BLK, d), lambda i: (i, 0))
    return pl.pallas_call(
        functools.partial(_merge2_kernel, alpha),
        grid=(n // ROW_BLK,),
        in_specs=[row, full((d, d)), row, _mod_spec(2, d, nbb), full((1, d)), full((1, d)),
                  _mod_spec(3, d, nbb), _mod_spec(4, d, nbb), full((2, d, LANE)), full((1, LANE))],
        out_specs=[row, row, pl.BlockSpec((ROW_BLK, LANE), lambda i: (i, 0))],
        out_shape=[jax.ShapeDtypeStruct((n, d), F32), jax.ShapeDtypeStruct((n, d), F32),
                   jax.ShapeDtypeStruct((n, LANE), F32)],
        compiler_params=_cparams(("parallel",)),
        name="merge_out_ln",
    )(m, lw["w_merge"], x, mods3, lw["ln1_g"], lw["ln1_b"], mods3, mods3, lw["router_w"], lw["router_b"])


def _route_kernel(lo_ref, gate_ref, idx_ref, rank_ref, counts_ref, carry_ref):
    x = lo_ref[...]
    lanes = lax.broadcasted_iota(jnp.int32, x.shape, 1).astype(F32)
    x = jnp.where(lanes < N_EXPERTS, x, -jnp.inf)
    vals, idxs = [], []
    for _ in range(TOP_K):
        m = jnp.max(x, axis=-1, keepdims=True)
        sel = jnp.min(jnp.where(x == m, lanes, float(LANE)), axis=-1, keepdims=True)
        vals.append(m)
        idxs.append(sel)
        x = jnp.where(lanes == sel, -jnp.inf, x)
    es = [jnp.exp(v - vals[0]) for v in vals]
    tot = es[0] + es[1] + es[2] + es[3]
    @pl.when(pl.program_id(0) == 0)
    def _():
        carry_ref[...] = jnp.zeros_like(carry_ref)

    onehot = jnp.zeros(lo_ref.shape, F32)
    for j in range(TOP_K):
        onehot = onehot + jnp.where(lanes == idxs[j], 1.0, 0.0)
    nr = lo_ref.shape[0]
    below = (lax.broadcasted_iota(jnp.int32, (nr, nr), 1) < lax.broadcasted_iota(jnp.int32, (nr, nr), 0))
    prefix = jnp.dot(below.astype(BF16), onehot.astype(BF16), preferred_element_type=F32) + carry_ref[...]
    gate = jnp.zeros(lo_ref.shape, F32)
    idx = jnp.zeros(lo_ref.shape, F32)
    rank = jnp.zeros(lo_ref.shape, F32)
    for j in range(TOP_K):
        gate = jnp.where(lanes == j, es[j] / tot, gate)
        idx = jnp.where(lanes == j, idxs[j], idx)
        rank_j = jnp.sum(jnp.where(lanes == idxs[j], prefix, 0.0), axis=-1, keepdims=True)
        rank = jnp.where(lanes == j, rank_j, rank)
    gate_ref[...] = gate
    idx_ref[...] = idx.astype(jnp.int32)
    rank_ref[...] = rank.astype(jnp.int32)
    carry_ref[...] = carry_ref[...] + jnp.sum(onehot, axis=0, keepdims=True)
    counts_ref[...] = carry_ref[...].astype(jnp.int32)


def _route(logits):
    n = logits.shape[0]
    row = pl.BlockSpec((ROW_BLK, LANE), lambda i: (i, 0))
    return pl.pallas_call(
        _route_kernel,
        grid=(n // ROW_BLK,),
        in_specs=[row],
        out_specs=[row, row, row, pl.BlockSpec((1, LANE), lambda i: (0, 0))],
        out_shape=[jax.ShapeDtypeStruct((n, LANE), F32), jax.ShapeDtypeStruct((n, LANE), jnp.int32),
                   jax.ShapeDtypeStruct((n, LANE), jnp.int32), jax.ShapeDtypeStruct((1, LANE), jnp.int32)],
        scratch_shapes=[pltpu.VMEM((1, LANE), F32)],
        compiler_params=_cparams(("arbitrary",)),
        name="moe_route",
    )(logits)


def _expert_kernel(be_ref, nu_ref, x_ref, w1_ref, b1_ref, w2_ref, b2_ref, sel_ref, o_ref, w1b_ref, w2b_ref):
    i = pl.program_id(0)
    used = i < nu_ref[0]

    @pl.when(jnp.logical_and(used, jnp.logical_or(i == 0, be_ref[i] != be_ref[jnp.maximum(i - 1, 0)])))
    def _():
        w1b_ref[...] = w1_ref[0].astype(BF16)
        w2b_ref[...] = w2_ref[0].astype(BF16)

    @pl.when(used)
    def _():
        z = jnp.dot(x_ref[...].astype(BF16), w1b_ref[...], preferred_element_type=F32) + b1_ref[0]
        z_glu = jnp.minimum(z, SWIGLU_LIMIT)
        z_lin = jnp.clip(pltpu.roll(z, z.shape[1] - 1, 1), -SWIGLU_LIMIT, SWIGLU_LIMIT)
        act = z_glu * _sigmoid(SWIGLU_ALPHA * z_glu) * (z_lin + 1.0)
        lanes = lax.broadcasted_iota(jnp.int32, act.shape, 1)
        act = jnp.where(lanes % 2 == 0, act, 0.0).astype(BF16)
        act = jnp.dot(act, sel_ref[...], preferred_element_type=F32).astype(BF16)
        o_ref[...] = jnp.dot(act, w2b_ref[...], preferred_element_type=F32) + b2_ref[0]

    @pl.when(jnp.logical_not(used))
    def _():
        o_ref[...] = jnp.zeros_like(o_ref)


def _experts(xs, block_e, n_used, lw):
    n_slots, d = xs.shape
    n_blocks = n_slots // MOE_BLOCK
    f2 = 2 * D_EXPERT
    sel = jnp.asarray(np.arange(f2)[:, None] == 2 * np.arange(D_EXPERT)[None, :], BF16)
    grid_spec = pltpu.PrefetchScalarGridSpec(
        num_scalar_prefetch=2,
        grid=(n_blocks,),
        in_specs=[pl.BlockSpec((MOE_BLOCK, d), lambda i, be, nu: (i, 0)),
                  pl.BlockSpec((1, d, f2), lambda i, be, nu: (be[i], 0, 0)),
                  pl.BlockSpec((1, 1, f2), lambda i, be, nu: (be[i], 0, 0)),
                  pl.BlockSpec((1, D_EXPERT, d), lambda i, be, nu: (be[i], 0, 0)),
                  pl.BlockSpec((1, 1, d), lambda i, be, nu: (be[i], 0, 0)),
                  pl.BlockSpec((f2, D_EXPERT), lambda i, be, nu: (0, 0))],
        out_specs=pl.BlockSpec((MOE_BLOCK, d), lambda i, be, nu: (i, 0)),
        scratch_shapes=[pltpu.VMEM((d, f2), BF16), pltpu.VMEM((D_EXPERT, d), BF16)],
    )
    return pl.pallas_call(
        _expert_kernel,
        grid_spec=grid_spec,
        out_shape=jax.ShapeDtypeStruct((n_slots, d), F32),
        compiler_params=_cparams(("arbitrary",)),
        name="moe_experts",
    )(block_e, n_used, xs, lw["moe_w1"], lw["moe_b1"], lw["moe_w2"], lw["moe_b2"], sel)


def _moe(h2, idx, rank, counts, lw):
    n, d = h2.shape
    a = n * TOP_K
    counts = counts[0, :N_EXPERTS]
    padded = (counts + MOE_BLOCK - 1) // MOE_BLOCK * MOE_BLOCK
    pad_end = jnp.cumsum(padded)
    pad_start = pad_end - padded
    dest = pad_start[idx[:, :TOP_K]] + rank[:, :TOP_K]
    n_blocks = -(-a // MOE_BLOCK) + N_EXPERTS
    n_slots = n_blocks * MOE_BLOCK
    slot_tok = jnp.zeros(n_slots, jnp.int32).at[dest.reshape(a)].set(jnp.arange(a, dtype=jnp.int32) // TOP_K)
    block_start = jnp.arange(n_blocks, dtype=jnp.int32) * MOE_BLOCK
    block_e = jnp.minimum(jnp.sum(block_start[:, None] >= pad_end[None, :], axis=1), N_EXPERTS - 1).astype(jnp.int32)
    xs = h2[slot_tok]
    n_used = (pad_end[N_EXPERTS - 1:] // MOE_BLOCK).astype(jnp.int32)
    ys = _experts(xs, block_e + lw["moe_row0"], n_used, lw)
    return [ys[dest[:, j]] for j in range(TOP_K)]


def _final_kernel(alpha, x_ref, y0_ref, y1_ref, y2_ref, y3_ref, gate_ref, g2_ref, lg_ref, lb_ref, sh_ref, sc_ref,
                  x2_ref, h_ref):
    gate = gate_ref[...]
    y = y0_ref[...] * gate[:, 0:1]
    for j, y_ref in enumerate((y1_ref, y2_ref, y3_ref), start=1):
        y = y + y_ref[...] * gate[:, j:j + 1]
    x2 = _ln(alpha * x_ref[...] + g2_ref[0] * y) * lg_ref[...] + lb_ref[...]
    x2_ref[...] = x2
    h_ref[...] = (_ln(x2) * (1.0 + sc_ref[0]) + sh_ref[0]).astype(h_ref.dtype)


def _final(x1, y_rows, gate, mods3, mods3_next, lw, nbb, alpha, latent_only=False):
    n, d = x1.shape
    assert len(y_rows) == TOP_K == 4
    full = lambda shape: pl.BlockSpec(shape, lambda i: (0,) * len(shape))
    row = pl.BlockSpec((ROW_BLK, d), lambda i: (i, 0))
    if latent_only:
        x2_spec = pl.BlockSpec((ROW_BLK, d), lambda i: ((i // nbb) * (nbb - 1) + jnp.maximum(i % nbb - 1, 0), 0))
        n_out = n // nbb * (nbb - 1)
    else:
        x2_spec, n_out = row, n
    return pl.pallas_call(
        functools.partial(_final_kernel, alpha),
        grid=(n // ROW_BLK,),
        in_specs=[row, row, row, row, row, pl.BlockSpec((ROW_BLK, LANE), lambda i: (i, 0)),
                  _mod_spec(5, d, nbb), full((1, d)), full((1, d)), _mod_spec(0, d, nbb), _mod_spec(1, d, nbb)],
        out_specs=[x2_spec, row],
        out_shape=[jax.ShapeDtypeStruct((n_out, d), F32), jax.ShapeDtypeStruct((n, d), BF16)],
        compiler_params=_cparams(("arbitrary",)),
        name="moe_combine_residual_ln",
    )(x1, *y_rows, gate, mods3, lw["ln2_g"], lw["ln2_b"], mods3_next, mods3_next)


def _rope_tables(ctx_len, seq):
    n_freq = HEAD_DIM // 4
    t = np.arange(seq)
    inv = np.power(ROPE_BASE, -np.arange(n_freq, dtype=np.float32) / n_freq).astype(np.float32)
    row = (t // GRID_W).astype(np.float32)
    col = (t % GRID_W).astype(np.float32)
    ang = jnp.asarray(np.concatenate([row[:, None] * inv, col[:, None] * inv], -1))
    cos = jnp.cos(ang)
    sin = jnp.sin(ang)
    cos_h = jnp.concatenate([cos, cos], -1)
    sin_h = jnp.concatenate([-sin, sin], -1)
    cos_t = jnp.concatenate([jnp.ones((ctx_len, HEAD_DIM), F32), cos_h], 0)
    sin_t = jnp.concatenate([jnp.zeros((ctx_len, HEAD_DIM), F32), sin_h], 0)
    return jnp.tile(cos_t, (1, LANE // HEAD_DIM)), jnp.tile(sin_t, (1, LANE // HEAD_DIM))


def _to_scan_layout(x, batch, t_len):
    return x.reshape(x.shape[:-2] + (batch, t_len, x.shape[-1]))


def _pad_dir_lora(w):
    z = jnp.zeros_like(w[0])
    return jnp.stack([_split_weight(jnp.concatenate([w[0], z], 0)), _split_weight(jnp.concatenate([z, w[1]], 0))])


def kernel(x, c, ctx, c_ctx, w_mod, b_mod, w_in, rw_mu, rw_w0, rw_w_lora, rw_a0, rw_a_lora, rw_g_lora, rw_k_k, rw_k_a, rw_r_k, rw_gn_w, rw_gn_b, w_out_rw, na_rpb, w_out_na, sc_conv, w_out_sc, b_gate, w_merge, ln1_g, ln1_b, router_w, router_b, moe_w1, moe_b1, moe_w2, moe_b2, ln2_g, ln2_b):
    batch, seq, d = x.shape
    ctx_len = ctx.shape[1]
    depth = w_in.shape[0]
    t_len = ctx_len + seq
    n = batch * t_len
    nbb = t_len // ROW_BLK
    assert batch <= 8 and ctx_len == ROW_BLK and seq % (NA_GROUP * GRID_W) == 0
    alpha = (2 * depth) ** 0.25

    cc = jnp.zeros((16, d), F32).at[:batch].set(c).at[8].set(c_ctx)
    mods = _modulation(cc, w_mod, b_mod)
    mods3 = [mods[l].reshape(16, 1, 6 * d) for l in range(depth)]

    cos_t, sin_t = _rope_tables(ctx_len, seq)
    seg = (np.arange(RW_WIDTH)[:, None] // HEAD_DIM == np.arange(LANE)[None, :]).astype(np.float32)
    consts = dict(cos=cos_t, sin=sin_t, seg=jnp.asarray(seg, BF16), seg_t=jnp.asarray(seg.T, BF16))
    tables = _na_tables(seq // GRID_W)

    o_na = RW_COLS_PAD
    o_sc = o_na + 3 * NA_WIDTH
    o_gate = o_sc + 3 * SC_WIDTH
    pad_cols = RW_COLS_PAD - RW_COLS

    xs = jnp.concatenate([ctx, x], axis=1).reshape(n, d)
    h = _lnmod(xs, mods3[0], nbb)
    for l in range(depth):
        w_in_p = jnp.concatenate([w_in[l, :, :RW_COLS], jnp.zeros((d, pad_cols), F32), w_in[l, :, RW_COLS:]],
                                 axis=1).astype(BF16)
        lw = dict(
            mu=jnp.pad(rw_mu[l], ((0, 0), (0, pad_cols))), w0=rw_w0[l], w_lora=_pad_dir_lora(rw_w_lora[l]),
            a0=rw_a0[l], a_lora=_pad_dir_lora(rw_a_lora[l]),
            g_lora=_split_weight(jnp.pad(rw_g_lora[l], ((0, GATE_LORA_PAD - GATE_LORA), (0, 0)))),
            k_k=rw_k_k[l].reshape(1, -1), k_a=rw_k_a[l].reshape(1, -1), r_k=rw_r_k[l].reshape(1, -1),
            gn_w=rw_gn_w[l].reshape(1, -1), gn_b=rw_gn_b[l].reshape(1, -1),
            w_out_rw=w_out_rw[l].astype(BF16), w_out_na=w_out_na[l].astype(BF16),
            w_out_sc=w_out_sc[l].astype(BF16), b_gate=b_gate[l].reshape(1, -1),
            w_merge=w_merge[l].astype(BF16), ln1_g=ln1_g[l].reshape(1, -1), ln1_b=ln1_b[l].reshape(1, -1),
            router_w=_split_weight(jnp.pad(router_w[l], ((0, 0), (0, LANE - N_EXPERTS)))),
            router_b=jnp.pad(router_b[l], (0, LANE - N_EXPERTS)).reshape(1, -1),
            moe_row0=l * N_EXPERTS,
            moe_w1=moe_w1.reshape((depth * N_EXPERTS,) + moe_w1.shape[2:]),
            moe_b1=moe_b1.reshape(depth * N_EXPERTS, 1, -1),
            moe_w2=moe_w2.reshape((depth * N_EXPERTS,) + moe_w2.shape[2:]),
            moe_b2=moe_b2.reshape(depth * N_EXPERTS, 1, -1),
            ln2_g=ln2_g[l].reshape(1, -1), ln2_b=ln2_b[l].reshape(1, -1),
        )
        proj = _matmul(h, w_in_p)
        r_s, kap, v, w, kd, b, bonus, g = _rwkv_prepare(proj, lw, consts, nbb)
        tl = functools.partial(_to_scan_layout, batch=batch, t_len=t_len)
        wkv_f, wkv_b = _wkv_scan(tl(r_s), tl(kap), tl(v), tl(w), tl(kd), tl(b), ctx_len)
        y_rw = _rwkv_out(wkv_f.reshape(n, RW_WIDTH), wkv_b.reshape(n, RW_WIDTH), bonus, g, lw, consts)
        y_na = _na_attention(proj, _na_bias(na_rpb[l]), tables, o_na, batch, t_len, ctx_len)
        y_sc = _short_conv(proj, jnp.pad(sc_conv[l], ((0, 5), (0, 0))), o_sc, nbb)
        m = _merge1(y_rw, y_na, y_sc, lw, proj, o_gate, d)
        x1, h2, logits = _merge2(m, xs, mods3[l], lw, nbb, alpha)
        gate, idx, rank, counts = _route(logits)
        y_rows = _moe(h2, idx, rank, counts, lw)
        xs, h = _final(x1, y_rows, gate, mods3[l], mods3[min(l + 1, depth - 1)], lw, nbb, alpha,
                       latent_only=(l == depth - 1))
    return xs.reshape(batch, seq, d)
```

```python
import functools
import math

import numpy as np
import jax
import jax.numpy as jnp
from jax import lax
from jax.experimental import pallas as pl
from jax.experimental.pallas import tpu as pltpu

GRID_W = 64
RW_HEADS = 16
HEAD_DIM = 64
RW_WIDTH = RW_HEADS * HEAD_DIM
DECAY_LORA = 64
ICLR_LORA = 64
GATE_LORA = 160
GATE_LORA_PAD = 256
GN_EPS = 64e-5
NA_HEADS = 16
NA_WIDTH = NA_HEADS * HEAD_DIM
NA_WIN_ROWS = 8
NA_WIN_COLS = 16
SC_WIDTH = 1024
RW_COLS = 3 * RW_WIDTH + 2 * DECAY_LORA + 2 * ICLR_LORA + GATE_LORA
RW_COLS_PAD = 3 * RW_WIDTH + 2 * DECAY_LORA + 2 * ICLR_LORA + GATE_LORA_PAD
N_EXPERTS = 32
TOP_K = 4
D_EXPERT = 512
SWIGLU_ALPHA = 1.702
SWIGLU_LIMIT = 7.0
MOE_BLOCK = 256
ROPE_BASE = 10000.0
LN_EPS = 1e-6
NEG_INF = -1e30

LANE = 128
ROW_BLK = 256
MM_TM = 2048
MM_TN = 512
SCAN_TT = 32
SCAN_VSPLIT = 2
SCAN_NACC = 1
SCAN_RELAYOUT_UNROLL = 4
NA_GROUP = 8
NA_KROWS = 16
VMEM_LIMIT = 56 * 1024 * 1024

F32 = jnp.float32
BF16 = jnp.bfloat16


def _cparams(sem):
    return pltpu.CompilerParams(dimension_semantics=sem, vmem_limit_bytes=VMEM_LIMIT)


def _ln(x):
    mu = jnp.mean(x, axis=-1, keepdims=True)
    xc = x - mu
    var = jnp.mean(xc * xc, axis=-1, keepdims=True)
    return xc * lax.rsqrt(var + LN_EPS)


def _sigmoid(x):
    return 1.0 / (1.0 + jnp.exp(-x))


def _mod_kernel(c_ref, w_ref, b_ref, o_ref):
    c = c_ref[...]
    s = (c * _sigmoid(c)).astype(BF16)
    o_ref[0] = jnp.dot(s, w_ref[0].astype(BF16), preferred_element_type=F32) + b_ref[0]


def _modulation(cc, w_mod, b_mod):
    depth, d, n = w_mod.shape
    tn = 1024
    return pl.pallas_call(
        _mod_kernel,
        grid=(depth, n // tn),
        in_specs=[pl.BlockSpec((16, d), lambda l, j: (0, 0)),
                  pl.BlockSpec((1, d, tn), lambda l, j: (l, 0, j)),
                  pl.BlockSpec((1, 1, tn), lambda l, j: (l, 0, j))],
        out_specs=pl.BlockSpec((1, 16, tn), lambda l, j: (l, 0, j)),
        out_shape=jax.ShapeDtypeStruct((depth, 16, n), F32),
        compiler_params=_cparams(("arbitrary", "arbitrary")),
        name="modulation",
    )(cc, w_mod, b_mod.reshape(depth, 1, n))


def _mod_spec(chunk, d, nblk_per_batch):
    def imap(i):
        row = jnp.where(i % nblk_per_batch == 0, 8, i // nblk_per_batch)
        return (row, 0, chunk)
    return pl.BlockSpec((1, 1, d), imap)


def _lnmod_kernel(x_ref, sh_ref, sc_ref, o_ref):
    o_ref[...] = (_ln(x_ref[...]) * (1.0 + sc_ref[0]) + sh_ref[0]).astype(o_ref.dtype)


def _lnmod(x, mods3, nbb):
    n, d = x.shape
    return pl.pallas_call(
        _lnmod_kernel,
        grid=(n // ROW_BLK,),
        in_specs=[pl.BlockSpec((ROW_BLK, d), lambda i: (i, 0)),
                  _mod_spec(0, d, nbb), _mod_spec(1, d, nbb)],
        out_specs=pl.BlockSpec((ROW_BLK, d), lambda i: (i, 0)),
        out_shape=jax.ShapeDtypeStruct((n, d), BF16),
        compiler_params=_cparams(("parallel",)),
        name="ln_mod",
    )(x, mods3, mods3)


def _mm_kernel(a_ref, b_ref, o_ref):
    o_ref[...] = jnp.dot(a_ref[...], b_ref[...], preferred_element_type=F32).astype(o_ref.dtype)


def _matmul(a, b, out_dtype=F32):
    m, k = a.shape
    _, n = b.shape
    tm = MM_TM if m % MM_TM == 0 else ROW_BLK
    return pl.pallas_call(
        _mm_kernel,
        grid=(m // tm, n // MM_TN),
        in_specs=[pl.BlockSpec((tm, k), lambda i, j: (i, 0)),
                  pl.BlockSpec((k, MM_TN), lambda i, j: (0, j))],
        out_specs=pl.BlockSpec((tm, MM_TN), lambda i, j: (i, j)),
        out_shape=jax.ShapeDtypeStruct((m, n), out_dtype),
        compiler_params=_cparams(("parallel", "arbitrary")),
        name="in_proj",
    )(a, b)


def _halo_specs(width, col_blk, nblk):
    per = ROW_BLK // 8

    def prev_map(i):
        return (jnp.maximum(i * per - 1, 0), col_blk)

    def next_map(i):
        return (jnp.minimum((i + 1) * per, nblk * per - 1), col_blk)

    return (pl.BlockSpec((8, width), prev_map), pl.BlockSpec((8, width), next_map))


def _shifted(p, prev8, next8, i, nbb):
    pos = i % nbb
    has_prev = pos >= 2
    has_next = jnp.logical_and(pos >= 1, pos <= nbb - 2)
    rows = lax.broadcasted_iota(jnp.int32, p.shape, 0)
    prow = jnp.where(has_prev, prev8[7:8, :], 0.0)
    nrow = jnp.where(has_next, next8[0:1, :], 0.0)
    p_prev = jnp.where(rows == 0, prow, pltpu.roll(p, 1, 0))
    p_next = jnp.where(rows == p.shape[0] - 1, nrow, pltpu.roll(p, p.shape[0] - 1, 0))
    return p_prev, p_next


def _split3(x):
    hi = x.astype(BF16)
    r = x - hi.astype(F32)
    mid = r.astype(BF16)
    lo = (r - mid.astype(F32)).astype(BF16)
    return hi, mid, lo


def _dot_indicator(x, w):
    hi, mid, lo = _split3(x)
    return (jnp.dot(hi, w, preferred_element_type=F32) + jnp.dot(mid, w, preferred_element_type=F32)
            + jnp.dot(lo, w, preferred_element_type=F32))


def _dot_split(x, w_ref):
    hi, mid, _ = _split3(x)
    w_hi = w_ref[0]
    return (jnp.dot(hi, w_hi, preferred_element_type=F32) + jnp.dot(mid, w_hi, preferred_element_type=F32)
            + jnp.dot(hi, w_ref[1], preferred_element_type=F32))


def _split_weight(w):
    hi = w.astype(BF16)
    return jnp.stack([hi, (w - hi.astype(F32)).astype(BF16)])


def _seg_sum(x, e_ref, et_ref):
    return _dot_indicator(_dot_indicator(x, e_ref[...]), et_ref[...])


def _swap_halves(x):
    lanes = lax.broadcasted_iota(jnp.int32, x.shape, 1)
    first = (lanes % HEAD_DIM) < (HEAD_DIM // 2)
    n = x.shape[1]
    return jnp.where(first, pltpu.roll(x, n - HEAD_DIM // 2, 1), pltpu.roll(x, HEAD_DIM // 2, 1))


def _prep_kernel(nbb, p_ref, pp_ref, pn_ref, mu_ref, w0_ref, wl_ref, a0_ref, al_ref, gl_ref,
                 kk_ref, ka_ref, rk_ref, cos_ref, sin_ref, e_ref, et_ref,
                 r_o, kap_o, v_o, w_o, kd_o, b_o, bonus_o, g_o):
    i = pl.program_id(0)
    c = RW_WIDTH
    p = p_ref[...]
    p_prev, p_next = _shifted(p, pp_ref[...], pn_ref[...], i, nbb)
    pm = p + mu_ref[0:1, :] * (p_prev - p) + mu_ref[1:2, :] * (p_next - p)
    r, k, v = pm[:, :c], pm[:, c:2 * c], pm[:, 2 * c:3 * c]
    o = 3 * c
    wl = jnp.tanh(pm[:, o:o + 2 * DECAY_LORA])
    o += 2 * DECAY_LORA
    al = pm[:, o:o + 2 * ICLR_LORA]
    o += 2 * ICLR_LORA
    gl = _sigmoid(pm[:, o:o + GATE_LORA_PAD])
    g_o[...] = _dot_split(gl, gl_ref)
    kk = k * kk_ref[...]
    ss = _seg_sum(kk * kk, e_ref, et_ref)
    kk = kk / jnp.maximum(jnp.sqrt(ss), 1e-12)
    cos = jnp.concatenate([cos_ref[...]] * (c // LANE), axis=1)
    sin = jnp.concatenate([sin_ref[...]] * (c // LANE), axis=1)

    def rope(x):
        return x * cos + _swap_halves(x) * sin

    kap = rope(kk)
    r_o[...] = rope(r)
    kap_o[...] = kap
    v_o[...] = v
    k_sum = jnp.zeros_like(k)
    for d in range(2):
        w_raw = w0_ref[d:d + 1, :] + _dot_split(wl, wl_ref.at[d])
        w_o[d] = jnp.exp(-math.exp(-0.5) * _sigmoid(w_raw))
        a = _sigmoid(a0_ref[d:d + 1, :] + _dot_split(al, al_ref.at[d]))
        kd = k * (1.0 + (a - 1.0) * ka_ref[...])
        k_sum = k_sum + kd
        kd_o[d] = rope(kd)
        b_o[d] = kap * a
    bonus_o[...] = _seg_sum(r * k_sum * rk_ref[...], e_ref, et_ref) * v


def _rwkv_prepare(proj, lw, consts, nbb):
    n = proj.shape[0]
    c = RW_WIDTH
    nblk = n // ROW_BLK
    prev_spec, next_spec = _halo_specs(RW_COLS_PAD, 0, nblk)
    full = lambda shape: pl.BlockSpec(shape, lambda i: (0,) * len(shape))
    row_c = pl.BlockSpec((ROW_BLK, c), lambda i: (i, 0))
    row_2c = pl.BlockSpec((2, ROW_BLK, c), lambda i: (0, i, 0))
    tab = pl.BlockSpec((ROW_BLK, LANE), lambda i: (i % nbb, 0))
    outs = pl.pallas_call(
        functools.partial(_prep_kernel, nbb),
        grid=(nblk,),
        in_specs=[pl.BlockSpec((ROW_BLK, RW_COLS_PAD), lambda i: (i, 0)), prev_spec, next_spec,
                  full((2, RW_COLS_PAD)), full((2, c)), full((2, 2, 2 * DECAY_LORA, c)), full((2, c)),
                  full((2, 2, 2 * ICLR_LORA, c)), full((2, GATE_LORA_PAD, c)), full((1, c)), full((1, c)),
                  full((1, c)), tab, tab, full((c, LANE)), full((LANE, c))],
        out_specs=[row_c, row_c, row_c, row_2c, row_2c, row_2c, row_c, row_c],
        out_shape=[jax.ShapeDtypeStruct((n, c), F32)] * 3 + [jax.ShapeDtypeStruct((2, n, c), F32)] * 3
        + [jax.ShapeDtypeStruct((n, c), F32)] * 2,
        compiler_params=_cparams(("parallel",)),
        name="rwkv_prepare",
    )(proj, proj, proj, lw["mu"], lw["w0"], lw["w_lora"], lw["a0"], lw["a_lora"], lw["g_lora"],
      lw["k_k"], lw["k_a"], lw["r_k"], consts["cos"], consts["sin"], consts["seg"], consts["seg_t"])
    return outs


def _scan_kernel(rf_ref, kapf_ref, vf_ref, rb_ref, kapb_ref, vb_ref, wf_ref, kdf_ref, bf_ref, wb_ref, kdb_ref, bb_ref,
                 of_ref, ob_ref, s_ref, vec_ref, out_ref):
    nb, tt = rf_ref.shape[0], rf_ref.shape[1]
    n_pair = rf_ref.shape[2] // (2 * HEAD_DIM)
    nl = nb * n_pair
    vh = HEAD_DIM // SCAN_VSPLIT

    @pl.when(pl.program_id(0) == 0)
    def _():
        s_ref[...] = jnp.zeros_like(s_ref)

    def to_lanes(i, carry):
        slab = lambda x: jnp.concatenate([x[:, p * 2 * HEAD_DIM:(p + 1) * 2 * HEAD_DIM] for p in range(n_pair)], axis=0)
        tb = tt - 1 - i
        pairs = ((kapf_ref[:, i, :], kapb_ref[:, tb, :]), (rf_ref[:, i, :], rb_ref[:, tb, :]),
                 (wf_ref[0, :, i, :], wb_ref[0, :, tb, :]), (kdf_ref[0, :, i, :], kdb_ref[0, :, tb, :]),
                 (bf_ref[0, :, i, :], bb_ref[0, :, tb, :]), (vf_ref[:, i, :], vb_ref[:, tb, :]))
        for j, (xf, xb) in enumerate(pairs):
            res = jnp.concatenate([slab(xf), slab(xb)], axis=0).T
            vec_ref[0, i, j] = res[:HEAD_DIM]
            vec_ref[1, i, j] = res[HEAD_DIM:]
        return carry

    lax.fori_loop(0, tt, to_lanes, 0, unroll=SCAN_RELAYOUT_UNROLL)

    def add_to(acc, j, term):
        acc[j] = term if acc[j] is None else acc[j] + term

    def total(acc):
        return functools.reduce(lambda a, b: a + b, acc)

    skk0 = []
    for g in range(2):
        acc = [None] * SCAN_NACC
        for k in range(HEAD_DIM):
            add_to(acc, k % SCAN_NACC, s_ref[g, k] * vec_ref[g, 0, 0, k:k + 1, :])
        skk0.append(total(acc))

    def step(i, skks):
        i_next = jnp.minimum(i + 1, tt - 1)
        skks_next = []
        for g in range(2):
            parts = []
            for h in range(SCAN_VSPLIT):
                rows = slice(h * vh, (h + 1) * vh)
                skk_h = skks[g][rows]
                v = vec_ref[g, i, 5, rows]
                acc_out = [None] * SCAN_NACC
                acc_next = [None] * SCAN_NACC
                for k in range(HEAD_DIM):
                    s_new = (s_ref[g, k, rows] * vec_ref[g, i, 2, k:k + 1, :] - skk_h * vec_ref[g, i, 4, k:k + 1, :]
                             + v * vec_ref[g, i, 3, k:k + 1, :])
                    s_ref[g, k, rows] = s_new
                    add_to(acc_out, k % SCAN_NACC, s_new * vec_ref[g, i, 1, k:k + 1, :])
                    add_to(acc_next, k % SCAN_NACC, s_new * vec_ref[g, i_next, 0, k:k + 1, :])
                out_ref[g, i, rows] = total(acc_out)
                parts.append(total(acc_next))
            skks_next.append(jnp.concatenate(parts, axis=0))
        return tuple(skks_next)

    lax.fori_loop(0, tt, step, tuple(skk0))

    def to_tokens(i, carry):
        back = jnp.concatenate([out_ref[0, i], out_ref[1, i]], axis=0).T
        unslab = lambda x: jnp.concatenate([x[p * nb:(p + 1) * nb] for p in range(n_pair)], axis=1)
        of_ref[:, i, :] = unslab(back[:nl])
        ob_ref[:, tt - 1 - i, :] = unslab(back[nl:])
        return carry

    lax.fori_loop(0, tt, to_tokens, 0, unroll=SCAN_RELAYOUT_UNROLL)


def _wkv_scan(r, kap, v, w, kd, b, ctx_len):
    batch, t, width = r.shape
    lanes = batch * width // HEAD_DIM
    nb = t // SCAN_TT
    n_ctx_blk = ctx_len // SCAN_TT

    def mirrored(j):
        return jnp.where(j < n_ctx_blk, n_ctx_blk - 1 - j, nb - 1 - (j - n_ctx_blk))

    fwd = pl.BlockSpec((batch, SCAN_TT, width), lambda j: (0, j, 0))
    bwd = pl.BlockSpec((batch, SCAN_TT, width), lambda j: (0, mirrored(j), 0))
    fwd_dir = pl.BlockSpec((1, batch, SCAN_TT, width), lambda j: (0, 0, j, 0))
    bwd_dir = pl.BlockSpec((1, batch, SCAN_TT, width), lambda j: (1, 0, mirrored(j), 0))
    out = jax.ShapeDtypeStruct((batch, t, width), F32)
    return pl.pallas_call(
        _scan_kernel,
        grid=(nb,),
        in_specs=[fwd, fwd, fwd, bwd, bwd, bwd, fwd_dir, fwd_dir, fwd_dir, bwd_dir, bwd_dir, bwd_dir],
        out_specs=[fwd, bwd],
        out_shape=[out, out],
        scratch_shapes=[pltpu.VMEM((2, HEAD_DIM, HEAD_DIM, lanes), F32),
                        pltpu.VMEM((2, SCAN_TT, 6, HEAD_DIM, lanes), F32),
                        pltpu.VMEM((2, SCAN_TT, HEAD_DIM, lanes), F32)],
        compiler_params=_cparams(("arbitrary",)),
        name="wkv_scan",
    )(r, kap, v, r, kap, v, w, kd, b, w, kd, b)


def _rwkv_out_kernel(wkv_f_ref, wkv_b_ref, bonus_ref, g_ref, gw_ref, gb_ref, e_ref, et_ref, o_ref):
    x = wkv_f_ref[...] + wkv_b_ref[...]
    inv_n = 1.0 / HEAD_DIM
    mu = _seg_sum(x, e_ref, et_ref) * inv_n
    xc = x - mu
    var = _seg_sum(xc * xc, e_ref, et_ref) * inv_n
    y = xc * lax.rsqrt(var + GN_EPS) * gw_ref[...] + gb_ref[...]
    o_ref[...] = ((y + bonus_ref[...]) * g_ref[...]).astype(o_ref.dtype)


def _rwkv_out(wkv_f, wkv_b, bonus, g, lw, consts):
    n, c = bonus.shape
    full = lambda shape: pl.BlockSpec(shape, lambda i: (0,) * len(shape))
    row_c = pl.BlockSpec((ROW_BLK, c), lambda i: (i, 0))
    return pl.pallas_call(
        _rwkv_out_kernel,
        grid=(n // ROW_BLK,),
        in_specs=[row_c, row_c, row_c, row_c,
                  full((1, c)), full((1, c)), full((c, LANE)), full((LANE, c))],
        out_specs=row_c,
        out_shape=jax.ShapeDtypeStruct((n, c), BF16),
        compiler_params=_cparams(("parallel",)),
        name="rwkv_out",
    )(wkv_f, wkv_b, bonus, g, lw["gn_w"], lw["gn_b"], consts["seg"], consts["seg_t"])


def _conv_kernel(nbb, bg_ref, cg_ref, x_ref, cgp_ref, xp_ref, cgn_ref, xn_ref, w_ref, o_ref):
    i = pl.program_id(0)
    u = cg_ref[...] * x_ref[...]
    u_prev, u_next = _shifted(u, cgp_ref[...] * xp_ref[...], cgn_ref[...] * xn_ref[...], i, nbb)
    y = w_ref[0:1, :] * u_prev + w_ref[1:2, :] * u + w_ref[2:3, :] * u_next
    o_ref[...] = (bg_ref[...] * y).astype(o_ref.dtype)


def _short_conv(proj, conv_w, col0, nbb):
    n = proj.shape[0]
    cw = MM_TN
    nc = SC_WIDTH // cw
    assert col0 % cw == 0
    cb = col0 // cw
    nblk = n // ROW_BLK
    blk = lambda s: pl.BlockSpec((ROW_BLK, cw), lambda i, j: (i, cb + s * nc + j))

    def halos(s):
        prev_spec, next_spec = _halo_specs(cw, 0, nblk)
        pm, nm = prev_spec.index_map, next_spec.index_map
        return (pl.BlockSpec((8, cw), lambda i, j: (pm(i)[0], cb + s * nc + j)),
                pl.BlockSpec((8, cw), lambda i, j: (nm(i)[0], cb + s * nc + j)))

    cg_prev, cg_next = halos(1)
    x_prev, x_next = halos(2)
    return pl.pallas_call(
        functools.partial(_conv_kernel, nbb),
        grid=(nblk, nc),
        in_specs=[blk(0), blk(1), blk(2), cg_prev, x_prev, cg_next, x_next,
                  pl.BlockSpec((8, cw), lambda i, j: (0, j))],
        out_specs=pl.BlockSpec((ROW_BLK, cw), lambda i, j: (i, j)),
        out_shape=jax.ShapeDtypeStruct((n, SC_WIDTH), BF16),
        compiler_params=_cparams(("parallel", "arbitrary")),
        name="short_conv",
    )(proj, proj, proj, proj, proj, proj, proj, conv_w)


def _na_tables(rows):
    kr = min(NA_WIN_ROWS, rows)
    n_groups = rows // NA_GROUP
    krows = min(NA_KROWS, rows)
    n_dr = 2 * NA_WIN_ROWS - 1
    dr = np.full((n_groups, NA_GROUP, krows), n_dr, np.int32)
    bases = []
    for g in range(n_groups):
        r0 = g * NA_GROUP
        base = int(np.clip(np.clip(r0 - kr // 2, 0, rows - kr), 0, rows - krows))
        bases.append(base)
        for rl in range(NA_GROUP):
            r = r0 + rl
            rs = int(np.clip(r - kr // 2, 0, rows - kr))
            for j in range(krows):
                if rs <= base + j < rs + kr:
                    dr[g, rl, j] = base + j - r + NA_WIN_ROWS - 1
            assert (dr[g, rl] < n_dr).sum() == kr
    uniq, table_of = [], []
    for g in range(n_groups):
        for u, gu in enumerate(uniq):
            if np.array_equal(dr[g], dr[gu]):
                table_of.append(u)
                break
        else:
            table_of.append(len(uniq))
            uniq.append(g)
    return dr[uniq], tuple(bases), tuple(table_of)


def _na_kernel(ctx_len, dr, bases, table_of, q_ref, k_ref, v_ref, bias_ref, o_ref):
    scale = HEAD_DIM ** -0.5
    nq = NA_GROUP * GRID_W
    krows = dr.shape[2]
    nk = krows * GRID_W
    first = lax.broadcasted_iota(jnp.int32, (GRID_W, 2 * GRID_W), 1) < GRID_W

    def bias_of(table, h):
        rows = []
        for rl in range(NA_GROUP):
            tiles = [jnp.where(first, bias_ref[h, int(dr[table, rl, j])], bias_ref[h, int(dr[table, rl, j + 1])])
                     for j in range(0, krows, 2)]
            rows.append(jnp.concatenate(tiles, axis=1))
        return jnp.concatenate(rows, axis=0)

    outs_heads = []
    for h in range(2):
        sl = slice(h * HEAD_DIM, (h + 1) * HEAD_DIM)
        q = (q_ref[:, sl] * scale).astype(BF16)
        k = k_ref[:, sl].astype(BF16)
        v = v_ref[:, sl].astype(BF16)
        kc, vc = k[:ctx_len], v[:ctx_len]
        dn = (((1,), (1,)), ((), ()))
        s = lax.dot_general(q[:ctx_len], kc, dn, preferred_element_type=F32)
        s = s - jnp.max(s, axis=-1, keepdims=True)
        e = jnp.exp(s)
        p = (e / jnp.sum(e, axis=-1, keepdims=True)).astype(BF16)
        pieces = [jnp.dot(p, vc, preferred_element_type=F32)]
        biases = [bias_of(t, h) for t in range(dr.shape[0])]
        for g, base in enumerate(bases):
            q0 = ctx_len + g * nq
            k0 = ctx_len + base * GRID_W
            qg = q[q0:q0 + nq]
            s_win = (lax.dot_general(qg, k[k0:k0 + nk], dn, preferred_element_type=F32)
                     + biases[table_of[g]])
            s_ctx = lax.dot_general(qg, kc, dn, preferred_element_type=F32)
            m = jnp.maximum(jnp.max(s_win, axis=-1, keepdims=True), jnp.max(s_ctx, axis=-1, keepdims=True))
            e_win = jnp.exp(s_win - m)
            e_ctx = jnp.exp(s_ctx - m)
            inv = 1.0 / (jnp.sum(e_win, axis=-1, keepdims=True) + jnp.sum(e_ctx, axis=-1, keepdims=True))
            acc = jnp.dot((e_win * inv).astype(BF16), v[k0:k0 + nk], preferred_element_type=F32)
            acc = acc + jnp.dot((e_ctx * inv).astype(BF16), vc, preferred_element_type=F32)
            pieces.append(acc)
        outs_heads.append(jnp.concatenate(pieces, axis=0))
    o_ref[...] = jnp.concatenate(outs_heads, axis=1).astype(o_ref.dtype)


def _na_bias(rpb):
    qc = np.arange(GRID_W)[:, None]
    kc = np.arange(GRID_W)[None, :]
    wstart = np.clip(qc - NA_WIN_COLS // 2, 0, GRID_W - NA_WIN_COLS)
    col_ok = (kc >= wstart) & (kc < wstart + NA_WIN_COLS)
    dc = np.clip(kc - qc + NA_WIN_COLS - 1, 0, 2 * NA_WIN_COLS - 2)
    blocks = jnp.where(col_ok, rpb[:, :, dc], NEG_INF)
    blocks = jnp.concatenate([blocks, jnp.full_like(blocks[:, :1], NEG_INF)], axis=1)
    return jnp.concatenate([blocks, blocks], axis=-1)


def _na_attention(proj, bias, tables, col0, batch, t_len, ctx_len):
    n = proj.shape[0]
    cb = col0 // LANE
    hp = NA_HEADS // 2
    seq = lambda j: pl.BlockSpec((t_len, LANE), lambda h, b: (b, cb + j * hp + h))
    return pl.pallas_call(
        functools.partial(_na_kernel, ctx_len, *tables),
        grid=(hp, batch),
        in_specs=[seq(0), seq(1), seq(2),
                  pl.BlockSpec((2,) + bias.shape[1:], lambda h, b: (h, 0, 0, 0))],
        out_specs=pl.BlockSpec((t_len, LANE), lambda h, b: (b, h)),
        out_shape=jax.ShapeDtypeStruct((n, NA_WIDTH), BF16),
        compiler_params=_cparams(("arbitrary", "arbitrary")),
        name="na_attention",
    )(proj, proj, proj, bias)


def _merge1_kernel(yr_ref, yn_ref, ys_ref, wr_ref, wn_ref, ws_ref, g0_ref, g1_ref, g2_ref,
                   b0_ref, b1_ref, b2_ref, o_ref):
    m = _sigmoid(g0_ref[...] + b0_ref[...]) * jnp.dot(yr_ref[...], wr_ref[...], preferred_element_type=F32)
    m = m + _sigmoid(g1_ref[...] + b1_ref[...]) * jnp.dot(yn_ref[...], wn_ref[...], preferred_element_type=F32)
    m = m + _sigmoid(g2_ref[...] + b2_ref[...]) * jnp.dot(ys_ref[...], ws_ref[...], preferred_element_type=F32)
    o_ref[...] = m.astype(o_ref.dtype)


def _merge1(y_rw, y_na, y_sc, lw, proj, gate_col0, d):
    n, c = y_rw.shape
    tm, tn = 512, MM_TN
    gb = gate_col0 // tn
    nd = d // tn
    ysp = pl.BlockSpec((tm, c), lambda i, j: (i, 0))
    wsp = pl.BlockSpec((c, tn), lambda i, j: (0, j))
    gsp = lambda br: pl.BlockSpec((tm, tn), lambda i, j: (i, gb + br * nd + j))
    bsp = lambda br: pl.BlockSpec((1, tn), lambda i, j: (0, br * nd + j))
    return pl.pallas_call(
        _merge1_kernel,
        grid=(n // tm, nd),
        in_specs=[ysp, ysp, ysp, wsp, wsp, wsp, gsp(0), gsp(1), gsp(2), bsp(0), bsp(1), bsp(2)],
        out_specs=pl.BlockSpec((tm, tn), lambda i, j: (i, j)),
        out_shape=jax.ShapeDtypeStruct((n, d), BF16),
        compiler_params=_cparams(("parallel", "arbitrary")),
        name="merge_branches",
    )(y_rw, y_na, y_sc, lw["w_out_rw"], lw["w_out_na"], lw["w_out_sc"], proj, proj, proj,
      lw["b_gate"], lw["b_gate"], lw["b_gate"])


def _merge2_kernel(alpha, m_ref, w_ref, x_ref, g1_ref, lg_ref, lb_ref, sh_ref, sc_ref, rw_ref, rb_ref,
                   x1_ref, h2_ref, lo_ref):
    mix = jnp.dot(m_ref[...], w_ref[...], preferred_element_type=F32)
    x1 = _ln(alpha * x_ref[...] + g1_ref[0] * mix) * lg_ref[...] + lb_ref[...]
    x1_ref[...] = x1
    h2 = _ln(x1) * (1.0 + sc_ref[0]) + sh_ref[0]
    h2_ref[...] = h2.astype(h2_ref.dtype)
    lo_ref[...] = _dot_split(h2, rw_ref) + rb_ref[...]


def _merge2(m, x, mods3, lw, nbb, alpha):
    n, d = x.shape
    full = lambda shape: pl.BlockSpec(shape, lambda i: (0,) * len(shape))
    row = pl.BlockSpec((ROW_BLK, d), lambda i: (i, 0))
    return pl.pallas_call(
        functools.partial(_merge2_kernel, alpha),
        grid=(n // ROW_BLK,),
        in_specs=[row, full((d, d)), row, _mod_spec(2, d, nbb), full((1, d)), full((1, d)),
                  _mod_spec(3, d, nbb), _mod_spec(4, d, nbb), full((2, d, LANE)), full((1, LANE))],
        out_specs=[row, row, pl.BlockSpec((ROW_BLK, LANE), lambda i: (i, 0))],
        out_shape=[jax.ShapeDtypeStruct((n, d), F32), jax.ShapeDtypeStruct((n, d), F32),
                   jax.ShapeDtypeStruct((n, LANE), F32)],
        compiler_params=_cparams(("parallel",)),
        name="merge_out_ln",
    )(m, lw["w_merge"], x, mods3, lw["ln1_g"], lw["ln1_b"], mods3, mods3, lw["router_w"], lw["router_b"])


def _route_kernel(lo_ref, gate_ref, idx_ref, rank_ref, counts_ref, carry_ref):
    x = lo_ref[...]
    lanes = lax.broadcasted_iota(jnp.int32, x.shape, 1).astype(F32)
    x = jnp.where(lanes < N_EXPERTS, x, -jnp.inf)
    vals, idxs = [], []
    for _ in range(TOP_K):
        m = jnp.max(x, axis=-1, keepdims=True)
        sel = jnp.min(jnp.where(x == m, lanes, float(LANE)), axis=-1, keepdims=True)
        vals.append(m)
        idxs.append(sel)
        x = jnp.where(lanes == sel, -jnp.inf, x)
    es = [jnp.exp(v - vals[0]) for v in vals]
    tot = es[0] + es[1] + es[2] + es[3]
    @pl.when(pl.program_id(0) == 0)
    def _():
        carry_ref[...] = jnp.zeros_like(carry_ref)

    onehot = jnp.zeros(lo_ref.shape, F32)
    for j in range(TOP_K):
        onehot = onehot + jnp.where(lanes == idxs[j], 1.0, 0.0)
    nr = lo_ref.shape[0]
    below = (lax.broadcasted_iota(jnp.int32, (nr, nr), 1) < lax.broadcasted_iota(jnp.int32, (nr, nr), 0))
    prefix = jnp.dot(below.astype(BF16), onehot.astype(BF16), preferred_element_type=F32) + carry_ref[...]
    gate = jnp.zeros(lo_ref.shape, F32)
    idx = jnp.zeros(lo_ref.shape, F32)
    rank = jnp.zeros(lo_ref.shape, F32)
    for j in range(TOP_K):
        gate = jnp.where(lanes == j, es[j] / tot, gate)
        idx = jnp.where(lanes == j, idxs[j], idx)
        rank_j = jnp.sum(jnp.where(lanes == idxs[j], prefix, 0.0), axis=-1, keepdims=True)
        rank = jnp.where(lanes == j, rank_j, rank)
    gate_ref[...] = gate
    idx_ref[...] = idx.astype(jnp.int32)
    rank_ref[...] = rank.astype(jnp.int32)
    carry_ref[...] = carry_ref[...] + jnp.sum(onehot, axis=0, keepdims=True)
    counts_ref[...] = carry_ref[...].astype(jnp.int32)


def _route(logits):
    n = logits.shape[0]
    row = pl.BlockSpec((ROW_BLK, LANE), lambda i: (i, 0))
    return pl.pallas_call(
        _route_kernel,
        grid=(n // ROW_BLK,),
        in_specs=[row],
        out_specs=[row, row, row, pl.BlockSpec((1, LANE), lambda i: (0, 0))],
        out_shape=[jax.ShapeDtypeStruct((n, LANE), F32), jax.ShapeDtypeStruct((n, LANE), jnp.int32),
                   jax.ShapeDtypeStruct((n, LANE), jnp.int32), jax.ShapeDtypeStruct((1, LANE), jnp.int32)],
        scratch_shapes=[pltpu.VMEM((1, LANE), F32)],
        compiler_params=_cparams(("arbitrary",)),
        name="moe_route",
    )(logits)


def _expert_kernel(be_ref, nu_ref, x_ref, w1_ref, b1_ref, w2_ref, b2_ref, sel_ref, o_ref, w1b_ref, w2b_ref):
    i = pl.program_id(0)
    used = i < nu_ref[0]

    @pl.when(jnp.logical_and(used, jnp.logical_or(i == 0, be_ref[i] != be_ref[jnp.maximum(i - 1, 0)])))
    def _():
        w1b_ref[...] = w1_ref[0].astype(BF16)
        w2b_ref[...] = w2_ref[0].astype(BF16)

    @pl.when(used)
    def _():
        z = jnp.dot(x_ref[...].astype(BF16), w1b_ref[...], preferred_element_type=F32) + b1_ref[0]
        z_glu = jnp.minimum(z, SWIGLU_LIMIT)
        z_lin = jnp.clip(pltpu.roll(z, z.shape[1] - 1, 1), -SWIGLU_LIMIT, SWIGLU_LIMIT)
        act = z_glu * _sigmoid(SWIGLU_ALPHA * z_glu) * (z_lin + 1.0)
        lanes = lax.broadcasted_iota(jnp.int32, act.shape, 1)
        act = jnp.where(lanes % 2 == 0, act, 0.0).astype(BF16)
        act = jnp.dot(act, sel_ref[...], preferred_element_type=F32).astype(BF16)
        o_ref[...] = jnp.dot(act, w2b_ref[...], preferred_element_type=F32) + b2_ref[0]

    @pl.when(jnp.logical_not(used))
    def _():
        o_ref[...] = jnp.zeros_like(o_ref)


def _experts(xs, block_e, n_used, lw):
    n_slots, d = xs.shape
    n_blocks = n_slots // MOE_BLOCK
    f2 = 2 * D_EXPERT
    sel = jnp.asarray(np.arange(f2)[:, None] == 2 * np.arange(D_EXPERT)[None, :], BF16)
    grid_spec = pltpu.PrefetchScalarGridSpec(
        num_scalar_prefetch=2,
        grid=(n_blocks,),
        in_specs=[pl.BlockSpec((MOE_BLOCK, d), lambda i, be, nu: (i, 0)),
                  pl.BlockSpec((1, d, f2), lambda i, be, nu: (be[i], 0, 0)),
                  pl.BlockSpec((1, 1, f2), lambda i, be, nu: (be[i], 0, 0)),
                  pl.BlockSpec((1, D_EXPERT, d), lambda i, be, nu: (be[i], 0, 0)),
                  pl.BlockSpec((1, 1, d), lambda i, be, nu: (be[i], 0, 0)),
                  pl.BlockSpec((f2, D_EXPERT), lambda i, be, nu: (0, 0))],
        out_specs=pl.BlockSpec((MOE_BLOCK, d), lambda i, be, nu: (i, 0)),
        scratch_shapes=[pltpu.VMEM((d, f2), BF16), pltpu.VMEM((D_EXPERT, d), BF16)],
    )
    return pl.pallas_call(
        _expert_kernel,
        grid_spec=grid_spec,
        out_shape=jax.ShapeDtypeStruct((n_slots, d), F32),
        compiler_params=_cparams(("arbitrary",)),
        name="moe_experts",
    )(block_e, n_used, xs, lw["moe_w1"], lw["moe_b1"], lw["moe_w2"], lw["moe_b2"], sel)


def _moe(h2, idx, rank, counts, lw):
    n, d = h2.shape
    a = n * TOP_K
    counts = counts[0, :N_EXPERTS]
    padded = (counts + MOE_BLOCK - 1) // MOE_BLOCK * MOE_BLOCK
    pad_end = jnp.cumsum(padded)
    pad_start = pad_end - padded
    dest = pad_start[idx[:, :TOP_K]] + rank[:, :TOP_K]
    n_blocks = -(-a // MOE_BLOCK) + N_EXPERTS
    n_slots = n_blocks * MOE_BLOCK
    slot_tok = jnp.zeros(n_slots, jnp.int32).at[dest.reshape(a)].set(jnp.arange(a, dtype=jnp.int32) // TOP_K)
    block_start = jnp.arange(n_blocks, dtype=jnp.int32) * MOE_BLOCK
    block_e = jnp.minimum(jnp.sum(block_start[:, None] >= pad_end[None, :], axis=1), N_EXPERTS - 1).astype(jnp.int32)
    xs = h2[slot_tok]
    n_used = (pad_end[N_EXPERTS - 1:] // MOE_BLOCK).astype(jnp.int32)
    ys = _experts(xs, block_e + lw["moe_row0"], n_used, lw)
    return [ys[dest[:, j]] for j in range(TOP_K)]


def _final_kernel(alpha, x_ref, y0_ref, y1_ref, y2_ref, y3_ref, gate_ref, g2_ref, lg_ref, lb_ref, sh_ref, sc_ref,
                  x2_ref, h_ref):
    gate = gate_ref[...]
    y = y0_ref[...] * gate[:, 0:1]
    for j, y_ref in enumerate((y1_ref, y2_ref, y3_ref), start=1):
        y = y + y_ref[...] * gate[:, j:j + 1]
    x2 = _ln(alpha * x_ref[...] + g2_ref[0] * y) * lg_ref[...] + lb_ref[...]
    x2_ref[...] = x2
    h_ref[...] = (_ln(x2) * (1.0 + sc_ref[0]) + sh_ref[0]).astype(h_ref.dtype)


def _final(x1, y_rows, gate, mods3, mods3_next, lw, nbb, alpha):
    n, d = x1.shape
    assert len(y_rows) == TOP_K == 4
    full = lambda shape: pl.BlockSpec(shape, lambda i: (0,) * len(shape))
    row = pl.BlockSpec((ROW_BLK, d), lambda i: (i, 0))
    return pl.pallas_call(
        functools.partial(_final_kernel, alpha),
        grid=(n // ROW_BLK,),
        in_specs=[row, row, row, row, row, pl.BlockSpec((ROW_BLK, LANE), lambda i: (i, 0)),
                  _mod_spec(5, d, nbb), full((1, d)), full((1, d)), _mod_spec(0, d, nbb), _mod_spec(1, d, nbb)],
        out_specs=[row, row],
        out_shape=[jax.ShapeDtypeStruct((n, d), F32), jax.ShapeDtypeStruct((n, d), BF16)],
        compiler_params=_cparams(("parallel",)),
        name="moe_combine_residual_ln",
    )(x1, *y_rows, gate, mods3, lw["ln2_g"], lw["ln2_b"], mods3_next, mods3_next)


def _rope_tables(ctx_len, seq):
    n_freq = HEAD_DIM // 4
    t = np.arange(seq)
    inv = np.power(ROPE_BASE, -np.arange(n_freq, dtype=np.float32) / n_freq).astype(np.float32)
    row = (t // GRID_W).astype(np.float32)
    col = (t % GRID_W).astype(np.float32)
    ang = jnp.asarray(np.concatenate([row[:, None] * inv, col[:, None] * inv], -1))
    cos = jnp.cos(ang)
    sin = jnp.sin(ang)
    cos_h = jnp.concatenate([cos, cos], -1)
    sin_h = jnp.concatenate([-sin, sin], -1)
    cos_t = jnp.concatenate([jnp.ones((ctx_len, HEAD_DIM), F32), cos_h], 0)
    sin_t = jnp.concatenate([jnp.zeros((ctx_len, HEAD_DIM), F32), sin_h], 0)
    return jnp.tile(cos_t, (1, LANE // HEAD_DIM)), jnp.tile(sin_t, (1, LANE // HEAD_DIM))


def _to_scan_layout(x, batch, t_len):
    return x.reshape(x.shape[:-2] + (batch, t_len, x.shape[-1]))


def _pad_dir_lora(w):
    z = jnp.zeros_like(w[0])
    return jnp.stack([_split_weight(jnp.concatenate([w[0], z], 0)), _split_weight(jnp.concatenate([z, w[1]], 0))])


def kernel(x, c, ctx, c_ctx, w_mod, b_mod, w_in, rw_mu, rw_w0, rw_w_lora, rw_a0, rw_a_lora, rw_g_lora, rw_k_k, rw_k_a, rw_r_k, rw_gn_w, rw_gn_b, w_out_rw, na_rpb, w_out_na, sc_conv, w_out_sc, b_gate, w_merge, ln1_g, ln1_b, router_w, router_b, moe_w1, moe_b1, moe_w2, moe_b2, ln2_g, ln2_b):
    batch, seq, d = x.shape
    ctx_len = ctx.shape[1]
    depth = w_in.shape[0]
    t_len = ctx_len + seq
    n = batch * t_len
    nbb = t_len // ROW_BLK
    assert batch <= 8 and ctx_len == ROW_BLK and seq % (NA_GROUP * GRID_W) == 0
    alpha = (2 * depth) ** 0.25

    cc = jnp.zeros((16, d), F32).at[:batch].set(c).at[8].set(c_ctx)
    mods = _modulation(cc, w_mod, b_mod)
    mods3 = [mods[l].reshape(16, 1, 6 * d) for l in range(depth)]

    cos_t, sin_t = _rope_tables(ctx_len, seq)
    seg = (np.arange(RW_WIDTH)[:, None] // HEAD_DIM == np.arange(LANE)[None, :]).astype(np.float32)
    consts = dict(cos=cos_t, sin=sin_t, seg=jnp.asarray(seg, BF16), seg_t=jnp.asarray(seg.T, BF16))
    tables = _na_tables(seq // GRID_W)

    o_na = RW_COLS_PAD
    o_sc = o_na + 3 * NA_WIDTH
    o_gate = o_sc + 3 * SC_WIDTH
    pad_cols = RW_COLS_PAD - RW_COLS

    xs = jnp.concatenate([ctx, x], axis=1).reshape(n, d)
    h = _lnmod(xs, mods3[0], nbb)
    for l in range(depth):
        w_in_p = jnp.concatenate([w_in[l, :, :RW_COLS], jnp.zeros((d, pad_cols), F32), w_in[l, :, RW_COLS:]],
                                 axis=1).astype(BF16)
        lw = dict(
            mu=jnp.pad(rw_mu[l], ((0, 0), (0, pad_cols))), w0=rw_w0[l], w_lora=_pad_dir_lora(rw_w_lora[l]),
            a0=rw_a0[l], a_lora=_pad_dir_lora(rw_a_lora[l]),
            g_lora=_split_weight(jnp.pad(rw_g_lora[l], ((0, GATE_LORA_PAD - GATE_LORA), (0, 0)))),
            k_k=rw_k_k[l].reshape(1, -1), k_a=rw_k_a[l].reshape(1, -1), r_k=rw_r_k[l].reshape(1, -1),
            gn_w=rw_gn_w[l].reshape(1, -1), gn_b=rw_gn_b[l].reshape(1, -1),
            w_out_rw=w_out_rw[l].astype(BF16), w_out_na=w_out_na[l].astype(BF16),
            w_out_sc=w_out_sc[l].astype(BF16), b_gate=b_gate[l].reshape(1, -1),
            w_merge=w_merge[l].astype(BF16), ln1_g=ln1_g[l].reshape(1, -1), ln1_b=ln1_b[l].reshape(1, -1),
            router_w=_split_weight(jnp.pad(router_w[l], ((0, 0), (0, LANE - N_EXPERTS)))),
            router_b=jnp.pad(router_b[l], (0, LANE - N_EXPERTS)).reshape(1, -1),
            moe_row0=l * N_EXPERTS,
            moe_w1=moe_w1.reshape((depth * N_EXPERTS,) + moe_w1.shape[2:]),
            moe_b1=moe_b1.reshape(depth * N_EXPERTS, 1, -1),
            moe_w2=moe_w2.reshape((depth * N_EXPERTS,) + moe_w2.shape[2:]),
            moe_b2=moe_b2.reshape(depth * N_EXPERTS, 1, -1),
            ln2_g=ln2_g[l].reshape(1, -1), ln2_b=ln2_b[l].reshape(1, -1),
        )
        proj = _matmul(h, w_in_p)
        r_s, kap, v, w, kd, b, bonus, g = _rwkv_prepare(proj, lw, consts, nbb)
        tl = functools.partial(_to_scan_layout, batch=batch, t_len=t_len)
        wkv_f, wkv_b = _wkv_scan(tl(r_s), tl(kap), tl(v), tl(w), tl(kd), tl(b), ctx_len)
        y_rw = _rwkv_out(wkv_f.reshape(n, RW_WIDTH), wkv_b.reshape(n, RW_WIDTH), bonus, g, lw, consts)
        y_na = _na_attention(proj, _na_bias(na_rpb[l]), tables, o_na, batch, t_len, ctx_len)
        y_sc = _short_conv(proj, jnp.pad(sc_conv[l], ((0, 5), (0, 0))), o_sc, nbb)
        m = _merge1(y_rw, y_na, y_sc, lw, proj, o_gate, d)
        x1, h2, logits = _merge2(m, xs, mods3[l], lw, nbb, alpha)
        gate, idx, rank, counts = _route(logits)
        y_rows = _moe(h2, idx, rank, counts, lw)
        xs, h = _final(x1, y_rows, gate, mods3[l], mods3[min(l + 1, depth - 1)], lw, nbb, alpha)
    return xs.reshape(batch, t_len, d)[:, ctx_len:]
```
